```python
import math
import jax, jax.numpy as jnp
from jax import lax
import numpy as np

D_MODEL = 1024
BATCH = 8
SEQ = 2048
DEPTH = 2

D_FF = 2816
RMS_EPS = 1e-6
LN_EPS = 1e-5
Q_BLOCK = 128
N_BRANCH = 3
DIFF_HEADS = 4
DIFF_HEAD_DIM = 64
DIFF_V_DIM = 2 * DIFF_HEAD_DIM
DIFF_QK_WIDTH = DIFF_HEADS * 2 * DIFF_HEAD_DIM
DIFF_V_WIDTH = DIFF_HEADS * DIFF_V_DIM
FOX_HEADS = 8
FOX_HEAD_DIM = 64
FOX_WIDTH = FOX_HEADS * FOX_HEAD_DIM
MLSTM_HEADS = 4
MLSTM_HEAD_DIM = 128
MLSTM_WIDTH = MLSTM_HEADS * MLSTM_HEAD_DIM
MLSTM_CONV = 4
MLSTM_CHUNK = 128
SPLIT_SIZES = (DIFF_QK_WIDTH, DIFF_QK_WIDTH, DIFF_V_WIDTH,
               FOX_WIDTH, FOX_WIDTH, FOX_WIDTH, FOX_HEADS,
               MLSTM_WIDTH, MLSTM_WIDTH,
               N_BRANCH * D_MODEL)
N_IN = sum(SPLIT_SIZES)

kernel_name = "hybrid_diffattn_fox_mlstm_macaron"


def rms_norm(x, w):
    x32 = x.astype(jnp.float32)
    y = x32 * lax.rsqrt(jnp.mean(x32 * x32, axis=-1, keepdims=True) + RMS_EPS)
    return (y * w.astype(jnp.float32)).astype(x.dtype)


def swiglu(x, w_gate, w_up, w_down):
    return (jax.nn.silu(x @ w_gate) * (x @ w_up)) @ w_down


def split_columns(p):
    offsets = []
    acc = 0
    for s in SPLIT_SIZES[:-1]:
        acc += s
        offsets.append(acc)
    return jnp.split(p, offsets, axis=-1)


def alibi_slopes(n):
    return np.array([2.0 ** (-8.0 * (h + 1) / n) for h in range(n)], dtype=np.float32)


def causal_block_attention(q, k, v, bias_fn):
    B, H, S, dk = q.shape
    dv = v.shape[-1]
    nb = S // Q_BLOCK
    qb = q.reshape(B, H, nb, Q_BLOCK, dk).transpose(2, 0, 1, 3, 4)
    k_pos = jnp.arange(S)
    scale = dk ** -0.5

    def one_block(args):
        blk, q_i = args
        q_pos = blk * Q_BLOCK + jnp.arange(Q_BLOCK)
        logits = jnp.einsum('bhtd,bhsd->bhts', q_i, k).astype(jnp.float32) * scale
        logits = logits + bias_fn(blk, q_pos, k_pos)
        logits = jnp.where(k_pos[None, :] <= q_pos[:, None], logits, -jnp.inf)
        p = jax.nn.softmax(logits, axis=-1).astype(v.dtype)
        return jnp.einsum('bhts,bhse->bhte', p, v)

    out = lax.map(one_block, (jnp.arange(nb), qb))
    return out.transpose(1, 2, 0, 3, 4).reshape(B, H, S, dv)


def causal_depthwise_conv(x, w, b):
    C = x.shape[-1]
    y = lax.conv_general_dilated(x, w[:, None, :].astype(x.dtype), window_strides=(1,),
                                 padding=[(w.shape[0] - 1, 0)],
                                 dimension_numbers=('NWC', 'WIO', 'NWC'),
                                 feature_group_count=C)
    return y + b


def mlstm_chunkwise(q, k, v, log_i, log_f):
    B, H, S, dk = q.shape
    dv = v.shape[-1]
    L = MLSTM_CHUNK
    nc = S // L
    f32 = jnp.float32

    def chunks(a):
        return a.astype(f32).reshape(B, H, nc, L, a.shape[-1]).transpose(2, 0, 1, 3, 4)

    def gchunks(a):
        return a.astype(f32).reshape(B, H, nc, L).transpose(2, 0, 1, 3)

    qc, kc, vc = chunks(q), chunks(k * dk ** -0.5), chunks(v)
    lic, lfc = gchunks(log_i), gchunks(log_f)
    causal = jnp.tril(jnp.ones((L, L), dtype=bool))

    def step(carry, xs):
        C, n, m = carry
        q_c, k_c, v_c, li, lf = xs
        b = jnp.cumsum(lf, axis=-1)
        b_last = b[..., -1]
        D = b[..., :, None] - b[..., None, :] + li[..., None, :]
        D = jnp.where(causal, D, -jnp.inf)
        inter = b + m[..., None]
        m_t = jnp.maximum(inter, jnp.max(D, axis=-1))
        w_intra = jnp.exp(D - m_t[..., None])
        w_inter = jnp.exp(inter - m_t)
        s = jnp.einsum('bhtd,bhsd->bhts', q_c, k_c) * w_intra
        num = w_inter[..., None] * jnp.einsum('bhtd,bhde->bhte', q_c, C) + jnp.einsum('bhts,bhse->bhte', s, v_c)
        den = w_inter * jnp.einsum('bhtd,bhd->bht', q_c, n) + jnp.sum(s, axis=-1)
        h = num / jnp.maximum(jnp.abs(den), jnp.exp(-m_t))[..., None]
        g = b_last[..., None] - b + li
        m_new = jnp.maximum(b_last + m, jnp.max(g, axis=-1))
        wk = jnp.exp(g - m_new[..., None])
        dec = jnp.exp(b_last + m - m_new)
        C_new = dec[..., None, None] * C + jnp.einsum('bhs,bhsd,bhse->bhde', wk, k_c, v_c)
        n_new = dec[..., None] * n + jnp.einsum('bhs,bhsd->bhd', wk, k_c)
        return (C_new, n_new, m_new), h

    init = (jnp.zeros((B, H, dk, dv), f32), jnp.zeros((B, H, dk), f32), jnp.zeros((B, H), f32))
    _, hs = lax.scan(step, init, (qc, kc, vc, lic, lfc))
    return hs.transpose(1, 2, 0, 3, 4).reshape(B, H, S, dv)


def hybrid_mixer(h, layer_idx, w_in, gate_bias,
                 diff_lq1, diff_lk1, diff_lq2, diff_lk2, diff_subln,
                 fox_b_f,
                 mlstm_conv_w, mlstm_conv_b, mlstm_wq, mlstm_wk, mlstm_wv, mlstm_w_if, mlstm_b_if,
                 mlstm_norm, mlstm_skip,
                 w_branch_diff, w_branch_fox, w_branch_mlstm, w_out):
    B, S, _ = h.shape
    f32 = jnp.float32
    proj = h @ w_in
    d_q, d_k, d_v, f_q, f_k, f_v, f_f, m_x, m_z, g_pre = split_columns(proj)

    dq = d_q.reshape(B, S, DIFF_HEADS, 2, DIFF_HEAD_DIM)
    dk_ = d_k.reshape(B, S, DIFF_HEADS, 2, DIFF_HEAD_DIM)
    q1, q2 = dq[..., 0, :].transpose(0, 2, 1, 3), dq[..., 1, :].transpose(0, 2, 1, 3)
    k1, k2 = dk_[..., 0, :].transpose(0, 2, 1, 3), dk_[..., 1, :].transpose(0, 2, 1, 3)
    dv = d_v.reshape(B, S, DIFF_HEADS, DIFF_V_DIM).transpose(0, 2, 1, 3)
    slopes = jnp.asarray(alibi_slopes(DIFF_HEADS))

    def alibi_bias(blk, q_pos, k_pos):
        dist = (q_pos[:, None] - k_pos[None, :]).astype(f32)
        return -(slopes[:, None, None] * dist[None])[None]

    a1 = causal_block_attention(q1, k1, dv, alibi_bias).astype(f32)
    a2 = causal_block_attention(q2, k2, dv, alibi_bias).astype(f32)
    lam_init = 0.8 - 0.6 * math.exp(-0.3 * layer_idx)
    lam = (jnp.exp(jnp.sum(diff_lq1.astype(f32) * diff_lk1.astype(f32)))
           - jnp.exp(jnp.sum(diff_lq2.astype(f32) * diff_lk2.astype(f32))) + lam_init)
    o_a = rms_norm(a1 - lam * a2, diff_subln) * (1.0 - lam_init)
    o_diff = o_a.transpose(0, 2, 1, 3).reshape(B, S, DIFF_V_WIDTH).astype(h.dtype)

    fq = f_q.reshape(B, S, FOX_HEADS, FOX_HEAD_DIM).transpose(0, 2, 1, 3)
    fk = f_k.reshape(B, S, FOX_HEADS, FOX_HEAD_DIM).transpose(0, 2, 1, 3)
    fv = f_v.reshape(B, S, FOX_HEADS, FOX_HEAD_DIM).transpose(0, 2, 1, 3)
    log_fg = jax.nn.log_sigmoid(f_f.astype(f32) + fox_b_f.astype(f32))
    F = jnp.cumsum(log_fg, axis=1).transpose(0, 2, 1)

    def fox_bias(blk, q_pos, k_pos):
        F_q = lax.dynamic_slice_in_dim(F, blk * Q_BLOCK, Q_BLOCK, axis=2)
        return F_q[..., :, None] - F[..., None, :]

    o_f = causal_block_attention(fq, fk, fv, fox_bias)
    o_fox = o_f.transpose(0, 2, 1, 3).reshape(B, S, FOX_WIDTH)

    x_c = jax.nn.silu(causal_depthwise_conv(m_x, mlstm_conv_w, mlstm_conv_b))
    xh = x_c.reshape(B, S, MLSTM_HEADS, MLSTM_HEAD_DIM)
    vh = m_x.reshape(B, S, MLSTM_HEADS, MLSTM_HEAD_DIM)
    mq = jnp.einsum('bshd,hde->bshe', xh, mlstm_wq)
    mk = jnp.einsum('bshd,hde->bshe', xh, mlstm_wk)
    mv = jnp.einsum('bshd,hde->bshe', vh, mlstm_wv)
    qkv_flat = jnp.concatenate([mq.reshape(B, S, MLSTM_WIDTH), mk.reshape(B, S, MLSTM_WIDTH),
                                mv.reshape(B, S, MLSTM_WIDTH)], axis=-1)
    if_pre = (qkv_flat @ mlstm_w_if + mlstm_b_if).astype(f32)
    log_i = if_pre[..., :MLSTM_HEADS].transpose(0, 2, 1)
    log_f = jax.nn.log_sigmoid(if_pre[..., MLSTM_HEADS:]).transpose(0, 2, 1)
    hm = mlstm_chunkwise(mq.transpose(0, 2, 1, 3), mk.transpose(0, 2, 1, 3),
                         mv.transpose(0, 2, 1, 3), log_i, log_f)
    mu = jnp.mean(hm, axis=-1, keepdims=True)
    var = jnp.mean(jnp.square(hm - mu), axis=-1, keepdims=True)
    hm = (hm - mu) * lax.rsqrt(var + LN_EPS)
    hm = hm.transpose(0, 2, 1, 3).reshape(B, S, MLSTM_WIDTH) * mlstm_norm.astype(f32)
    o_mlstm = ((hm.astype(h.dtype) + mlstm_skip * x_c) * jax.nn.silu(m_z))

    gates = jax.nn.sigmoid(g_pre.reshape(B, S, N_BRANCH, D_MODEL) + gate_bias)
    merged = (gates[..., 0, :] * (o_diff @ w_branch_diff)
              + gates[..., 1, :] * (o_fox @ w_branch_fox)
              + gates[..., 2, :] * (o_mlstm @ w_branch_mlstm))
    return merged @ w_out


def setup_inputs(seed: int = 0) -> dict:
    key = jax.random.key(seed)
    ks = iter(jax.random.split(key, 40))
    f32 = jnp.float32

    def w(shape, fan_in):
        return jax.random.normal(next(ks), shape, f32) * (fan_in ** -0.5)

    def gain(shape):
        return 1.0 + 0.02 * jax.random.normal(next(ks), shape, f32)

    def small(shape, s=0.01):
        return s * jax.random.normal(next(ks), shape, f32)

    L = DEPTH
    fox_b = jnp.tile(jnp.linspace(1.0, 4.0, FOX_HEADS, dtype=f32)[None], (L, 1)) + small((L, FOX_HEADS), 0.1)
    b_i = small((L, MLSTM_HEADS), 0.1)
    b_f = jnp.tile(jnp.linspace(3.0, 6.0, MLSTM_HEADS, dtype=f32)[None], (L, 1)) + small((L, MLSTM_HEADS), 0.1)
    return {
        "x": jax.random.normal(next(ks), (BATCH, SEQ, D_MODEL), f32),
        "ffn1_norm": gain((L, D_MODEL)),
        "ffn1_w_gate": w((L, D_MODEL, D_FF), D_MODEL),
        "ffn1_w_up": w((L, D_MODEL, D_FF), D_MODEL),
        "ffn1_w_down": w((L, D_FF, D_MODEL), D_FF),
        "mix_norm": gain((L, D_MODEL)),
        "w_in": w((L, D_MODEL, N_IN), D_MODEL),
        "gate_bias": small((L, N_BRANCH, D_MODEL)),
        "diff_lq1": small((L, DIFF_HEAD_DIM), 0.1),
        "diff_lk1": small((L, DIFF_HEAD_DIM), 0.1),
        "diff_lq2": small((L, DIFF_HEAD_DIM), 0.1),
        "diff_lk2": small((L, DIFF_HEAD_DIM), 0.1),
        "diff_subln": gain((L, DIFF_V_DIM)),
        "fox_b_f": fox_b,
        "mlstm_conv_w": w((L, MLSTM_CONV, MLSTM_WIDTH), MLSTM_CONV),
        "mlstm_conv_b": small((L, MLSTM_WIDTH)),
        "mlstm_wq": w((L, MLSTM_HEADS, MLSTM_HEAD_DIM, MLSTM_HEAD_DIM), MLSTM_HEAD_DIM),
        "mlstm_wk": w((L, MLSTM_HEADS, MLSTM_HEAD_DIM, MLSTM_HEAD_DIM), MLSTM_HEAD_DIM),
        "mlstm_wv": w((L, MLSTM_HEADS, MLSTM_HEAD_DIM, MLSTM_HEAD_DIM), MLSTM_HEAD_DIM),
        "mlstm_w_if": w((L, 3 * MLSTM_WIDTH, 2 * MLSTM_HEADS), 3 * MLSTM_WIDTH),
        "mlstm_b_if": jnp.concatenate([b_i, b_f], axis=-1),
        "mlstm_norm": gain((L, MLSTM_WIDTH)),
        "mlstm_skip": gain((L, MLSTM_WIDTH)),
        "w_branch_diff": w((L, DIFF_V_WIDTH, D_MODEL), DIFF_V_WIDTH),
        "w_branch_fox": w((L, FOX_WIDTH, D_MODEL), FOX_WIDTH),
        "w_branch_mlstm": w((L, MLSTM_WIDTH, D_MODEL), MLSTM_WIDTH),
        "w_out": w((L, D_MODEL, D_MODEL), D_MODEL),
        "ffn2_norm": gain((L, D_MODEL)),
        "ffn2_w_gate": w((L, D_MODEL, D_FF), D_MODEL),
        "ffn2_w_up": w((L, D_MODEL, D_FF), D_MODEL),
        "ffn2_w_down": w((L, D_FF, D_MODEL), D_FF),
        "final_norm": gain((D_MODEL,)),
    }


def reference(x, ffn1_norm, ffn1_w_gate, ffn1_w_up, ffn1_w_down,
              mix_norm, w_in, gate_bias,
              diff_lq1, diff_lk1, diff_lq2, diff_lk2, diff_subln,
              fox_b_f,
              mlstm_conv_w, mlstm_conv_b, mlstm_wq, mlstm_wk, mlstm_wv, mlstm_w_if, mlstm_b_if,
              mlstm_norm, mlstm_skip,
              w_branch_diff, w_branch_fox, w_branch_mlstm, w_out,
              ffn2_norm, ffn2_w_gate, ffn2_w_up, ffn2_w_down,
              final_norm):
    for l in range(DEPTH):
        x = x + 0.5 * swiglu(rms_norm(x, ffn1_norm[l]), ffn1_w_gate[l], ffn1_w_up[l], ffn1_w_down[l])
        x = x + hybrid_mixer(rms_norm(x, mix_norm[l]), l, w_in[l], gate_bias[l],
                             diff_lq1[l], diff_lk1[l], diff_lq2[l], diff_lk2[l], diff_subln[l],
                             fox_b_f[l],
                             mlstm_conv_w[l], mlstm_conv_b[l], mlstm_wq[l], mlstm_wk[l], mlstm_wv[l],
                             mlstm_w_if[l], mlstm_b_if[l], mlstm_norm[l], mlstm_skip[l],
                             w_branch_diff[l], w_branch_fox[l], w_branch_mlstm[l], w_out[l])
        x = x + 0.5 * swiglu(rms_norm(x, ffn2_norm[l]), ffn2_w_gate[l], ffn2_w_up[l], ffn2_w_down[l])
    return rms_norm(x, final_norm)
```

```python
import functools
import math

import numpy as np
import jax
import jax.numpy as jnp
from jax import lax
from jax.experimental import pallas as pl
from jax.experimental.pallas import tpu as pltpu

D_MODEL = 1024
D_FF = 2816
RMS_EPS = 1e-6
LN_EPS = 1e-5
N_BRANCH = 3
DIFF_HEADS = 4
DIFF_HEAD_DIM = 64
FOX_HEADS = 8
FOX_HEAD_DIM = 64
MLSTM_HEADS = 4
MLSTM_HEAD_DIM = 128
MLSTM_WIDTH = MLSTM_HEADS * MLSTM_HEAD_DIM
MLSTM_CONV = 4
MLSTM_CHUNK = 128
ATTN_WIDTH = 512
QKV_WIDTH = 6 * ATTN_WIDTH
LANES = 128
SUBLANES = 8
VMEM_LIMIT_BYTES = 56 * 1024 * 1024

F32 = jnp.float32
BF16 = jnp.bfloat16
NT_DIMS = (((1,), (1,)), ((), ()))
TN_DIMS = (((0,), (0,)), ((), ()))


def _params(*semantics):
    return pltpu.CompilerParams(dimension_semantics=semantics, vmem_limit_bytes=VMEM_LIMIT_BYTES)


def _rms_norm(x, w):
    return x * lax.rsqrt(jnp.mean(x * x, axis=-1, keepdims=True) + RMS_EPS) * w


def _silu(x):
    return x * jax.nn.sigmoid(x)


def _log_sigmoid(x):
    return jnp.minimum(x, 0.0) - jnp.log1p(jnp.exp(-jnp.abs(x)))


def _lane_cumsum(x):
    n = x.shape[-1]
    lane = lax.broadcasted_iota(jnp.int32, x.shape, 1)
    shift = 1
    while shift < n:
        x = x + jnp.where(lane >= shift, pltpu.roll(x, shift, axis=1), 0.0)
        shift *= 2
    return x


def _ffn_kernel(*refs, final):
    if final:
        x_ref, nw_ref, wg_ref, wu_ref, wd_ref, fn_ref, o_ref, xn_ref, acc_ref = refs
    else:
        x_ref, nw_ref, wg_ref, wu_ref, wd_ref, o_ref, xn_ref, acc_ref = refs
    f = pl.program_id(1)

    @pl.when(f == 0)
    def _():
        xn_ref[...] = _rms_norm(x_ref[...], nw_ref[...]).astype(BF16)
        acc_ref[...] = jnp.zeros_like(acc_ref)

    xn = xn_ref[...]
    g = jnp.dot(xn, wg_ref[...], preferred_element_type=F32)
    u = jnp.dot(xn, wu_ref[...], preferred_element_type=F32)
    h = (_silu(g) * u).astype(BF16)
    acc_ref[...] += jnp.dot(h, wd_ref[...], preferred_element_type=F32)

    @pl.when(f == pl.num_programs(1) - 1)
    def _():
        y = x_ref[...] + 0.5 * acc_ref[...]
        if final:
            y = _rms_norm(y, fn_ref[...])
        o_ref[...] = y


def _ffn(x, norm_w, w_gate, w_up, w_down, final_norm=None, *, tm=1024, tf=256):
    m, d = x.shape
    dff = w_gate.shape[1]
    final = final_norm is not None
    in_specs = [
        pl.BlockSpec((tm, d), lambda i, f: (i, 0)),
        pl.BlockSpec((1, d), lambda i, f: (0, 0)),
        pl.BlockSpec((d, tf), lambda i, f: (0, f)),
        pl.BlockSpec((d, tf), lambda i, f: (0, f)),
        pl.BlockSpec((tf, d), lambda i, f: (f, 0)),
    ]
    args = [x, norm_w, w_gate, w_up, w_down]
    if final:
        in_specs.append(pl.BlockSpec((1, d), lambda i, f: (0, 0)))
        args.append(final_norm)
    return pl.pallas_call(
        functools.partial(_ffn_kernel, final=final),
        grid=(m // tm, dff // tf),
        in_specs=in_specs,
        out_specs=pl.BlockSpec((tm, d), lambda i, f: (i, 0)),
        out_shape=jax.ShapeDtypeStruct((m, d), F32),
        scratch_shapes=[pltpu.VMEM((tm, d), BF16), pltpu.VMEM((tm, d), F32)],
        compiler_params=_params("parallel", "arbitrary"),
        name="ffn_final" if final else "ffn",
    )(*args)


def _proj_kernel(x_ref, nw_ref, wqkv_ref, wm_ref, wff_ref, bf_ref,
                 qkv_ref, mxz_ref, cf_ref, carry_ref, *, tiles_per_seq, tn):
    i = pl.program_id(0)
    xn = _rms_norm(x_ref[...], nw_ref[...]).astype(BF16)
    for c in range(0, qkv_ref.shape[1], tn):
        qkv_ref[:, c:c + tn] = jnp.dot(
            xn, wqkv_ref[:, c:c + tn], preferred_element_type=F32).astype(BF16)
    for c in range(0, mxz_ref.shape[1], tn):
        mxz_ref[:, c:c + tn] = jnp.dot(xn, wm_ref[:, c:c + tn], preferred_element_type=F32)

    ff = lax.dot_general(wff_ref[...], xn, NT_DIMS, preferred_element_type=F32)
    csum = _lane_cumsum(_log_sigmoid(ff + bf_ref[...]))

    @pl.when(i % tiles_per_seq == 0)
    def _():
        carry_ref[...] = jnp.zeros_like(carry_ref)

    csum = csum + carry_ref[:, 0:1]
    carry_ref[...] = jnp.broadcast_to(csum[:, -1:], carry_ref.shape)
    cf_ref[...] = -csum


def _proj(x, norm_w, w_qkv, w_m, w_ff_t, b_f, *, seq, tm=512, tn=512):
    m, d = x.shape
    nh = w_ff_t.shape[0]
    return pl.pallas_call(
        functools.partial(_proj_kernel, tiles_per_seq=seq // tm, tn=tn),
        grid=(m // tm,),
        in_specs=[
            pl.BlockSpec((tm, d), lambda i: (i, 0)),
            pl.BlockSpec((1, d), lambda i: (0, 0)),
            pl.BlockSpec(w_qkv.shape, lambda i: (0, 0)),
            pl.BlockSpec(w_m.shape, lambda i: (0, 0)),
            pl.BlockSpec(w_ff_t.shape, lambda i: (0, 0)),
            pl.BlockSpec((nh, 1), lambda i: (0, 0)),
        ],
        out_specs=[
            pl.BlockSpec((tm, w_qkv.shape[1]), lambda i: (i, 0)),
            pl.BlockSpec((tm, w_m.shape[1]), lambda i: (i, 0)),
            pl.BlockSpec((nh, tm), lambda i: (0, i)),
        ],
        out_shape=[
            jax.ShapeDtypeStruct((m, w_qkv.shape[1]), BF16),
            jax.ShapeDtypeStruct((m, w_m.shape[1]), F32),
            jax.ShapeDtypeStruct((nh, m), F32),
        ],
        scratch_shapes=[pltpu.VMEM((nh, LANES), F32)],
        compiler_params=_params("arbitrary"),
        name="proj",
    )(x, norm_w, w_qkv, w_m, w_ff_t, b_f)


def _flash_pair(q_ref, k_ref, v_ref, bias_ref, m_ref, l_ref, acc_ref, *, tq):
    qi = pl.program_id(2)
    q_start = pl.multiple_of(qi * tq, tq)
    q = q_ref[0]
    lane = lax.broadcasted_iota(jnp.int32, q.shape, 1)
    half = LANES // 2
    scale = jnp.asarray(DIFF_HEAD_DIM ** -0.5, q.dtype)
    qs = q * scale
    qm = (jnp.where(lane < half, qs, jnp.zeros_like(qs)),
          jnp.where(lane >= half, qs, jnp.zeros_like(qs)))
    cref = [bias_ref[0, c:c + 1, pl.ds(q_start, LANES)][:, 0:1] for c in range(2)]

    m_ref[...] = jnp.full_like(m_ref, -jnp.inf)
    l_ref[...] = jnp.zeros_like(l_ref)
    acc_ref[...] = jnp.zeros_like(acc_ref)

    def step(k_start, masked):
        k = k_ref[0, pl.ds(k_start, tq), :]
        v = v_ref[0, pl.ds(k_start, tq), :]
        if masked:
            row = lax.broadcasted_iota(jnp.int32, (tq, tq), 0)
            col = lax.broadcasted_iota(jnp.int32, (tq, tq), 1)
            keep = col <= row
        for c in range(2):
            s = lax.dot_general(qm[c], k, NT_DIMS, preferred_element_type=F32)
            s = s + (bias_ref[0, c:c + 1, pl.ds(k_start, tq)] - cref[c])
            if masked:
                s = jnp.where(keep, s, -jnp.inf)
            m_old = m_ref[c]
            m_new = jnp.maximum(m_old, jnp.max(s, axis=1, keepdims=True))
            alpha = jnp.exp(m_old - m_new)
            p = jnp.exp(s - m_new)
            l_ref[c] = alpha * l_ref[c] + jnp.sum(p, axis=1, keepdims=True)
            acc_ref[c] = alpha * acc_ref[c] + jnp.dot(p.astype(BF16), v, preferred_element_type=F32)
            m_ref[c] = m_new

    def body(j, carry):
        step(pl.multiple_of(j * tq, tq), masked=False)
        return carry

    lax.fori_loop(0, qi, body, 0)
    step(q_start, masked=True)
    return acc_ref[0] / l_ref[0], acc_ref[1] / l_ref[1]


def _diff_attn_kernel(q_ref, k_ref, v_ref, bias_ref, lq1_ref, lk1_ref, lq2_ref, lk2_ref, sub_ref,
                      o_ref, m_ref, l_ref, acc_ref, *, tq, lam_init):
    a1, a2 = _flash_pair(q_ref, k_ref, v_ref, bias_ref, m_ref, l_ref, acc_ref, tq=tq)
    lam = (jnp.exp(jnp.sum(lq1_ref[...] * lk1_ref[...], axis=1, keepdims=True))
           - jnp.exp(jnp.sum(lq2_ref[...] * lk2_ref[...], axis=1, keepdims=True)) + lam_init)
    d = a1 - lam * a2
    o_ref[0] = (_rms_norm(d, sub_ref[...]) * (1.0 - lam_init)).astype(o_ref.dtype)


def _fox_attn_kernel(q_ref, k_ref, v_ref, bias_ref, o_ref, m_ref, l_ref, acc_ref, *, tq):
    a_even, a_odd = _flash_pair(q_ref, k_ref, v_ref, bias_ref, m_ref, l_ref, acc_ref, tq=tq)
    lane = lax.broadcasted_iota(jnp.int32, a_even.shape, 1)
    o_ref[0] = jnp.where(lane < LANES // 2, a_even, a_odd).astype(o_ref.dtype)


def _attention(kernel, qkv, bias, extra, *, col0, bias_per_batch, tq=256, name):
    b, s, _ = qkv.shape
    groups = ATTN_WIDTH // LANES
    if bias_per_batch:
        bias_map = lambda bi, g, i: (g, 0, bi)
    else:
        bias_map = lambda bi, g, i: (g, 0, 0)
    in_specs = [
        pl.BlockSpec((1, tq, LANES), lambda bi, g, i: (bi, i, col0 + g)),
        pl.BlockSpec((1, s, LANES), lambda bi, g, i: (bi, 0, col0 + groups + g)),
        pl.BlockSpec((1, s, LANES), lambda bi, g, i: (bi, 0, col0 + 2 * groups + g)),
        pl.BlockSpec((1, 2, s), bias_map),
    ] + [pl.BlockSpec(e.shape, lambda bi, g, i: (0, 0)) for e in extra]
    return pl.pallas_call(
        functools.partial(kernel, tq=tq),
        grid=(b, groups, s // tq),
        in_specs=in_specs,
        out_specs=pl.BlockSpec((1, tq, LANES), lambda bi, g, i: (bi, i, g)),
        out_shape=jax.ShapeDtypeStruct((b, s, ATTN_WIDTH), BF16),
        scratch_shapes=[pltpu.VMEM((2, tq, 1), F32), pltpu.VMEM((2, tq, 1), F32),
                        pltpu.VMEM((2, tq, LANES), F32)],
        compiler_params=_params("parallel", "parallel", "arbitrary"),
        name=name,
    )(qkv, qkv, qkv, bias, *extra)


def _mlstm_kernel(cur_ref, prev_ref, z_ref, cw_ref, cb_ref, wq_ref, wk_ref, wv_ref, wif_ref, bif_ref,
                  nrm_ref, skip_ref, o_ref, xbuf_ref, c_ref, n_ref, m_ref):
    ci = pl.program_id(1)
    L = MLSTM_CHUNK
    dh = MLSTM_HEAD_DIM
    nh = MLSTM_HEADS

    @pl.when(ci == 0)
    def _():
        c_ref[...] = jnp.zeros_like(c_ref)
        n_ref[...] = jnp.zeros_like(n_ref)
        m_ref[...] = jnp.zeros_like(m_ref)

    cur = cur_ref[0]
    xbuf_ref[0:SUBLANES, :] = jnp.where(ci > 0, prev_ref[0], 0.0)
    xbuf_ref[SUBLANES:, :] = cur
    conv = cb_ref[...]
    for j in range(MLSTM_CONV):
        off = SUBLANES - (MLSTM_CONV - 1) + j
        conv = conv + cw_ref[j:j + 1, :] * xbuf_ref[off:off + L, :]
    xc = _silu(conv)

    qs, ks, vs = [], [], []
    ift = bif_ref[...]
    for h in range(nh):
        sl = slice(h * dh, (h + 1) * dh)
        xh = xc[:, sl].astype(BF16)
        vh = cur[:, sl].astype(BF16)
        qs.append(jnp.dot(xh, wq_ref[h], preferred_element_type=F32).astype(BF16))
        ks.append(jnp.dot(xh, wk_ref[h], preferred_element_type=F32))
        vs.append(jnp.dot(vh, wv_ref[h], preferred_element_type=F32).astype(BF16))
    for g, group in enumerate((qs, ks, vs)):
        for h in range(nh):
            col = (g * nh + h) * dh
            ift = ift + lax.dot_general(wif_ref[:, col:col + dh], group[h].astype(BF16), NT_DIMS,
                                        preferred_element_type=F32)
    li_rows = ift[0:nh]
    b_rows = _lane_cumsum(_log_sigmoid(ift))[nh:2 * nh]

    row = lax.broadcasted_iota(jnp.int32, (L, L), 0)
    col = lax.broadcasted_iota(jnp.int32, (L, L), 1)
    eye = row == col
    causal = col <= row

    def to_col(r):
        return jnp.sum(jnp.where(eye, r, 0.0), axis=1, keepdims=True)

    for h in range(nh):
        sl = slice(h * dh, (h + 1) * dh)
        li = li_rows[h:h + 1]
        b = b_rows[h:h + 1]
        li_col = to_col(li)
        b_col = to_col(b)
        b_last = b[:, L - 1:L]
        m_prev = m_ref[h:h + 1, 0:1]
        q = qs[h]
        k = ks[h] * (dh ** -0.5)
        v = vs[h]
        c_state = c_ref[h]
        n_state = n_ref[h]

        d = jnp.where(causal, b_col - b + li, -jnp.inf)
        inter = b_col + m_prev
        m_t = jnp.maximum(inter, jnp.max(d, axis=1, keepdims=True))
        w_intra = jnp.exp(d - m_t)
        w_inter = jnp.exp(inter - m_t)
        s = lax.dot_general(q, k.astype(BF16), NT_DIMS, preferred_element_type=F32) * w_intra
        num = (w_inter * jnp.dot(q, c_state.astype(BF16), preferred_element_type=F32)
               + jnp.dot(s.astype(BF16), v, preferred_element_type=F32))
        den = (w_inter * jnp.sum(q.astype(F32) * n_state, axis=1, keepdims=True)
               + jnp.sum(s, axis=1, keepdims=True))
        hh = num / jnp.maximum(jnp.abs(den), jnp.exp(-m_t))

        m_new = jnp.maximum(b_last + m_prev, jnp.max(b_last - b + li, axis=1, keepdims=True))
        wk_col = jnp.exp(b_last - b_col + li_col - m_new)
        dec = jnp.exp(b_last + m_prev - m_new)
        kw = k * wk_col
        c_ref[h] = dec * c_state + lax.dot_general(kw.astype(BF16), v, TN_DIMS,
                                                   preferred_element_type=F32)
        n_ref[h] = dec * n_state + jnp.sum(kw, axis=0, keepdims=True)
        m_ref[h:h + 1, :] = jnp.broadcast_to(m_new, (1, LANES))

        mu = jnp.mean(hh, axis=1, keepdims=True)
        cen = hh - mu
        var = jnp.mean(cen * cen, axis=1, keepdims=True)
        hn = cen * lax.rsqrt(var + LN_EPS) * nrm_ref[:, sl]
        o_ref[0, :, sl] = ((hn + skip_ref[:, sl] * xc[:, sl]) * _silu(z_ref[0, :, sl])).astype(o_ref.dtype)


def _mlstm(mxz, conv_w, conv_b, wq, wk, wv, w_if_t, b_if, norm_w, skip_w):
    b, s, _ = mxz.shape
    L = MLSTM_CHUNK
    w = MLSTM_WIDTH
    rows_per_chunk = L // SUBLANES
    const2 = lambda bi, c: (0, 0)
    const3 = lambda bi, c: (0, 0, 0)
    return pl.pallas_call(
        _mlstm_kernel,
        grid=(b, s // L),
        in_specs=[
            pl.BlockSpec((1, L, w), lambda bi, c: (bi, c, 0)),
            pl.BlockSpec((1, SUBLANES, w), lambda bi, c: (bi, jnp.maximum(c * rows_per_chunk - 1, 0), 0)),
            pl.BlockSpec((1, L, w), lambda bi, c: (bi, c, 1)),
            pl.BlockSpec(conv_w.shape, const2),
            pl.BlockSpec(conv_b.shape, const2),
            pl.BlockSpec(wq.shape, const3),
            pl.BlockSpec(wk.shape, const3),
            pl.BlockSpec(wv.shape, const3),
            pl.BlockSpec(w_if_t.shape, const2),
            pl.BlockSpec(b_if.shape, const2),
            pl.BlockSpec(norm_w.shape, const2),
            pl.BlockSpec(skip_w.shape, const2),
        ],
        out_specs=pl.BlockSpec((1, L, w), lambda bi, c: (bi, c, 0)),
        out_shape=jax.ShapeDtypeStruct((b, s, w), BF16),
        scratch_shapes=[
            pltpu.VMEM((SUBLANES + L, w), F32),
            pltpu.VMEM((MLSTM_HEADS, MLSTM_HEAD_DIM, MLSTM_HEAD_DIM), F32),
            pltpu.VMEM((MLSTM_HEADS, 1, MLSTM_HEAD_DIM), F32),
            pltpu.VMEM((SUBLANES, LANES), F32),
        ],
        compiler_params=_params("parallel", "arbitrary"),
        name="mlstm",
    )(mxz, mxz, mxz, conv_w, conv_b, wq, wk, wv, w_if_t, b_if, norm_w, skip_w)


def _merge_kernel(x_ref, nw_ref, wg_ref, gb_ref, od_ref, of_ref, om_ref, wbd_ref, wbf_ref, wbm_ref,
                  wo_ref, o_ref):
    x = x_ref[...]
    d = x.shape[1]
    xn = _rms_norm(x, nw_ref[...]).astype(BF16)
    merged = None
    for br, (b_ref, w_ref) in enumerate(((od_ref, wbd_ref), (of_ref, wbf_ref), (om_ref, wbm_ref))):
        cols = slice(br * d, (br + 1) * d)
        gate = jax.nn.sigmoid(jnp.dot(xn, wg_ref[:, cols], preferred_element_type=F32) + gb_ref[:, cols])
        term = gate * jnp.dot(b_ref[...], w_ref[...], preferred_element_type=F32)
        merged = term if merged is None else merged + term
    o_ref[...] = x + jnp.dot(merged.astype(BF16), wo_ref[...], preferred_element_type=F32)


def _merge(x, norm_w, w_g, gate_b, o_diff, o_fox, o_mlstm, wbd, wbf, wbm, w_out, *, tm=512):
    m, d = x.shape
    const = lambda i: (0, 0)
    row = lambda i: (i, 0)
    return pl.pallas_call(
        _merge_kernel,
        grid=(m // tm,),
        in_specs=[
            pl.BlockSpec((tm, d), row),
            pl.BlockSpec((1, d), const),
            pl.BlockSpec(w_g.shape, const),
            pl.BlockSpec(gate_b.shape, const),
            pl.BlockSpec((tm, ATTN_WIDTH), row),
            pl.BlockSpec((tm, ATTN_WIDTH), row),
            pl.BlockSpec((tm, MLSTM_WIDTH), row),
            pl.BlockSpec(wbd.shape, const),
            pl.BlockSpec(wbf.shape, const),
            pl.BlockSpec(wbm.shape, const),
            pl.BlockSpec(w_out.shape, const),
        ],
        out_specs=pl.BlockSpec((tm, d), row),
        out_shape=jax.ShapeDtypeStruct((m, d), F32),
        compiler_params=_params("parallel"),
        name="merge",
    )(x, norm_w, w_g, gate_b, o_diff, o_fox, o_mlstm, wbd, wbf, wbm, w_out)


def _alibi_bias(seq):
    slopes = np.array([2.0 ** (-8.0 * (h + 1) / DIFF_HEADS) for h in range(DIFF_HEADS)], dtype=np.float32)
    pos = np.arange(seq, dtype=np.float32)
    per_head = slopes[:, None] * pos[None, :]
    return jnp.asarray(np.repeat(per_head[:, None, :], 2, axis=1))


def kernel(x, ffn1_norm, ffn1_w_gate, ffn1_w_up, ffn1_w_down, mix_norm, w_in, gate_bias, diff_lq1, diff_lk1, diff_lq2, diff_lk2, diff_subln, fox_b_f, mlstm_conv_w, mlstm_conv_b, mlstm_wq, mlstm_wk, mlstm_wv, mlstm_w_if, mlstm_b_if, mlstm_norm, mlstm_skip, w_branch_diff, w_branch_fox, w_branch_mlstm, w_out, ffn2_norm, ffn2_w_gate, ffn2_w_up, ffn2_w_down, final_norm):
    batch, seq, d = x.shape
    depth = w_in.shape[0]
    m = batch * seq
    bf = lambda a: a.astype(BF16)
    row = lambda a: a.reshape(1, -1)
    ff0 = QKV_WIDTH
    mx0 = ff0 + FOX_HEADS
    g0 = mx0 + 2 * MLSTM_WIDTH
    alibi = _alibi_bias(seq)

    x = x.reshape(m, d)
    for l in range(depth):
        x = _ffn(x, row(ffn1_norm[l]), bf(ffn1_w_gate[l]), bf(ffn1_w_up[l]), bf(ffn1_w_down[l]))

        w_in_l = w_in[l]
        qkv, mxz, cf = _proj(x, row(mix_norm[l]), bf(w_in_l[:, :ff0]), bf(w_in_l[:, mx0:g0]),
                             bf(w_in_l[:, ff0:mx0].T), fox_b_f[l].reshape(-1, 1), seq=seq)
        qkv = qkv.reshape(batch, seq, QKV_WIDTH)
        lam_init = 0.8 - 0.6 * math.exp(-0.3 * l)
        o_diff = _attention(
            functools.partial(_diff_attn_kernel, lam_init=lam_init), qkv, alibi,
            [row(diff_lq1[l]), row(diff_lk1[l]), row(diff_lq2[l]), row(diff_lk2[l]), row(diff_subln[l])],
            col0=0, bias_per_batch=False, name="diff_attn")
        o_fox = _attention(_fox_attn_kernel, qkv, cf.reshape(FOX_HEADS // 2, 2, m), [],
                           col0=3 * ATTN_WIDTH // LANES, bias_per_batch=True, name="fox_attn")
        o_mlstm = _mlstm(mxz.reshape(batch, seq, 2 * MLSTM_WIDTH), mlstm_conv_w[l], row(mlstm_conv_b[l]),
                         bf(mlstm_wq[l]), bf(mlstm_wk[l]), bf(mlstm_wv[l]), bf(mlstm_w_if[l].T),
                         mlstm_b_if[l].reshape(-1, 1), row(mlstm_norm[l]), row(mlstm_skip[l]))
        x = _merge(x, row(mix_norm[l]), bf(w_in_l[:, g0:]), gate_bias[l].reshape(1, -1),
                   o_diff.reshape(m, ATTN_WIDTH), o_fox.reshape(m, ATTN_WIDTH),
                   o_mlstm.reshape(m, MLSTM_WIDTH), bf(w_branch_diff[l]), bf(w_branch_fox[l]),
                   bf(w_branch_mlstm[l]), bf(w_out[l]))

        last = l == depth - 1
        x = _ffn(x, row(ffn2_norm[l]), bf(ffn2_w_gate[l]), bf(ffn2_w_up[l]), bf(ffn2_w_down[l]),
                 row(final_norm) if last else None)
    return x.reshape(batch, seq, d)
```

```python
import functools
import math

import numpy as np
import jax
import jax.numpy as jnp
from jax import lax
from jax.experimental import pallas as pl
from jax.experimental.pallas import tpu as pltpu

D_MODEL = 1024
D_FF = 2816
RMS_EPS = 1e-6
LN_EPS = 1e-5
N_BRANCH = 3
DIFF_HEADS = 4
ATTN_HEAD_DIM = 64
FOX_HEADS = 8
MLSTM_HEADS = 4
MLSTM_HEAD_DIM = 128
MLSTM_WIDTH = MLSTM_HEADS * MLSTM_HEAD_DIM
MLSTM_CONV = 4
MLSTM_CHUNK = 128
ATTN_WIDTH = 512
LANES = 128
SUBLANES = 8
VMEM_LIMIT_BYTES = 56 * 1024 * 1024
LOG2E = math.log2(math.e)
Q_SCALE = ATTN_HEAD_DIM ** -0.5 * LOG2E

F32 = jnp.float32
BF16 = jnp.bfloat16
NT_DIMS = (((1,), (1,)), ((), ()))
TN_DIMS = (((0,), (0,)), ((), ()))


def _params(*semantics):
    return pltpu.CompilerParams(dimension_semantics=semantics, vmem_limit_bytes=VMEM_LIMIT_BYTES)


def _rms_norm(x, w):
    return x * lax.rsqrt(jnp.mean(x * x, axis=-1, keepdims=True) + RMS_EPS) * w


def _silu(x):
    return x * jax.nn.sigmoid(x)


def _log_sigmoid(x):
    return jnp.minimum(x, 0.0) - jnp.log1p(jnp.exp(-jnp.abs(x)))


def _lane_cumsum(x):
    n = x.shape[-1]
    lane = lax.broadcasted_iota(jnp.int32, x.shape, 1)
    shift = 1
    while shift < n:
        x = x + jnp.where(lane >= shift, pltpu.roll(x, shift, axis=1), 0.0)
        shift *= 2
    return x


def _ffn_kernel(*refs, final):
    if final:
        x_ref, nw_ref, wg_ref, wu_ref, wd_ref, fn_ref, o_ref, xn_ref, acc_ref = refs
    else:
        x_ref, nw_ref, wg_ref, wu_ref, wd_ref, o_ref, xn_ref, acc_ref = refs
    f = pl.program_id(1)

    @pl.when(f == 0)
    def _():
        xn_ref[...] = _rms_norm(x_ref[...], nw_ref[...]).astype(BF16)
        acc_ref[...] = jnp.zeros_like(acc_ref)

    xn = xn_ref[...]
    g = jnp.dot(xn, wg_ref[...], preferred_element_type=F32)
    u = jnp.dot(xn, wu_ref[...], preferred_element_type=F32)
    h = (_silu(g) * u).astype(BF16)
    acc_ref[...] += jnp.dot(h, wd_ref[...], preferred_element_type=F32)

    @pl.when(f == pl.num_programs(1) - 1)
    def _():
        y = x_ref[...] + 0.5 * acc_ref[...]
        if final:
            y = _rms_norm(y, fn_ref[...])
        o_ref[...] = y


def _ffn(x, norm_w, w_gate, w_up, w_down, final_norm=None, *, tm=1024, tf=256):
    m, d = x.shape
    dff = w_gate.shape[1]
    final = final_norm is not None
    in_specs = [
        pl.BlockSpec((tm, d), lambda i, f: (i, 0)),
        pl.BlockSpec((1, d), lambda i, f: (0, 0)),
        pl.BlockSpec((d, tf), lambda i, f: (0, f)),
        pl.BlockSpec((d, tf), lambda i, f: (0, f)),
        pl.BlockSpec((tf, d), lambda i, f: (f, 0)),
    ]
    args = [x, norm_w, w_gate, w_up, w_down]
    if final:
        in_specs.append(pl.BlockSpec((1, d), lambda i, f: (0, 0)))
        args.append(final_norm)
    return pl.pallas_call(
        functools.partial(_ffn_kernel, final=final),
        grid=(m // tm, dff // tf),
        in_specs=in_specs,
        out_specs=pl.BlockSpec((tm, d), lambda i, f: (i, 0)),
        out_shape=jax.ShapeDtypeStruct((m, d), F32),
        scratch_shapes=[pltpu.VMEM((tm, d), BF16), pltpu.VMEM((tm, d), F32)],
        compiler_params=_params("parallel", "arbitrary"),
        name="ffn_final" if final else "ffn",
    )(*args)


def _proj_kernel(x_ref, nw_ref, wqk_ref, wvt_ref, wm_ref, wff_ref, bf_ref,
                 qk_ref, vt_ref, mxz_ref, cf_ref, carry_ref, *, tiles_per_seq):
    i = pl.program_id(0)
    xn = _rms_norm(x_ref[...], nw_ref[...]).astype(BF16)
    for grp in range(4):
        cols = slice(grp * ATTN_WIDTH, (grp + 1) * ATTN_WIDTH)
        r = jnp.dot(xn, wqk_ref[:, cols], preferred_element_type=F32)
        if grp % 2 == 0:
            r = r * Q_SCALE
        qk_ref[:, cols] = r.astype(BF16)
    for grp in range(2):
        rows = slice(grp * ATTN_WIDTH, (grp + 1) * ATTN_WIDTH)
        vt_ref[rows, :] = lax.dot_general(wvt_ref[rows, :], xn, NT_DIMS,
                                          preferred_element_type=F32).astype(BF16)
    for grp in range(2):
        cols = slice(grp * MLSTM_WIDTH, (grp + 1) * MLSTM_WIDTH)
        mxz_ref[:, cols] = jnp.dot(xn, wm_ref[:, cols], preferred_element_type=F32)

    ff = lax.dot_general(wff_ref[...], xn, NT_DIMS, preferred_element_type=F32)
    csum = _lane_cumsum(_log_sigmoid(ff + bf_ref[...]))

    @pl.when(i % tiles_per_seq == 0)
    def _():
        carry_ref[...] = jnp.zeros_like(carry_ref)

    csum = csum + carry_ref[:, 0:1]
    carry_ref[...] = jnp.broadcast_to(csum[:, -1:], carry_ref.shape)
    cf_ref[...] = -csum


def _proj(x, norm_w, w_qk, w_vt, w_m, w_ff_t, b_f, *, seq, tm=512):
    m, d = x.shape
    nh = w_ff_t.shape[0]
    const = lambda i: (0, 0)
    return pl.pallas_call(
        functools.partial(_proj_kernel, tiles_per_seq=seq // tm),
        grid=(m // tm,),
        in_specs=[
            pl.BlockSpec((tm, d), lambda i: (i, 0)),
            pl.BlockSpec((1, d), const),
            pl.BlockSpec(w_qk.shape, const),
            pl.BlockSpec(w_vt.shape, const),
            pl.BlockSpec(w_m.shape, const),
            pl.BlockSpec(w_ff_t.shape, const),
            pl.BlockSpec((nh, 1), const),
        ],
        out_specs=[
            pl.BlockSpec((tm, w_qk.shape[1]), lambda i: (i, 0)),
            pl.BlockSpec((w_vt.shape[0], tm), lambda i: (0, i)),
            pl.BlockSpec((tm, w_m.shape[1]), lambda i: (i, 0)),
            pl.BlockSpec((nh, tm), lambda i: (0, i)),
        ],
        out_shape=[
            jax.ShapeDtypeStruct((m, w_qk.shape[1]), BF16),
            jax.ShapeDtypeStruct((w_vt.shape[0], m), BF16),
            jax.ShapeDtypeStruct((m, w_m.shape[1]), F32),
            jax.ShapeDtypeStruct((nh, m), F32),
        ],
        scratch_shapes=[pltpu.VMEM((nh, LANES), F32)],
        compiler_params=_params("arbitrary"),
        name="proj",
    )(x, norm_w, w_qk, w_vt, w_m, w_ff_t, b_f)


def _flash_pair(q_ref, k_ref, vt_ref, bias_ref, cb_ref, qm_ref, s_ref, m_ref, l_ref, acc_ref, *, t, v_rows):
    qi = pl.program_id(2)
    seq = k_ref.shape[1]

    @pl.when(qi == 0)
    def _():
        for c in range(2):
            for blk in range(seq // LANES):
                sl = slice(blk * LANES, (blk + 1) * LANES)
                tile = jnp.broadcast_to(bias_ref[0, c:c + 1, sl] * LOG2E, (LANES, LANES))
                cb_ref[c, sl, :] = tile.T

    q = q_ref[0]
    lane = lax.broadcasted_iota(jnp.int32, q.shape, 1)
    half = LANES // 2
    qm_ref[0] = jnp.where(lane < half, q, jnp.zeros_like(q))
    qm_ref[1] = jnp.where(lane >= half, q, jnp.zeros_like(q))

    m_ref[...] = jnp.full_like(m_ref, -jnp.inf)
    l_ref[...] = jnp.zeros_like(l_ref)
    acc_ref[...] = jnp.zeros_like(acc_ref)

    def score_stage(j, slot):
        k_start = pl.multiple_of(j * t, t)
        k = k_ref[0, pl.ds(k_start, t), :]
        for c in range(2):
            s = lax.dot_general(k, qm_ref[c], NT_DIMS, preferred_element_type=F32)
            s_ref[slot, c] = s + jnp.tile(cb_ref[c, pl.ds(k_start, t), :], (1, t // LANES))

    def softmax_stage(j, slot, masked):
        k_start = pl.multiple_of(j * t, t)
        if masked:
            key = lax.broadcasted_iota(jnp.int32, (t, t), 0)
            qry = lax.broadcasted_iota(jnp.int32, (t, t), 1)
            keep = key <= qry
        for c in range(2):
            rows = v_rows[c]
            s = s_ref[slot, c]
            if masked:
                s = jnp.where(keep, s, -jnp.inf)
            m_old = m_ref[c]
            m_new = jnp.maximum(m_old, jnp.max(s, axis=0, keepdims=True))
            alpha = jnp.exp2(m_old - m_new)
            p = jnp.exp2(s - m_new)
            l_ref[c] = alpha * l_ref[c] + jnp.sum(p, axis=0, keepdims=True)
            pv = jnp.dot(vt_ref[rows, pl.ds(k_start, t)], p.astype(BF16), preferred_element_type=F32)
            acc_ref[c, rows, :] = alpha * acc_ref[c, rows, :] + pv
            m_ref[c] = m_new

    score_stage(0, 0)

    def body(pair, carry):
        j = 2 * pair
        score_stage(j + 1, 1)
        softmax_stage(j, 0, masked=False)
        score_stage(j + 2, 0)
        softmax_stage(j + 1, 1, masked=False)
        return carry

    lax.fori_loop(0, qi // 2, body, 0)

    @pl.when(qi % 2 == 1)
    def _():
        score_stage(qi, 1)
        softmax_stage(qi - 1, 0, masked=False)
        softmax_stage(qi, 1, masked=True)

    @pl.when(qi % 2 == 0)
    def _():
        softmax_stage(qi, 0, masked=True)

    return [acc_ref[c, v_rows[c], :] * (1.0 / l_ref[c]) for c in range(2)]


def _diff_attn_kernel(q_ref, k_ref, vt_ref, bias_ref, lq1_ref, lk1_ref, lq2_ref, lk2_ref, sub_ref,
                      o_ref, cb_ref, qm_ref, s_ref, m_ref, l_ref, acc_ref, *, t, lam_init):
    full = slice(0, LANES)
    a1, a2 = _flash_pair(q_ref, k_ref, vt_ref, bias_ref, cb_ref, qm_ref, s_ref, m_ref, l_ref, acc_ref,
                         t=t, v_rows=(full, full))
    lam = (jnp.exp(jnp.sum(lq1_ref[...] * lk1_ref[...], axis=1, keepdims=True))
           - jnp.exp(jnp.sum(lq2_ref[...] * lk2_ref[...], axis=1, keepdims=True)) + lam_init)
    d = (a1 - lam * a2).T
    o_ref[0] = (_rms_norm(d, sub_ref[...]) * (1.0 - lam_init)).astype(o_ref.dtype)


def _fox_attn_kernel(q_ref, k_ref, vt_ref, bias_ref, o_ref, cb_ref, qm_ref, s_ref, m_ref, l_ref, acc_ref, *, t):
    half = LANES // 2
    a_even, a_odd = _flash_pair(q_ref, k_ref, vt_ref, bias_ref, cb_ref, qm_ref, s_ref, m_ref, l_ref, acc_ref,
                                t=t, v_rows=(slice(0, half), slice(half, LANES)))
    o_ref[0] = jnp.concatenate([a_even, a_odd], axis=0).T.astype(o_ref.dtype)


def _attention(kernel, qk, vt, bias, extra, *, q_col, v_row, bias_per_batch, t=512, name):
    b, s, _ = qk.shape
    groups = ATTN_WIDTH // LANES
    if bias_per_batch:
        bias_map = lambda bi, g, i: (g, 0, bi)
    else:
        bias_map = lambda bi, g, i: (g, 0, 0)
    in_specs = [
        pl.BlockSpec((1, t, LANES), lambda bi, g, i: (bi, i, q_col + g)),
        pl.BlockSpec((1, s, LANES), lambda bi, g, i: (bi, 0, q_col + groups + g)),
        pl.BlockSpec((LANES, s), lambda bi, g, i: (v_row + g, bi)),
        pl.BlockSpec((1, 2, s), bias_map),
    ] + [pl.BlockSpec(e.shape, lambda bi, g, i: (0, 0)) for e in extra]
    return pl.pallas_call(
        functools.partial(kernel, t=t),
        grid=(b, groups, s // t),
        in_specs=in_specs,
        out_specs=pl.BlockSpec((1, t, LANES), lambda bi, g, i: (bi, i, g)),
        out_shape=jax.ShapeDtypeStruct((b, s, ATTN_WIDTH), BF16),
        scratch_shapes=[pltpu.VMEM((2, s, LANES), F32), pltpu.VMEM((2, t, LANES), BF16),
                        pltpu.VMEM((2, 2, t, t), F32), pltpu.VMEM((2, 1, t), F32),
                        pltpu.VMEM((2, 1, t), F32), pltpu.VMEM((2, LANES, t), F32)],
        compiler_params=_params("parallel", "parallel", "arbitrary"),
        name=name,
    )(qk, qk, vt, bias, *extra)


def _mlstm_kernel(cur_ref, prev_ref, z_ref, cw_ref, cb_ref, wq_ref, wk_ref, wv_ref, wif_ref, bif_ref,
                  nrm_ref, skip_ref, o_ref, xbuf_ref, c_ref, n_ref, m_ref):
    ci = pl.program_id(1)
    L = MLSTM_CHUNK
    dh = MLSTM_HEAD_DIM
    nh = MLSTM_HEADS

    @pl.when(ci == 0)
    def _():
        c_ref[...] = jnp.zeros_like(c_ref)
        n_ref[...] = jnp.zeros_like(n_ref)
        m_ref[...] = jnp.zeros_like(m_ref)

    cur = cur_ref[0]
    xbuf_ref[0:SUBLANES, :] = jnp.where(ci > 0, prev_ref[0], 0.0)
    xbuf_ref[SUBLANES:, :] = cur
    conv = cb_ref[...]
    for j in range(MLSTM_CONV):
        off = SUBLANES - (MLSTM_CONV - 1) + j
        conv = conv + cw_ref[j:j + 1, :] * xbuf_ref[off:off + L, :]
    xc = _silu(conv)

    qs, ks, vs = [], [], []
    ift = bif_ref[...]
    for h in range(nh):
        sl = slice(h * dh, (h + 1) * dh)
        xh = xc[:, sl].astype(BF16)
        vh = cur[:, sl].astype(BF16)
        qs.append(jnp.dot(xh, wq_ref[h], preferred_element_type=F32).astype(BF16))
        ks.append(jnp.dot(xh, wk_ref[h], preferred_element_type=F32))
        vs.append(jnp.dot(vh, wv_ref[h], preferred_element_type=F32).astype(BF16))
    for g, group in enumerate((qs, ks, vs)):
        for h in range(nh):
            col = (g * nh + h) * dh
            ift = ift + lax.dot_general(wif_ref[:, col:col + dh], group[h].astype(BF16), NT_DIMS,
                                        preferred_element_type=F32)
    li_rows = ift[0:nh]
    b_rows = _lane_cumsum(_log_sigmoid(ift))[nh:2 * nh]

    row = lax.broadcasted_iota(jnp.int32, (L, L), 0)
    col = lax.broadcasted_iota(jnp.int32, (L, L), 1)
    eye = row == col
    causal = col <= row

    def to_col(r):
        return jnp.sum(jnp.where(eye, r, 0.0), axis=1, keepdims=True)

    for h in range(nh):
        sl = slice(h * dh, (h + 1) * dh)
        li = li_rows[h:h + 1]
        b = b_rows[h:h + 1]
        li_col = to_col(li)
        b_col = to_col(b)
        b_last = b[:, L - 1:L]
        m_prev = m_ref[h:h + 1, 0:1]
        q = qs[h]
        k = ks[h] * (dh ** -0.5)
        v = vs[h]
        c_state = c_ref[h]
        n_state = n_ref[h]

        d = jnp.where(causal, b_col - b + li, -jnp.inf)
        inter = b_col + m_prev
        m_t = jnp.maximum(inter, jnp.max(d, axis=1, keepdims=True))
        w_intra = jnp.exp(d - m_t)
        w_inter = jnp.exp(inter - m_t)
        s = lax.dot_general(q, k.astype(BF16), NT_DIMS, preferred_element_type=F32) * w_intra
        num = (w_inter * jnp.dot(q, c_state.astype(BF16), preferred_element_type=F32)
               + jnp.dot(s.astype(BF16), v, preferred_element_type=F32))
        den = (w_inter * jnp.sum(q.astype(F32) * n_state, axis=1, keepdims=True)
               + jnp.sum(s, axis=1, keepdims=True))
        hh = num / jnp.maximum(jnp.abs(den), jnp.exp(-m_t))

        m_new = jnp.maximum(b_last + m_prev, jnp.max(b_last - b + li, axis=1, keepdims=True))
        wk_col = jnp.exp(b_last - b_col + li_col - m_new)
        dec = jnp.exp(b_last + m_prev - m_new)
        kw = k * wk_col
        c_ref[h] = dec * c_state + lax.dot_general(kw.astype(BF16), v, TN_DIMS,
                                                   preferred_element_type=F32)
        n_ref[h] = dec * n_state + jnp.sum(kw, axis=0, keepdims=True)
        m_ref[h:h + 1, :] = jnp.broadcast_to(m_new, (1, LANES))

        mu = jnp.mean(hh, axis=1, keepdims=True)
        cen = hh - mu
        var = jnp.mean(cen * cen, axis=1, keepdims=True)
        hn = cen * lax.rsqrt(var + LN_EPS) * nrm_ref[:, sl]
        o_ref[0, :, sl] = ((hn + skip_ref[:, sl] * xc[:, sl]) * _silu(z_ref[0, :, sl])).astype(o_ref.dtype)


def _mlstm(mxz, conv_w, conv_b, wq, wk, wv, w_if_t, b_if, norm_w, skip_w):
    b, s, _ = mxz.shape
    L = MLSTM_CHUNK
    w = MLSTM_WIDTH
    rows_per_chunk = L // SUBLANES
    const2 = lambda bi, c: (0, 0)
    const3 = lambda bi, c: (0, 0, 0)
    return pl.pallas_call(
        _mlstm_kernel,
        grid=(b, s // L),
        in_specs=[
            pl.BlockSpec((1, L, w), lambda bi, c: (bi, c, 0)),
            pl.BlockSpec((1, SUBLANES, w), lambda bi, c: (bi, jnp.maximum(c * rows_per_chunk - 1, 0), 0)),
            pl.BlockSpec((1, L, w), lambda bi, c: (bi, c, 1)),
            pl.BlockSpec(conv_w.shape, const2),
            pl.BlockSpec(conv_b.shape, const2),
            pl.BlockSpec(wq.shape, const3),
            pl.BlockSpec(wk.shape, const3),
            pl.BlockSpec(wv.shape, const3),
            pl.BlockSpec(w_if_t.shape, const2),
            pl.BlockSpec(b_if.shape, const2),
            pl.BlockSpec(norm_w.shape, const2),
            pl.BlockSpec(skip_w.shape, const2),
        ],
        out_specs=pl.BlockSpec((1, L, w), lambda bi, c: (bi, c, 0)),
        out_shape=jax.ShapeDtypeStruct((b, s, w), BF16),
        scratch_shapes=[
            pltpu.VMEM((SUBLANES + L, w), F32),
            pltpu.VMEM((MLSTM_HEADS, MLSTM_HEAD_DIM, MLSTM_HEAD_DIM), F32),
            pltpu.VMEM((MLSTM_HEADS, 1, MLSTM_HEAD_DIM), F32),
            pltpu.VMEM((SUBLANES, LANES), F32),
        ],
        compiler_params=_params("parallel", "arbitrary"),
        name="mlstm",
    )(mxz, mxz, mxz, conv_w, conv_b, wq, wk, wv, w_if_t, b_if, norm_w, skip_w)


def _merge_kernel(x_ref, nw_ref, wg_ref, gb_ref, od_ref, of_ref, om_ref, wbd_ref, wbf_ref, wbm_ref,
                  wo_ref, o_ref):
    x = x_ref[...]
    d = x.shape[1]
    xn = _rms_norm(x, nw_ref[...]).astype(BF16)
    merged = None
    for br, (b_ref, w_ref) in enumerate(((od_ref, wbd_ref), (of_ref, wbf_ref), (om_ref, wbm_ref))):
        cols = slice(br * d, (br + 1) * d)
        gate = jax.nn.sigmoid(jnp.dot(xn, wg_ref[:, cols], preferred_element_type=F32) + gb_ref[:, cols])
        term = gate * jnp.dot(b_ref[...], w_ref[...], preferred_element_type=F32)
        merged = term if merged is None else merged + term
    o_ref[...] = x + jnp.dot(merged.astype(BF16), wo_ref[...], preferred_element_type=F32)


def _merge(x, norm_w, w_g, gate_b, o_diff, o_fox, o_mlstm, wbd, wbf, wbm, w_out, *, tm=512):
    m, d = x.shape
    const = lambda i: (0, 0)
    row = lambda i: (i, 0)
    return pl.pallas_call(
        _merge_kernel,
        grid=(m // tm,),
        in_specs=[
            pl.BlockSpec((tm, d), row),
            pl.BlockSpec((1, d), const),
            pl.BlockSpec(w_g.shape, const),
            pl.BlockSpec(gate_b.shape, const),
            pl.BlockSpec((tm, ATTN_WIDTH), row),
            pl.BlockSpec((tm, ATTN_WIDTH), row),
            pl.BlockSpec((tm, MLSTM_WIDTH), row),
            pl.BlockSpec(wbd.shape, const),
            pl.BlockSpec(wbf.shape, const),
            pl.BlockSpec(wbm.shape, const),
            pl.BlockSpec(w_out.shape, const),
        ],
        out_specs=pl.BlockSpec((tm, d), row),
        out_shape=jax.ShapeDtypeStruct((m, d), F32),
        compiler_params=_params("parallel"),
        name="merge",
    )(x, norm_w, w_g, gate_b, o_diff, o_fox, o_mlstm, wbd, wbf, wbm, w_out)


def _alibi_bias(seq):
    slopes = np.array([2.0 ** (-8.0 * (h + 1) / DIFF_HEADS) for h in range(DIFF_HEADS)], dtype=np.float32)
    pos = np.arange(seq, dtype=np.float32)
    per_head = slopes[:, None] * pos[None, :]
    return jnp.asarray(np.repeat(per_head[:, None, :], 2, axis=1))


def kernel(x, ffn1_norm, ffn1_w_gate, ffn1_w_up, ffn1_w_down, mix_norm, w_in, gate_bias, diff_lq1, diff_lk1, diff_lq2, diff_lk2, diff_subln, fox_b_f, mlstm_conv_w, mlstm_conv_b, mlstm_wq, mlstm_wk, mlstm_wv, mlstm_w_if, mlstm_b_if, mlstm_norm, mlstm_skip, w_branch_diff, w_branch_fox, w_branch_mlstm, w_out, ffn2_norm, ffn2_w_gate, ffn2_w_up, ffn2_w_down, final_norm):
    batch, seq, d = x.shape
    depth = w_in.shape[0]
    m = batch * seq
    bf = lambda a: a.astype(BF16)
    row = lambda a: a.reshape(1, -1)
    aw = ATTN_WIDTH
    ff0 = 6 * aw
    mx0 = ff0 + FOX_HEADS
    g0 = mx0 + 2 * MLSTM_WIDTH
    blocks = aw // LANES
    alibi = _alibi_bias(seq)

    x = x.reshape(m, d)
    for l in range(depth):
        x = _ffn(x, row(ffn1_norm[l]), bf(ffn1_w_gate[l]), bf(ffn1_w_up[l]), bf(ffn1_w_down[l]))

        w_in_l = w_in[l]
        w_qk = bf(jnp.concatenate([w_in_l[:, 0:2 * aw], w_in_l[:, 3 * aw:5 * aw]], axis=1))
        w_vt = bf(jnp.concatenate([w_in_l[:, 2 * aw:3 * aw], w_in_l[:, 5 * aw:6 * aw]], axis=1).T)
        qk, vt, mxz, cf = _proj(x, row(mix_norm[l]), w_qk, w_vt, bf(w_in_l[:, mx0:g0]),
                                bf(w_in_l[:, ff0:mx0].T), fox_b_f[l].reshape(-1, 1), seq=seq)
        qk = qk.reshape(batch, seq, 4 * aw)
        lam_init = 0.8 - 0.6 * math.exp(-0.3 * l)
        o_diff = _attention(
            functools.partial(_diff_attn_kernel, lam_init=lam_init), qk, vt, alibi,
            [row(diff_lq1[l]), row(diff_lk1[l]), row(diff_lq2[l]), row(diff_lk2[l]), row(diff_subln[l])],
            q_col=0, v_row=0, bias_per_batch=False, name="diff_attn")
        o_fox = _attention(_fox_attn_kernel, qk, vt, cf.reshape(FOX_HEADS // 2, 2, m), [],
                           q_col=2 * blocks, v_row=blocks, bias_per_batch=True, name="fox_attn")
        o_mlstm = _mlstm(mxz.reshape(batch, seq, 2 * MLSTM_WIDTH), mlstm_conv_w[l], row(mlstm_conv_b[l]),
                         bf(mlstm_wq[l]), bf(mlstm_wk[l]), bf(mlstm_wv[l]), bf(mlstm_w_if[l].T),
                         mlstm_b_if[l].reshape(-1, 1), row(mlstm_norm[l]), row(mlstm_skip[l]))
        x = _merge(x, row(mix_norm[l]), bf(w_in_l[:, g0:]), gate_bias[l].reshape(1, -1),
                   o_diff.reshape(m, aw), o_fox.reshape(m, aw),
                   o_mlstm.reshape(m, MLSTM_WIDTH), bf(w_branch_diff[l]), bf(w_branch_fox[l]),
                   bf(w_branch_mlstm[l]), bf(w_out[l]))

        last = l == depth - 1
        x = _ffn(x, row(ffn2_norm[l]), bf(ffn2_w_gate[l]), bf(ffn2_w_up[l]), bf(ffn2_w_down[l]),
                 row(final_norm) if last else None)
    return x.reshape(batch, seq, d)
```

```python
import functools
import math

import numpy as np
import jax
import jax.numpy as jnp
from jax import lax
from jax.experimental import pallas as pl
from jax.experimental.pallas import tpu as pltpu

D_MODEL = 1024
D_FF = 2816
RMS_EPS = 1e-6
LN_EPS = 1e-5
N_BRANCH = 3
DIFF_HEADS = 4
ATTN_HEAD_DIM = 64
FOX_HEADS = 8
MLSTM_HEADS = 4
MLSTM_HEAD_DIM = 128
MLSTM_WIDTH = MLSTM_HEADS * MLSTM_HEAD_DIM
MLSTM_CONV = 4
MLSTM_CHUNK = 128
ATTN_WIDTH = 512
LANES = 128
SUBLANES = 8
VMEM_LIMIT_BYTES = 56 * 1024 * 1024
LOG2E = math.log2(math.e)
Q_SCALE = ATTN_HEAD_DIM ** -0.5 * LOG2E

F32 = jnp.float32
BF16 = jnp.bfloat16
NT_DIMS = (((1,), (1,)), ((), ()))
TN_DIMS = (((0,), (0,)), ((), ()))


def _params(*semantics):
    return pltpu.CompilerParams(dimension_semantics=semantics, vmem_limit_bytes=VMEM_LIMIT_BYTES)


def _rms_norm(x, w):
    return x * lax.rsqrt(jnp.mean(x * x, axis=-1, keepdims=True) + RMS_EPS) * w


def _silu(x):
    return x * jax.nn.sigmoid(x)


def _log_sigmoid(x):
    return jnp.minimum(x, 0.0) - jnp.log1p(jnp.exp(-jnp.abs(x)))


def _lane_cumsum(x):
    n = x.shape[-1]
    lane = lax.broadcasted_iota(jnp.int32, x.shape, 1)
    shift = 1
    while shift < n:
        x = x + jnp.where(lane >= shift, pltpu.roll(x, shift, axis=1), 0.0)
        shift *= 2
    return x


def _ffn_kernel(*refs, final):
    if final:
        x_ref, nw_ref, wg_ref, wu_ref, wd_ref, fn_ref, o_ref, xn_ref, acc_ref = refs
    else:
        x_ref, nw_ref, wg_ref, wu_ref, wd_ref, o_ref, xn_ref, acc_ref = refs
    f = pl.program_id(1)

    @pl.when(f == 0)
    def _():
        xn_ref[...] = _rms_norm(x_ref[...], nw_ref[...]).astype(BF16)
        acc_ref[...] = jnp.zeros_like(acc_ref)

    xn = xn_ref[...]
    g = jnp.dot(xn, wg_ref[...], preferred_element_type=F32)
    u = jnp.dot(xn, wu_ref[...], preferred_element_type=F32)
    h = (_silu(g) * u).astype(BF16)
    acc_ref[...] += jnp.dot(h, wd_ref[...], preferred_element_type=F32)

    @pl.when(f == pl.num_programs(1) - 1)
    def _():
        y = x_ref[...] + 0.5 * acc_ref[...]
        if final:
            y = _rms_norm(y, fn_ref[...])
        o_ref[...] = y


def _ffn(x, norm_w, w_gate, w_up, w_down, final_norm=None, *, tm=1024, tf=256):
    m, d = x.shape
    dff = w_gate.shape[1]
    final = final_norm is not None
    in_specs = [
        pl.BlockSpec((tm, d), lambda i, f: (i, 0)),
        pl.BlockSpec((1, d), lambda i, f: (0, 0)),
        pl.BlockSpec((d, tf), lambda i, f: (0, f)),
        pl.BlockSpec((d, tf), lambda i, f: (0, f)),
        pl.BlockSpec((tf, d), lambda i, f: (f, 0)),
    ]
    args = [x, norm_w, w_gate, w_up, w_down]
    if final:
        in_specs.append(pl.BlockSpec((1, d), lambda i, f: (0, 0)))
        args.append(final_norm)
    return pl.pallas_call(
        functools.partial(_ffn_kernel, final=final),
        grid=(m // tm, dff // tf),
        in_specs=in_specs,
        out_specs=pl.BlockSpec((tm, d), lambda i, f: (i, 0)),
        out_shape=jax.ShapeDtypeStruct((m, d), F32),
        scratch_shapes=[pltpu.VMEM((tm, d), BF16), pltpu.VMEM((tm, d), F32)],
        compiler_params=_params("parallel", "arbitrary"),
        name="ffn_final" if final else "ffn",
    )(*args)


def _proj_kernel(x_ref, nw_ref, wqk_ref, wvt_ref, wm_ref, wff_ref, bf_ref,
                 qk_ref, vt_ref, mxz_ref, cf_ref, carry_ref, *, tiles_per_seq):
    i = pl.program_id(0)
    xn = _rms_norm(x_ref[...], nw_ref[...]).astype(BF16)
    for grp in range(4):
        cols = slice(grp * ATTN_WIDTH, (grp + 1) * ATTN_WIDTH)
        r = jnp.dot(xn, wqk_ref[:, cols], preferred_element_type=F32)
        if grp % 2 == 0:
            r = r * Q_SCALE
        qk_ref[:, cols] = r.astype(BF16)
    for grp in range(2):
        rows = slice(grp * ATTN_WIDTH, (grp + 1) * ATTN_WIDTH)
        vt_ref[rows, :] = lax.dot_general(wvt_ref[rows, :], xn, NT_DIMS,
                                          preferred_element_type=F32).astype(BF16)
    for grp in range(2):
        cols = slice(grp * MLSTM_WIDTH, (grp + 1) * MLSTM_WIDTH)
        mxz_ref[:, cols] = jnp.dot(xn, wm_ref[:, cols], preferred_element_type=F32)

    ff = lax.dot_general(wff_ref[...], xn, NT_DIMS, preferred_element_type=F32)
    csum = _lane_cumsum(_log_sigmoid(ff + bf_ref[...]))

    @pl.when(i % tiles_per_seq == 0)
    def _():
        carry_ref[...] = jnp.zeros_like(carry_ref)

    csum = csum + carry_ref[:, 0:1]
    carry_ref[...] = jnp.broadcast_to(csum[:, -1:], carry_ref.shape)
    cf_ref[...] = -csum


def _proj(x, norm_w, w_qk, w_vt, w_m, w_ff_t, b_f, *, seq, tm=512):
    m, d = x.shape
    nh = w_ff_t.shape[0]
    const = lambda i: (0, 0)
    return pl.pallas_call(
        functools.partial(_proj_kernel, tiles_per_seq=seq // tm),
        grid=(m // tm,),
        in_specs=[
            pl.BlockSpec((tm, d), lambda i: (i, 0)),
            pl.BlockSpec((1, d), const),
            pl.BlockSpec(w_qk.shape, const),
            pl.BlockSpec(w_vt.shape, const),
            pl.BlockSpec(w_m.shape, const),
            pl.BlockSpec(w_ff_t.shape, const),
            pl.BlockSpec((nh, 1), const),
        ],
        out_specs=[
            pl.BlockSpec((tm, w_qk.shape[1]), lambda i: (i, 0)),
            pl.BlockSpec((w_vt.shape[0], tm), lambda i: (0, i)),
            pl.BlockSpec((tm, w_m.shape[1]), lambda i: (i, 0)),
            pl.BlockSpec((nh, tm), lambda i: (0, i)),
        ],
        out_shape=[
            jax.ShapeDtypeStruct((m, w_qk.shape[1]), BF16),
            jax.ShapeDtypeStruct((w_vt.shape[0], m), BF16),
            jax.ShapeDtypeStruct((m, w_m.shape[1]), F32),
            jax.ShapeDtypeStruct((nh, m), F32),
        ],
        scratch_shapes=[pltpu.VMEM((nh, LANES), F32)],
        compiler_params=_params("arbitrary"),
        name="proj",
    )(x, norm_w, w_qk, w_vt, w_m, w_ff_t, b_f)


def _flash_pair(q_ref, k_ref, vt_ref, bias_ref, cb_ref, qm_ref, s_ref, m_ref, l_ref, acc_ref, *, t, v_rows):
    qi = pl.program_id(2)
    seq = k_ref.shape[1]

    @pl.when(qi == 0)
    def _():
        for c in range(2):
            for blk in range(seq // LANES):
                sl = slice(blk * LANES, (blk + 1) * LANES)
                tile = jnp.broadcast_to(bias_ref[0, c:c + 1, sl] * LOG2E, (LANES, LANES))
                cb_ref[c, sl, :] = tile.T

    q = q_ref[0]
    lane = lax.broadcasted_iota(jnp.int32, q.shape, 1)
    half = LANES // 2
    qm_ref[0] = jnp.where(lane < half, q, jnp.zeros_like(q))
    qm_ref[1] = jnp.where(lane >= half, q, jnp.zeros_like(q))

    m_ref[...] = jnp.full_like(m_ref, -jnp.inf)
    l_ref[...] = jnp.zeros_like(l_ref)
    acc_ref[...] = jnp.zeros_like(acc_ref)

    def score_stage(j, slot):
        k_start = pl.multiple_of(j * t, t)
        k = k_ref[0, pl.ds(k_start, t), :]
        for c in range(2):
            s = lax.dot_general(k, qm_ref[c], NT_DIMS, preferred_element_type=F32)
            s_ref[slot, c] = s + jnp.tile(cb_ref[c, pl.ds(k_start, t), :], (1, t // LANES))

    def softmax_stage(j, slot, masked):
        k_start = pl.multiple_of(j * t, t)
        if masked:
            key = lax.broadcasted_iota(jnp.int32, (t, t), 0)
            qry = lax.broadcasted_iota(jnp.int32, (t, t), 1)
            keep = key <= qry
        for c in range(2):
            rows = v_rows[c]
            s = s_ref[slot, c]
            if masked:
                s = jnp.where(keep, s, -jnp.inf)
            m_old = m_ref[c]
            m_new = jnp.maximum(m_old, jnp.max(s, axis=0, keepdims=True))
            alpha = jnp.exp2(m_old - m_new)
            p = jnp.exp2(s - m_new)
            l_ref[c] = alpha * l_ref[c] + jnp.sum(p, axis=0, keepdims=True)
            pv = jnp.dot(vt_ref[rows, pl.ds(k_start, t)], p.astype(BF16), preferred_element_type=F32)
            acc_ref[c, rows, :] = alpha * acc_ref[c, rows, :] + pv
            m_ref[c] = m_new

    score_stage(0, 0)

    def body(pair, carry):
        j = 2 * pair
        score_stage(j + 1, 1)
        softmax_stage(j, 0, masked=False)
        score_stage(j + 2, 0)
        softmax_stage(j + 1, 1, masked=False)
        return carry

    lax.fori_loop(0, qi // 2, body, 0)

    @pl.when(qi % 2 == 1)
    def _():
        score_stage(qi, 1)
        softmax_stage(qi - 1, 0, masked=False)
        softmax_stage(qi, 1, masked=True)

    @pl.when(qi % 2 == 0)
    def _():
        softmax_stage(qi, 0, masked=True)

    return [acc_ref[c, v_rows[c], :] * (1.0 / l_ref[c]) for c in range(2)]


def _diff_attn_kernel(q_ref, k_ref, vt_ref, bias_ref, lq1_ref, lk1_ref, lq2_ref, lk2_ref, sub_ref,
                      o_ref, cb_ref, qm_ref, s_ref, m_ref, l_ref, acc_ref, *, t, lam_init):
    full = slice(0, LANES)
    a1, a2 = _flash_pair(q_ref, k_ref, vt_ref, bias_ref, cb_ref, qm_ref, s_ref, m_ref, l_ref, acc_ref,
                         t=t, v_rows=(full, full))
    lam = (jnp.exp(jnp.sum(lq1_ref[...] * lk1_ref[...], axis=1, keepdims=True))
           - jnp.exp(jnp.sum(lq2_ref[...] * lk2_ref[...], axis=1, keepdims=True)) + lam_init)
    d = (a1 - lam * a2).T
    o_ref[0] = (_rms_norm(d, sub_ref[...]) * (1.0 - lam_init)).astype(o_ref.dtype)


def _fox_attn_kernel(q_ref, k_ref, vt_ref, bias_ref, o_ref, cb_ref, qm_ref, s_ref, m_ref, l_ref, acc_ref, *, t):
    half = LANES // 2
    a_even, a_odd = _flash_pair(q_ref, k_ref, vt_ref, bias_ref, cb_ref, qm_ref, s_ref, m_ref, l_ref, acc_ref,
                                t=t, v_rows=(slice(0, half), slice(half, LANES)))
    o_ref[0] = jnp.concatenate([a_even, a_odd], axis=0).T.astype(o_ref.dtype)


def _attention(kernel, qk, vt, bias, extra, *, q_col, v_row, bias_per_batch, t=512, name):
    b, s, _ = qk.shape
    groups = ATTN_WIDTH // LANES
    if bias_per_batch:
        bias_map = lambda bi, g, i: (g, 0, bi)
    else:
        bias_map = lambda bi, g, i: (g, 0, 0)
    in_specs = [
        pl.BlockSpec((1, t, LANES), lambda bi, g, i: (bi, i, q_col + g)),
        pl.BlockSpec((1, s, LANES), lambda bi, g, i: (bi, 0, q_col + groups + g)),
        pl.BlockSpec((LANES, s), lambda bi, g, i: (v_row + g, bi)),
        pl.BlockSpec((1, 2, s), bias_map),
    ] + [pl.BlockSpec(e.shape, lambda bi, g, i: (0, 0)) for e in extra]
    return pl.pallas_call(
        functools.partial(kernel, t=t),
        grid=(b, groups, s // t),
        in_specs=in_specs,
        out_specs=pl.BlockSpec((1, t, LANES), lambda bi, g, i: (bi, i, g)),
        out_shape=jax.ShapeDtypeStruct((b, s, ATTN_WIDTH), BF16),
        scratch_shapes=[pltpu.VMEM((2, s, LANES), F32), pltpu.VMEM((2, t, LANES), BF16),
                        pltpu.VMEM((2, 2, t, t), F32), pltpu.VMEM((2, 1, t), F32),
                        pltpu.VMEM((2, 1, t), F32), pltpu.VMEM((2, LANES, t), F32)],
        compiler_params=_params("parallel", "parallel", "arbitrary"),
        name=name,
    )(qk, qk, vt, bias, *extra)


def _mlstm_kernel(cur_ref, prev_ref, z_ref, cw_ref, cb_ref, wqt_ref, wk_ref, wvt_ref, wif_ref, bif_ref,
                  nrm_ref, skip_ref, o_ref, xbuf_ref, ct_ref, n_ref, m_ref):
    ci = pl.program_id(1)

    @pl.when(ci == 0)
    def _():
        ct_ref[...] = jnp.zeros_like(ct_ref)
        n_ref[...] = jnp.zeros_like(n_ref)
        m_ref[...] = jnp.zeros_like(m_ref)

    for r in range(cur_ref.shape[0]):
        one = pl.ds(r, 1)
        _mlstm_chunk(ci, cur_ref.at[one], prev_ref.at[one], z_ref.at[one], cw_ref, cb_ref, wqt_ref, wk_ref,
                     wvt_ref, wif_ref, bif_ref, nrm_ref, skip_ref, o_ref.at[one], xbuf_ref.at[r],
                     ct_ref.at[r], n_ref.at[r], m_ref.at[r])


def _mlstm_chunk(ci, cur_ref, prev_ref, z_ref, cw_ref, cb_ref, wqt_ref, wk_ref, wvt_ref, wif_ref, bif_ref,
                 nrm_ref, skip_ref, o_ref, xbuf_ref, ct_ref, n_ref, m_ref):
    L = MLSTM_CHUNK
    dh = MLSTM_HEAD_DIM
    nh = MLSTM_HEADS

    cur = cur_ref[0]
    xbuf_ref[0:SUBLANES, :] = jnp.where(ci > 0, prev_ref[0], 0.0)
    xbuf_ref[SUBLANES:, :] = cur
    conv = cb_ref[...]
    for j in range(MLSTM_CONV):
        off = SUBLANES - (MLSTM_CONV - 1) + j
        conv = conv + cw_ref[j:j + 1, :] * xbuf_ref[off:off + L, :]
    xc = _silu(conv)

    qts, kss, vts = [], [], []
    ift = bif_ref[...]
    for h in range(nh):
        sl = slice(h * dh, (h + 1) * dh)
        xh = xc[:, sl].astype(BF16)
        vh = cur[:, sl].astype(BF16)
        qt = lax.dot_general(wqt_ref[h], xh, NT_DIMS, preferred_element_type=F32).astype(BF16)
        k = jnp.dot(xh, wk_ref[h], preferred_element_type=F32)
        vt = lax.dot_general(wvt_ref[h], vh, NT_DIMS, preferred_element_type=F32).astype(BF16)
        ift = ift + jnp.dot(wif_ref[:, sl], qt, preferred_element_type=F32)
        ift = ift + lax.dot_general(wif_ref[:, MLSTM_WIDTH + h * dh:MLSTM_WIDTH + (h + 1) * dh],
                                    k.astype(BF16), NT_DIMS, preferred_element_type=F32)
        ift = ift + jnp.dot(wif_ref[:, 2 * MLSTM_WIDTH + h * dh:2 * MLSTM_WIDTH + (h + 1) * dh], vt,
                            preferred_element_type=F32)
        qts.append(qt)
        kss.append((k * (dh ** -0.5)).astype(BF16))
        vts.append(vt)

    key = lax.broadcasted_iota(jnp.int32, (L, L), 0)
    qry = lax.broadcasted_iota(jnp.int32, (L, L), 1)
    causal = key <= qry
    li_rows = ift[0:nh]
    lf = _log_sigmoid(ift)
    tri = jnp.where(causal, 1.0, 0.0).astype(BF16)
    b_all = jnp.zeros_like(lf)
    rem = lf
    for _ in range(3):
        part = rem.astype(BF16)
        b_all = b_all + jnp.dot(part, tri, preferred_element_type=F32)
        rem = rem - part.astype(F32)
    b_rows = b_all[nh:2 * nh]

    for h in range(nh):
        sl = slice(h * dh, (h + 1) * dh)
        li = li_rows[h:h + 1]
        b = b_rows[h:h + 1]
        a = li - b
        a_keys = jnp.broadcast_to(a, (L, L)).T
        b_last = b[:, L - 1:L]
        m_prev = m_ref[h:h + 1, 0:1]
        qt, ks, vt = qts[h], kss[h], vts[h]
        ct_state = ct_ref[h]
        n_state = n_ref[h]

        d = jnp.where(causal, b + a_keys, -jnp.inf)
        inter = b + m_prev
        m_t = jnp.maximum(inter, jnp.max(d, axis=0, keepdims=True))
        w_inter = jnp.exp(inter - m_t)
        s = jnp.dot(ks, qt, preferred_element_type=F32) * jnp.exp(d - m_t)
        num = (w_inter * jnp.dot(ct_state.astype(BF16), qt, preferred_element_type=F32)
               + jnp.dot(vt, s.astype(BF16), preferred_element_type=F32))
        den = (w_inter * jnp.dot(n_state.astype(BF16), qt, preferred_element_type=F32)[0:1]
               + jnp.sum(s, axis=0, keepdims=True))
        hh = num * (1.0 / jnp.maximum(jnp.abs(den), jnp.exp(-m_t)))

        g = b_last + a
        m_new = jnp.maximum(b_last + m_prev, jnp.max(g, axis=1, keepdims=True))
        wk = jnp.exp(g - m_new)
        dec = jnp.exp(b_last + m_prev - m_new)
        ct_ref[h] = dec * ct_state + jnp.dot((vt.astype(F32) * wk).astype(BF16), ks,
                                             preferred_element_type=F32)
        n_ref[h] = dec * n_state + jnp.dot(jnp.broadcast_to(wk, (SUBLANES, L)).astype(BF16), ks,
                                           preferred_element_type=F32)
        m_ref[h:h + 1, :] = jnp.broadcast_to(m_new, (1, LANES))

        mu = jnp.mean(hh, axis=0, keepdims=True)
        cen = hh - mu
        var = jnp.mean(cen * cen, axis=0, keepdims=True)
        hn = (cen * lax.rsqrt(var + LN_EPS)).T * nrm_ref[:, sl]
        o_ref[0, :, sl] = ((hn + skip_ref[:, sl] * xc[:, sl]) * _silu(z_ref[0, :, sl])).astype(o_ref.dtype)


def _mlstm(mxz, conv_w, conv_b, wq, wk, wv, w_if_t, b_if, norm_w, skip_w, *, nb=2):
    b, s, _ = mxz.shape
    L = MLSTM_CHUNK
    w = MLSTM_WIDTH
    rows_per_chunk = L // SUBLANES
    const2 = lambda bi, c: (0, 0)
    const3 = lambda bi, c: (0, 0, 0)
    return pl.pallas_call(
        _mlstm_kernel,
        grid=(b // nb, s // L),
        in_specs=[
            pl.BlockSpec((nb, L, w), lambda bi, c: (bi, c, 0)),
            pl.BlockSpec((nb, SUBLANES, w), lambda bi, c: (bi, jnp.maximum(c * rows_per_chunk - 1, 0), 0)),
            pl.BlockSpec((nb, L, w), lambda bi, c: (bi, c, 1)),
            pl.BlockSpec(conv_w.shape, const2),
            pl.BlockSpec(conv_b.shape, const2),
            pl.BlockSpec(wq.shape, const3),
            pl.BlockSpec(wk.shape, const3),
            pl.BlockSpec(wv.shape, const3),
            pl.BlockSpec(w_if_t.shape, const2),
            pl.BlockSpec(b_if.shape, const2),
            pl.BlockSpec(norm_w.shape, const2),
            pl.BlockSpec(skip_w.shape, const2),
        ],
        out_specs=pl.BlockSpec((nb, L, w), lambda bi, c: (bi, c, 0)),
        out_shape=jax.ShapeDtypeStruct((b, s, w), BF16),
        scratch_shapes=[
            pltpu.VMEM((nb, SUBLANES + L, w), F32),
            pltpu.VMEM((nb, MLSTM_HEADS, MLSTM_HEAD_DIM, MLSTM_HEAD_DIM), F32),
            pltpu.VMEM((nb, MLSTM_HEADS, SUBLANES, MLSTM_HEAD_DIM), F32),
            pltpu.VMEM((nb, SUBLANES, LANES), F32),
        ],
        compiler_params=_params("parallel", "arbitrary"),
        name="mlstm",
    )(mxz, mxz, mxz, conv_w, conv_b, wq, wk, wv, w_if_t, b_if, norm_w, skip_w)


def _merge_kernel(x_ref, nw_ref, wg_ref, gb_ref, od_ref, of_ref, om_ref, wbd_ref, wbf_ref, wbm_ref,
                  wo_ref, o_ref):
    x = x_ref[...]
    d = x.shape[1]
    xn = _rms_norm(x, nw_ref[...]).astype(BF16)
    merged = None
    for br, (b_ref, w_ref) in enumerate(((od_ref, wbd_ref), (of_ref, wbf_ref), (om_ref, wbm_ref))):
        cols = slice(br * d, (br + 1) * d)
        gate = jax.nn.sigmoid(jnp.dot(xn, wg_ref[:, cols], preferred_element_type=F32) + gb_ref[:, cols])
        term = gate * jnp.dot(b_ref[...], w_ref[...], preferred_element_type=F32)
        merged = term if merged is None else merged + term
    o_ref[...] = x + jnp.dot(merged.astype(BF16), wo_ref[...], preferred_element_type=F32)


def _merge(x, norm_w, w_g, gate_b, o_diff, o_fox, o_mlstm, wbd, wbf, wbm, w_out, *, tm=512):
    m, d = x.shape
    const = lambda i: (0, 0)
    row = lambda i: (i, 0)
    return pl.pallas_call(
        _merge_kernel,
        grid=(m // tm,),
        in_specs=[
            pl.BlockSpec((tm, d), row),
            pl.BlockSpec((1, d), const),
            pl.BlockSpec(w_g.shape, const),
            pl.BlockSpec(gate_b.shape, const),
            pl.BlockSpec((tm, ATTN_WIDTH), row),
            pl.BlockSpec((tm, ATTN_WIDTH), row),
            pl.BlockSpec((tm, MLSTM_WIDTH), row),
            pl.BlockSpec(wbd.shape, const),
            pl.BlockSpec(wbf.shape, const),
            pl.BlockSpec(wbm.shape, const),
            pl.BlockSpec(w_out.shape, const),
        ],
        out_specs=pl.BlockSpec((tm, d), row),
        out_shape=jax.ShapeDtypeStruct((m, d), F32),
        compiler_params=_params("parallel"),
        name="merge",
    )(x, norm_w, w_g, gate_b, o_diff, o_fox, o_mlstm, wbd, wbf, wbm, w_out)


def _alibi_bias(seq):
    slopes = np.array([2.0 ** (-8.0 * (h + 1) / DIFF_HEADS) for h in range(DIFF_HEADS)], dtype=np.float32)
    pos = np.arange(seq, dtype=np.float32)
    per_head = slopes[:, None] * pos[None, :]
    return jnp.asarray(np.repeat(per_head[:, None, :], 2, axis=1))


def kernel(x, ffn1_norm, ffn1_w_gate, ffn1_w_up, ffn1_w_down, mix_norm, w_in, gate_bias, diff_lq1, diff_lk1, diff_lq2, diff_lk2, diff_subln, fox_b_f, mlstm_conv_w, mlstm_conv_b, mlstm_wq, mlstm_wk, mlstm_wv, mlstm_w_if, mlstm_b_if, mlstm_norm, mlstm_skip, w_branch_diff, w_branch_fox, w_branch_mlstm, w_out, ffn2_norm, ffn2_w_gate, ffn2_w_up, ffn2_w_down, final_norm):
    batch, seq, d = x.shape
    depth = w_in.shape[0]
    m = batch * seq
    bf = lambda a: a.astype(BF16)
    row = lambda a: a.reshape(1, -1)
    aw = ATTN_WIDTH
    ff0 = 6 * aw
    mx0 = ff0 + FOX_HEADS
    g0 = mx0 + 2 * MLSTM_WIDTH
    blocks = aw // LANES
    alibi = _alibi_bias(seq)

    x = x.reshape(m, d)
    for l in range(depth):
        x = _ffn(x, row(ffn1_norm[l]), bf(ffn1_w_gate[l]), bf(ffn1_w_up[l]), bf(ffn1_w_down[l]))

        w_in_l = w_in[l]
        w_qk = bf(jnp.concatenate([w_in_l[:, 0:2 * aw], w_in_l[:, 3 * aw:5 * aw]], axis=1))
        w_vt = bf(jnp.concatenate([w_in_l[:, 2 * aw:3 * aw], w_in_l[:, 5 * aw:6 * aw]], axis=1).T)
        qk, vt, mxz, cf = _proj(x, row(mix_norm[l]), w_qk, w_vt, bf(w_in_l[:, mx0:g0]),
                                bf(w_in_l[:, ff0:mx0].T), fox_b_f[l].reshape(-1, 1), seq=seq)
        qk = qk.reshape(batch, seq, 4 * aw)
        lam_init = 0.8 - 0.6 * math.exp(-0.3 * l)
        o_diff = _attention(
            functools.partial(_diff_attn_kernel, lam_init=lam_init), qk, vt, alibi,
            [row(diff_lq1[l]), row(diff_lk1[l]), row(diff_lq2[l]), row(diff_lk2[l]), row(diff_subln[l])],
            q_col=0, v_row=0, bias_per_batch=False, name="diff_attn")
        o_fox = _attention(_fox_attn_kernel, qk, vt, cf.reshape(FOX_HEADS // 2, 2, m), [],
                           q_col=2 * blocks, v_row=blocks, bias_per_batch=True, name="fox_attn")
        o_mlstm = _mlstm(mxz.reshape(batch, seq, 2 * MLSTM_WIDTH), mlstm_conv_w[l], row(mlstm_conv_b[l]),
                         bf(jnp.swapaxes(mlstm_wq[l], 1, 2)), bf(mlstm_wk[l]),
                         bf(jnp.swapaxes(mlstm_wv[l], 1, 2)), bf(mlstm_w_if[l].T),
                         mlstm_b_if[l].reshape(-1, 1), row(mlstm_norm[l]), row(mlstm_skip[l]))
        x = _merge(x, row(mix_norm[l]), bf(w_in_l[:, g0:]), gate_bias[l].reshape(1, -1),
                   o_diff.reshape(m, aw), o_fox.reshape(m, aw),
                   o_mlstm.reshape(m, MLSTM_WIDTH), bf(w_branch_diff[l]), bf(w_branch_fox[l]),
                   bf(w_branch_mlstm[l]), bf(w_out[l]))

        last = l == depth - 1
        x = _ffn(x, row(ffn2_norm[l]), bf(ffn2_w_gate[l]), bf(ffn2_w_up[l]), bf(ffn2_w_down[l]),
                 row(final_norm) if last else None)
    return x.reshape(batch, seq, d)
```

```python
import functools
import math

import numpy as np
import jax
import jax.numpy as jnp
from jax import lax
from jax.experimental import pallas as pl
from jax.experimental.pallas import tpu as pltpu

D_MODEL = 1024
D_FF = 2816
RMS_EPS = 1e-6
LN_EPS = 1e-5
N_BRANCH = 3
DIFF_HEADS = 4
ATTN_HEAD_DIM = 64
FOX_HEADS = 8
MLSTM_HEADS = 4
MLSTM_HEAD_DIM = 128
MLSTM_WIDTH = MLSTM_HEADS * MLSTM_HEAD_DIM
MLSTM_CONV = 4
MLSTM_CHUNK = 128
ATTN_WIDTH = 512
LANES = 128
SUBLANES = 8
VMEM_LIMIT_BYTES = 56 * 1024 * 1024
LOG2E = math.log2(math.e)
Q_SCALE = ATTN_HEAD_DIM ** -0.5 * LOG2E

F32 = jnp.float32
BF16 = jnp.bfloat16
NT_DIMS = (((1,), (1,)), ((), ()))
TN_DIMS = (((0,), (0,)), ((), ()))


def _params(*semantics):
    return pltpu.CompilerParams(dimension_semantics=semantics, vmem_limit_bytes=VMEM_LIMIT_BYTES)


def _rms_norm(x, w):
    return x * lax.rsqrt(jnp.mean(x * x, axis=-1, keepdims=True) + RMS_EPS) * w


def _silu(x):
    return x * jax.nn.sigmoid(x)


def _log_sigmoid(x):
    return jnp.minimum(x, 0.0) - jnp.log1p(jnp.exp(-jnp.abs(x)))


def _lane_cumsum(x):
    n = x.shape[-1]
    lane = lax.broadcasted_iota(jnp.int32, x.shape, 1)
    shift = 1
    while shift < n:
        x = x + jnp.where(lane >= shift, pltpu.roll(x, shift, axis=1), 0.0)
        shift *= 2
    return x


def _ffn_kernel(*refs, final):
    if final:
        x_ref, nw_ref, wg_ref, wu_ref, wd_ref, fn_ref, o_ref, xn_ref, acc_ref = refs
    else:
        x_ref, nw_ref, wg_ref, wu_ref, wd_ref, o_ref, xn_ref, acc_ref = refs
    f = pl.program_id(1)

    @pl.when(f == 0)
    def _():
        xn_ref[...] = _rms_norm(x_ref[...], nw_ref[...]).astype(BF16)
        acc_ref[...] = jnp.zeros_like(acc_ref)

    xn = xn_ref[...]
    g = jnp.dot(xn, wg_ref[...].astype(BF16), preferred_element_type=F32)
    u = jnp.dot(xn, wu_ref[...].astype(BF16), preferred_element_type=F32)
    h = (_silu(g) * u).astype(BF16)
    acc_ref[...] += jnp.dot(h, wd_ref[...].astype(BF16), preferred_element_type=F32)

    @pl.when(f == pl.num_programs(1) - 1)
    def _():
        y = x_ref[...] + 0.5 * acc_ref[...]
        if final:
            y = _rms_norm(y, fn_ref[...])
        o_ref[...] = y


def _ffn(x, norm_w, w_gate, w_up, w_down, layer, final_norm=None, *, tm=1024, tf=256):
    m, d = x.shape
    dff = w_gate.shape[2]
    final = final_norm is not None
    in_specs = [
        pl.BlockSpec((tm, d), lambda i, f: (i, 0)),
        pl.BlockSpec((1, d), lambda i, f: (0, 0)),
        pl.BlockSpec((None, d, tf), lambda i, f: (layer, 0, f)),
        pl.BlockSpec((None, d, tf), lambda i, f: (layer, 0, f)),
        pl.BlockSpec((None, tf, d), lambda i, f: (layer, f, 0)),
    ]
    args = [x, norm_w, w_gate, w_up, w_down]
    if final:
        in_specs.append(pl.BlockSpec((1, d), lambda i, f: (0, 0)))
        args.append(final_norm)
    return pl.pallas_call(
        functools.partial(_ffn_kernel, final=final),
        grid=(m // tm, dff // tf),
        in_specs=in_specs,
        out_specs=pl.BlockSpec((tm, d), lambda i, f: (i, 0)),
        out_shape=jax.ShapeDtypeStruct((m, d), F32),
        scratch_shapes=[pltpu.VMEM((tm, d), BF16), pltpu.VMEM((tm, d), F32)],
        compiler_params=_params("parallel", "arbitrary"),
        name="ffn_final" if final else "ffn",
    )(*args)


def _proj_kernel(x_ref, nw_ref, wqk_ref, wvt_ref, wm_ref, wff_ref, bf_ref,
                 qk_ref, vt_ref, mxz_ref, cf_ref, carry_ref, *, tiles_per_seq):
    i = pl.program_id(0)
    xn = _rms_norm(x_ref[...], nw_ref[...]).astype(BF16)
    for grp in range(4):
        cols = slice(grp * ATTN_WIDTH, (grp + 1) * ATTN_WIDTH)
        r = jnp.dot(xn, wqk_ref[:, cols], preferred_element_type=F32)
        if grp % 2 == 0:
            r = r * Q_SCALE
        qk_ref[:, cols] = r.astype(BF16)
    for grp in range(2):
        rows = slice(grp * ATTN_WIDTH, (grp + 1) * ATTN_WIDTH)
        vt_ref[rows, :] = lax.dot_general(wvt_ref[rows, :], xn, NT_DIMS,
                                          preferred_element_type=F32).astype(BF16)
    for grp in range(2):
        cols = slice(grp * MLSTM_WIDTH, (grp + 1) * MLSTM_WIDTH)
        mxz_ref[:, cols] = jnp.dot(xn, wm_ref[:, cols], preferred_element_type=F32)

    ff = lax.dot_general(wff_ref[...], xn, NT_DIMS, preferred_element_type=F32)
    csum = _lane_cumsum(_log_sigmoid(ff + bf_ref[...]))

    @pl.when(i % tiles_per_seq == 0)
    def _():
        carry_ref[...] = jnp.zeros_like(carry_ref)

    csum = csum + carry_ref[:, 0:1]
    carry_ref[...] = jnp.broadcast_to(csum[:, -1:], carry_ref.shape)
    cf_ref[...] = -csum


def _proj(x, norm_w, w_qk, w_vt, w_m, w_ff_t, b_f, *, seq, tm=512):
    m, d = x.shape
    nh = w_ff_t.shape[0]
    const = lambda i: (0, 0)
    return pl.pallas_call(
        functools.partial(_proj_kernel, tiles_per_seq=seq // tm),
        grid=(m // tm,),
        in_specs=[
            pl.BlockSpec((tm, d), lambda i: (i, 0)),
            pl.BlockSpec((1, d), const),
            pl.BlockSpec(w_qk.shape, const),
            pl.BlockSpec(w_vt.shape, const),
            pl.BlockSpec(w_m.shape, const),
            pl.BlockSpec(w_ff_t.shape, const),
            pl.BlockSpec((nh, 1), const),
        ],
        out_specs=[
            pl.BlockSpec((tm, w_qk.shape[1]), lambda i: (i, 0)),
            pl.BlockSpec((w_vt.shape[0], tm), lambda i: (0, i)),
            pl.BlockSpec((tm, w_m.shape[1]), lambda i: (i, 0)),
            pl.BlockSpec((nh, tm), lambda i: (0, i)),
        ],
        out_shape=[
            jax.ShapeDtypeStruct((m, w_qk.shape[1]), BF16),
            jax.ShapeDtypeStruct((w_vt.shape[0], m), BF16),
            jax.ShapeDtypeStruct((m, w_m.shape[1]), F32),
            jax.ShapeDtypeStruct((nh, m), F32),
        ],
        scratch_shapes=[pltpu.VMEM((nh, LANES), F32)],
        compiler_params=_params("arbitrary"),
        name="proj",
    )(x, norm_w, w_qk, w_vt, w_m, w_ff_t, b_f)


def _flash_pair(q_ref, k_ref, vt_ref, bias_ref, cb_ref, qm_ref, s_ref, m_ref, l_ref, acc_ref, *, t, v_rows):
    qi = pl.program_id(2)
    seq = k_ref.shape[1]

    @pl.when(qi == 0)
    def _():
        for c in range(2):
            for blk in range(seq // LANES):
                sl = slice(blk * LANES, (blk + 1) * LANES)
                tile = jnp.broadcast_to(bias_ref[0, c:c + 1, sl] * LOG2E, (LANES, LANES))
                cb_ref[c, sl, :] = tile.T

    q = q_ref[0]
    lane = lax.broadcasted_iota(jnp.int32, q.shape, 1)
    half = LANES // 2
    qm_ref[0] = jnp.where(lane < half, q, jnp.zeros_like(q))
    qm_ref[1] = jnp.where(lane >= half, q, jnp.zeros_like(q))

    m_ref[...] = jnp.full_like(m_ref, -jnp.inf)
    l_ref[...] = jnp.zeros_like(l_ref)
    acc_ref[...] = jnp.zeros_like(acc_ref)

    def score_stage(j, slot):
        k_start = pl.multiple_of(j * t, t)
        k = k_ref[0, pl.ds(k_start, t), :]
        for c in range(2):
            s = lax.dot_general(k, qm_ref[c], NT_DIMS, preferred_element_type=F32)
            s_ref[slot, c] = s + jnp.tile(cb_ref[c, pl.ds(k_start, t), :], (1, t // LANES))

    def softmax_stage(j, slot, masked):
        k_start = pl.multiple_of(j * t, t)
        if masked:
            key = lax.broadcasted_iota(jnp.int32, (t, t), 0)
            qry = lax.broadcasted_iota(jnp.int32, (t, t), 1)
            keep = key <= qry
        for c in range(2):
            rows = v_rows[c]
            s = s_ref[slot, c]
            if masked:
                s = jnp.where(keep, s, -jnp.inf)
            m_old = m_ref[c]
            m_new = jnp.maximum(m_old, jnp.max(s, axis=0, keepdims=True))
            alpha = jnp.exp2(m_old - m_new)
            p = jnp.exp2(s - m_new)
            l_ref[c] = alpha * l_ref[c] + jnp.sum(p, axis=0, keepdims=True)
            pv = jnp.dot(vt_ref[rows, pl.ds(k_start, t)], p.astype(BF16), preferred_element_type=F32)
            acc_ref[c, rows, :] = alpha * acc_ref[c, rows, :] + pv
            m_ref[c] = m_new

    score_stage(0, 0)

    def body(pair, carry):
        j = 2 * pair
        score_stage(j + 1, 1)
        softmax_stage(j, 0, masked=False)
        score_stage(j + 2, 0)
        softmax_stage(j + 1, 1, masked=False)
        return carry

    lax.fori_loop(0, qi // 2, body, 0)

    @pl.when(qi % 2 == 1)
    def _():
        score_stage(qi, 1)
        softmax_stage(qi - 1, 0, masked=False)
        softmax_stage(qi, 1, masked=True)

    @pl.when(qi % 2 == 0)
    def _():
        softmax_stage(qi, 0, masked=True)

    return [acc_ref[c, v_rows[c], :] * (1.0 / l_ref[c]) for c in range(2)]


def _diff_attn_kernel(q_ref, k_ref, vt_ref, bias_ref, lq1_ref, lk1_ref, lq2_ref, lk2_ref, sub_ref,
                      o_ref, cb_ref, qm_ref, s_ref, m_ref, l_ref, acc_ref, *, t, lam_init):
    full = slice(0, LANES)
    a1, a2 = _flash_pair(q_ref, k_ref, vt_ref, bias_ref, cb_ref, qm_ref, s_ref, m_ref, l_ref, acc_ref,
                         t=t, v_rows=(full, full))
    lam = (jnp.exp(jnp.sum(lq1_ref[...] * lk1_ref[...], axis=1, keepdims=True))
           - jnp.exp(jnp.sum(lq2_ref[...] * lk2_ref[...], axis=1, keepdims=True)) + lam_init)
    d = (a1 - lam * a2).T
    o_ref[0] = (_rms_norm(d, sub_ref[...]) * (1.0 - lam_init)).astype(o_ref.dtype)


def _fox_attn_kernel(q_ref, k_ref, vt_ref, bias_ref, o_ref, cb_ref, qm_ref, s_ref, m_ref, l_ref, acc_ref, *, t):
    half = LANES // 2
    a_even, a_odd = _flash_pair(q_ref, k_ref, vt_ref, bias_ref, cb_ref, qm_ref, s_ref, m_ref, l_ref, acc_ref,
                                t=t, v_rows=(slice(0, half), slice(half, LANES)))
    o_ref[0] = jnp.concatenate([a_even, a_odd], axis=0).T.astype(o_ref.dtype)


def _attention(kernel, qk, vt, bias, extra, *, q_col, v_row, bias_per_batch, t=512, name):
    b, s, _ = qk.shape
    groups = ATTN_WIDTH // LANES
    if bias_per_batch:
        bias_map = lambda bi, g, i: (g, 0, bi)
    else:
        bias_map = lambda bi, g, i: (g, 0, 0)
    in_specs = [
        pl.BlockSpec((1, t, LANES), lambda bi, g, i: (bi, i, q_col + g)),
        pl.BlockSpec((1, s, LANES), lambda bi, g, i: (bi, 0, q_col + groups + g)),
        pl.BlockSpec((LANES, s), lambda bi, g, i: (v_row + g, bi)),
        pl.BlockSpec((1, 2, s), bias_map),
    ] + [pl.BlockSpec(e.shape, lambda bi, g, i: (0, 0)) for e in extra]
    return pl.pallas_call(
        functools.partial(kernel, t=t),
        grid=(b, groups, s // t),
        in_specs=in_specs,
        out_specs=pl.BlockSpec((1, t, LANES), lambda bi, g, i: (bi, i, g)),
        out_shape=jax.ShapeDtypeStruct((b, s, ATTN_WIDTH), BF16),
        scratch_shapes=[pltpu.VMEM((2, s, LANES), F32), pltpu.VMEM((2, t, LANES), BF16),
                        pltpu.VMEM((2, 2, t, t), F32), pltpu.VMEM((2, 1, t), F32),
                        pltpu.VMEM((2, 1, t), F32), pltpu.VMEM((2, LANES, t), F32)],
        compiler_params=_params("parallel", "parallel", "arbitrary"),
        name=name,
    )(qk, qk, vt, bias, *extra)


def _mlstm_kernel(cur_ref, prev_ref, z_ref, cw_ref, cb_ref, wqt_ref, wk_ref, wvt_ref, wif_ref, bif_ref,
                  nrm_ref, skip_ref, o_ref, xbuf_ref, ct_ref, n_ref, m_ref):
    ci = pl.program_id(1)

    @pl.when(ci == 0)
    def _():
        ct_ref[...] = jnp.zeros_like(ct_ref)
        n_ref[...] = jnp.zeros_like(n_ref)
        m_ref[...] = jnp.zeros_like(m_ref)

    for r in range(cur_ref.shape[0]):
        one = pl.ds(r, 1)
        _mlstm_chunk(ci, cur_ref.at[one], prev_ref.at[one], z_ref.at[one], cw_ref, cb_ref, wqt_ref, wk_ref,
                     wvt_ref, wif_ref, bif_ref, nrm_ref, skip_ref, o_ref.at[one], xbuf_ref.at[r],
                     ct_ref.at[r], n_ref.at[r], m_ref.at[r])


def _mlstm_chunk(ci, cur_ref, prev_ref, z_ref, cw_ref, cb_ref, wqt_ref, wk_ref, wvt_ref, wif_ref, bif_ref,
                 nrm_ref, skip_ref, o_ref, xbuf_ref, ct_ref, n_ref, m_ref):
    L = MLSTM_CHUNK
    dh = MLSTM_HEAD_DIM
    nh = MLSTM_HEADS

    cur = cur_ref[0]
    xbuf_ref[0:SUBLANES, :] = jnp.where(ci > 0, prev_ref[0], 0.0)
    xbuf_ref[SUBLANES:, :] = cur
    conv = cb_ref[...]
    for j in range(MLSTM_CONV):
        off = SUBLANES - (MLSTM_CONV - 1) + j
        conv = conv + cw_ref[j:j + 1, :] * xbuf_ref[off:off + L, :]
    xc = _silu(conv)

    qts, kss, vts = [], [], []
    ift = bif_ref[...]
    for h in range(nh):
        sl = slice(h * dh, (h + 1) * dh)
        xh = xc[:, sl].astype(BF16)
        vh = cur[:, sl].astype(BF16)
        qt = lax.dot_general(wqt_ref[h], xh, NT_DIMS, preferred_element_type=F32).astype(BF16)
        k = jnp.dot(xh, wk_ref[h], preferred_element_type=F32)
        vt = lax.dot_general(wvt_ref[h], vh, NT_DIMS, preferred_element_type=F32).astype(BF16)
        ift = ift + jnp.dot(wif_ref[:, sl], qt, preferred_element_type=F32)
        ift = ift + lax.dot_general(wif_ref[:, MLSTM_WIDTH + h * dh:MLSTM_WIDTH + (h + 1) * dh],
                                    k.astype(BF16), NT_DIMS, preferred_element_type=F32)
        ift = ift + jnp.dot(wif_ref[:, 2 * MLSTM_WIDTH + h * dh:2 * MLSTM_WIDTH + (h + 1) * dh], vt,
                            preferred_element_type=F32)
        qts.append(qt)
        kss.append((k * (dh ** -0.5)).astype(BF16))
        vts.append(vt)

    key = lax.broadcasted_iota(jnp.int32, (L, L), 0)
    qry = lax.broadcasted_iota(jnp.int32, (L, L), 1)
    causal = key <= qry
    li_rows = ift[0:nh]
    lf = _log_sigmoid(ift)
    tri = jnp.where(causal, 1.0, 0.0).astype(BF16)
    b_all = jnp.zeros_like(lf)
    rem = lf
    for _ in range(3):
        part = rem.astype(BF16)
        b_all = b_all + jnp.dot(part, tri, preferred_element_type=F32)
        rem = rem - part.astype(F32)
    b_rows = b_all[nh:2 * nh]

    for h in range(nh):
        sl = slice(h * dh, (h + 1) * dh)
        li = li_rows[h:h + 1]
        b = b_rows[h:h + 1]
        a = li - b
        a_keys = jnp.broadcast_to(a, (L, L)).T
        b_last = b[:, L - 1:L]
        m_prev = m_ref[h:h + 1, 0:1]
        qt, ks, vt = qts[h], kss[h], vts[h]
        ct_state = ct_ref[h]
        n_state = n_ref[h]

        d = jnp.where(causal, b + a_keys, -jnp.inf)
        inter = b + m_prev
        m_t = jnp.maximum(inter, jnp.max(d, axis=0, keepdims=True))
        w_inter = jnp.exp(inter - m_t)
        s = jnp.dot(ks, qt, preferred_element_type=F32) * jnp.exp(d - m_t)
        num = (w_inter * jnp.dot(ct_state.astype(BF16), qt, preferred_element_type=F32)
               + jnp.dot(vt, s.astype(BF16), preferred_element_type=F32))
        den = (w_inter * jnp.dot(n_state.astype(BF16), qt, preferred_element_type=F32)[0:1]
               + jnp.sum(s, axis=0, keepdims=True))
        hh = num * (1.0 / jnp.maximum(jnp.abs(den), jnp.exp(-m_t)))

        g = b_last + a
        m_new = jnp.maximum(b_last + m_prev, jnp.max(g, axis=1, keepdims=True))
        wk = jnp.exp(g - m_new)
        dec = jnp.exp(b_last + m_prev - m_new)
        ct_ref[h] = dec * ct_state + jnp.dot((vt.astype(F32) * wk).astype(BF16), ks,
                                             preferred_element_type=F32)
        n_ref[h] = dec * n_state + jnp.dot(jnp.broadcast_to(wk, (SUBLANES, L)).astype(BF16), ks,
                                           preferred_element_type=F32)
        m_ref[h:h + 1, :] = jnp.broadcast_to(m_new, (1, LANES))

        mu = jnp.mean(hh, axis=0, keepdims=True)
        cen = hh - mu
        var = jnp.mean(cen * cen, axis=0, keepdims=True)
        hn = (cen * lax.rsqrt(var + LN_EPS)).T * nrm_ref[:, sl]
        o_ref[0, :, sl] = ((hn + skip_ref[:, sl] * xc[:, sl]) * _silu(z_ref[0, :, sl])).astype(o_ref.dtype)


def _mlstm(mxz, conv_w, conv_b, wq, wk, wv, w_if_t, b_if, norm_w, skip_w, *, nb=2):
    b, s, _ = mxz.shape
    L = MLSTM_CHUNK
    w = MLSTM_WIDTH
    rows_per_chunk = L // SUBLANES
    const2 = lambda bi, c: (0, 0)
    const3 = lambda bi, c: (0, 0, 0)
    return pl.pallas_call(
        _mlstm_kernel,
        grid=(b // nb, s // L),
        in_specs=[
            pl.BlockSpec((nb, L, w), lambda bi, c: (bi, c, 0)),
            pl.BlockSpec((nb, SUBLANES, w), lambda bi, c: (bi, jnp.maximum(c * rows_per_chunk - 1, 0), 0)),
            pl.BlockSpec((nb, L, w), lambda bi, c: (bi, c, 1)),
            pl.BlockSpec(conv_w.shape, const2),
            pl.BlockSpec(conv_b.shape, const2),
            pl.BlockSpec(wq.shape, const3),
            pl.BlockSpec(wk.shape, const3),
            pl.BlockSpec(wv.shape, const3),
            pl.BlockSpec(w_if_t.shape, const2),
            pl.BlockSpec(b_if.shape, const2),
            pl.BlockSpec(norm_w.shape, const2),
            pl.BlockSpec(skip_w.shape, const2),
        ],
        out_specs=pl.BlockSpec((nb, L, w), lambda bi, c: (bi, c, 0)),
        out_shape=jax.ShapeDtypeStruct((b, s, w), BF16),
        scratch_shapes=[
            pltpu.VMEM((nb, SUBLANES + L, w), F32),
            pltpu.VMEM((nb, MLSTM_HEADS, MLSTM_HEAD_DIM, MLSTM_HEAD_DIM), F32),
            pltpu.VMEM((nb, MLSTM_HEADS, SUBLANES, MLSTM_HEAD_DIM), F32),
            pltpu.VMEM((nb, SUBLANES, LANES), F32),
        ],
        compiler_params=_params("parallel", "arbitrary"),
        name="mlstm",
    )(mxz, mxz, mxz, conv_w, conv_b, wq, wk, wv, w_if_t, b_if, norm_w, skip_w)


def _merge_kernel(x_ref, nw_ref, wg_ref, gb_ref, od_ref, of_ref, om_ref, wbd_ref, wbf_ref, wbm_ref,
                  wo_ref, o_ref):
    x = x_ref[...]
    d = x.shape[1]
    xn = _rms_norm(x, nw_ref[...]).astype(BF16)
    merged = None
    for br, (b_ref, w_ref) in enumerate(((od_ref, wbd_ref), (of_ref, wbf_ref), (om_ref, wbm_ref))):
        cols = slice(br * d, (br + 1) * d)
        gate = jax.nn.sigmoid(jnp.dot(xn, wg_ref[:, cols], preferred_element_type=F32) + gb_ref[:, cols])
        term = gate * jnp.dot(b_ref[...], w_ref[...], preferred_element_type=F32)
        merged = term if merged is None else merged + term
    o_ref[...] = x + jnp.dot(merged.astype(BF16), wo_ref[...], preferred_element_type=F32)


def _merge(x, norm_w, w_g, gate_b, o_diff, o_fox, o_mlstm, wbd, wbf, wbm, w_out, *, tm=512):
    m, d = x.shape
    const = lambda i: (0, 0)
    row = lambda i: (i, 0)
    return pl.pallas_call(
        _merge_kernel,
        grid=(m // tm,),
        in_specs=[
            pl.BlockSpec((tm, d), row),
            pl.BlockSpec((1, d), const),
            pl.BlockSpec(w_g.shape, const),
            pl.BlockSpec(gate_b.shape, const),
            pl.BlockSpec((tm, ATTN_WIDTH), row),
            pl.BlockSpec((tm, ATTN_WIDTH), row),
            pl.BlockSpec((tm, MLSTM_WIDTH), row),
            pl.BlockSpec(wbd.shape, const),
            pl.BlockSpec(wbf.shape, const),
            pl.BlockSpec(wbm.shape, const),
            pl.BlockSpec(w_out.shape, const),
        ],
        out_specs=pl.BlockSpec((tm, d), row),
        out_shape=jax.ShapeDtypeStruct((m, d), F32),
        compiler_params=_params("parallel"),
        name="merge",
    )(x, norm_w, w_g, gate_b, o_diff, o_fox, o_mlstm, wbd, wbf, wbm, w_out)


def _alibi_bias(seq):
    slopes = np.array([2.0 ** (-8.0 * (h + 1) / DIFF_HEADS) for h in range(DIFF_HEADS)], dtype=np.float32)
    pos = np.arange(seq, dtype=np.float32)
    per_head = slopes[:, None] * pos[None, :]
    return jnp.asarray(np.repeat(per_head[:, None, :], 2, axis=1))


def kernel(x, ffn1_norm, ffn1_w_gate, ffn1_w_up, ffn1_w_down, mix_norm, w_in, gate_bias, diff_lq1, diff_lk1, diff_lq2, diff_lk2, diff_subln, fox_b_f, mlstm_conv_w, mlstm_conv_b, mlstm_wq, mlstm_wk, mlstm_wv, mlstm_w_if, mlstm_b_if, mlstm_norm, mlstm_skip, w_branch_diff, w_branch_fox, w_branch_mlstm, w_out, ffn2_norm, ffn2_w_gate, ffn2_w_up, ffn2_w_down, final_norm):
    batch, seq, d = x.shape
    depth = w_in.shape[0]
    m = batch * seq
    bf = lambda a: a.astype(BF16)
    row = lambda a: a.reshape(1, -1)
    aw = ATTN_WIDTH
    ff0 = 6 * aw
    mx0 = ff0 + FOX_HEADS
    g0 = mx0 + 2 * MLSTM_WIDTH
    blocks = aw // LANES
    alibi = _alibi_bias(seq)

    x = x.reshape(m, d)
    for l in range(depth):
        x = _ffn(x, row(ffn1_norm[l]), ffn1_w_gate, ffn1_w_up, ffn1_w_down, l)

        w_in_l = w_in[l]
        w_qk = bf(jnp.concatenate([w_in_l[:, 0:2 * aw], w_in_l[:, 3 * aw:5 * aw]], axis=1))
        w_vt = bf(jnp.concatenate([w_in_l[:, 2 * aw:3 * aw], w_in_l[:, 5 * aw:6 * aw]], axis=1).T)
        qk, vt, mxz, cf = _proj(x, row(mix_norm[l]), w_qk, w_vt, bf(w_in_l[:, mx0:g0]),
                                bf(w_in_l[:, ff0:mx0].T), fox_b_f[l].reshape(-1, 1), seq=seq)
        qk = qk.reshape(batch, seq, 4 * aw)
        lam_init = 0.8 - 0.6 * math.exp(-0.3 * l)
        o_diff = _attention(
            functools.partial(_diff_attn_kernel, lam_init=lam_init), qk, vt, alibi,
            [row(diff_lq1[l]), row(diff_lk1[l]), row(diff_lq2[l]), row(diff_lk2[l]), row(diff_subln[l])],
            q_col=0, v_row=0, bias_per_batch=False, name="diff_attn")
        o_fox = _attention(_fox_attn_kernel, qk, vt, cf.reshape(FOX_HEADS // 2, 2, m), [],
                           q_col=2 * blocks, v_row=blocks, bias_per_batch=True, name="fox_attn")
        o_mlstm = _mlstm(mxz.reshape(batch, seq, 2 * MLSTM_WIDTH), mlstm_conv_w[l], row(mlstm_conv_b[l]),
                         bf(jnp.swapaxes(mlstm_wq[l], 1, 2)), bf(mlstm_wk[l]),
                         bf(jnp.swapaxes(mlstm_wv[l], 1, 2)), bf(mlstm_w_if[l].T),
                         mlstm_b_if[l].reshape(-1, 1), row(mlstm_norm[l]), row(mlstm_skip[l]))
        x = _merge(x, row(mix_norm[l]), bf(w_in_l[:, g0:]), gate_bias[l].reshape(1, -1),
                   o_diff.reshape(m, aw), o_fox.reshape(m, aw),
                   o_mlstm.reshape(m, MLSTM_WIDTH), bf(w_branch_diff[l]), bf(w_branch_fox[l]),
                   bf(w_branch_mlstm[l]), bf(w_out[l]))

        last = l == depth - 1
        x = _ffn(x, row(ffn2_norm[l]), ffn2_w_gate, ffn2_w_up, ffn2_w_down, l,
                 row(final_norm) if last else None)
    return x.reshape(batch, seq, d)
```

```python
import functools
import math

import numpy as np
import jax
import jax.numpy as jnp
from jax import lax
from jax.experimental import pallas as pl
from jax.experimental.pallas import tpu as pltpu

D_MODEL = 1024
D_FF = 2816
RMS_EPS = 1e-6
LN_EPS = 1e-5
N_BRANCH = 3
DIFF_HEADS = 4
ATTN_HEAD_DIM = 64
FOX_HEADS = 8
MLSTM_HEADS = 4
MLSTM_HEAD_DIM = 128
MLSTM_WIDTH = MLSTM_HEADS * MLSTM_HEAD_DIM
MLSTM_CONV = 4
MLSTM_CHUNK = 128
ATTN_WIDTH = 512
LANES = 128
SUBLANES = 8
VMEM_LIMIT_BYTES = 56 * 1024 * 1024
LOG2E = math.log2(math.e)
Q_SCALE = ATTN_HEAD_DIM ** -0.5 * LOG2E

F32 = jnp.float32
BF16 = jnp.bfloat16
NT_DIMS = (((1,), (1,)), ((), ()))
TN_DIMS = (((0,), (0,)), ((), ()))


def _params(*semantics):
    return pltpu.CompilerParams(dimension_semantics=semantics, vmem_limit_bytes=VMEM_LIMIT_BYTES)


def _rms_norm(x, w):
    return x * lax.rsqrt(jnp.mean(x * x, axis=-1, keepdims=True) + RMS_EPS) * w


def _silu(x):
    return x * jax.nn.sigmoid(x)


def _log_sigmoid(x):
    return jnp.minimum(x, 0.0) - jnp.log1p(jnp.exp(-jnp.abs(x)))


def _lane_cumsum(x):
    n = x.shape[-1]
    lane = lax.broadcasted_iota(jnp.int32, x.shape, 1)
    shift = 1
    while shift < n:
        x = x + jnp.where(lane >= shift, pltpu.roll(x, shift, axis=1), 0.0)
        shift *= 2
    return x


def _ffn_kernel(*refs, final):
    if final:
        x_ref, nw_ref, wg_ref, wu_ref, wd_ref, fn_ref, o_ref, xn_ref = refs
    else:
        x_ref, nw_ref, wg_ref, wu_ref, wd_ref, o_ref, xn_ref = refs
    f = pl.program_id(1)

    @pl.when(f == 0)
    def _():
        xn_ref[...] = _rms_norm(x_ref[...], nw_ref[...]).astype(BF16)
        o_ref[...] = jnp.zeros_like(o_ref)

    xn = xn_ref[...]
    g = jnp.dot(xn, wg_ref[...].astype(BF16), preferred_element_type=F32)
    u = jnp.dot(xn, wu_ref[...].astype(BF16), preferred_element_type=F32)
    h = (_silu(g) * u).astype(BF16)
    o_ref[...] += jnp.dot(h, wd_ref[...].astype(BF16), preferred_element_type=F32)

    @pl.when(f == pl.num_programs(1) - 1)
    def _():
        y = x_ref[...] + 0.5 * o_ref[...]
        if final:
            y = _rms_norm(y, fn_ref[...])
        o_ref[...] = y


def _ffn(x, norm_w, w_gate, w_up, w_down, layer, final_norm=None, *, tm=2048, tf=256):
    m, d = x.shape
    dff = w_gate.shape[2]
    final = final_norm is not None
    in_specs = [
        pl.BlockSpec((tm, d), lambda i, f: (i, 0)),
        pl.BlockSpec((1, d), lambda i, f: (0, 0)),
        pl.BlockSpec((None, d, tf), lambda i, f: (layer, 0, f)),
        pl.BlockSpec((None, d, tf), lambda i, f: (layer, 0, f)),
        pl.BlockSpec((None, tf, d), lambda i, f: (layer, f, 0)),
    ]
    args = [x, norm_w, w_gate, w_up, w_down]
    if final:
        in_specs.append(pl.BlockSpec((1, d), lambda i, f: (0, 0)))
        args.append(final_norm)
    return pl.pallas_call(
        functools.partial(_ffn_kernel, final=final),
        grid=(m // tm, dff // tf),
        in_specs=in_specs,
        out_specs=pl.BlockSpec((tm, d), lambda i, f: (i, 0)),
        out_shape=jax.ShapeDtypeStruct((m, d), F32),
        scratch_shapes=[pltpu.VMEM((tm, d), BF16)],
        compiler_params=_params("parallel", "arbitrary"),
        name="ffn_final" if final else "ffn",
    )(*args)


def _proj_kernel(x_ref, nw_ref, wqk_ref, wvt_ref, wm_ref, wff_ref, bf_ref,
                 qk_ref, vt_ref, mxz_ref, cf_ref, carry_ref, *, tiles_per_seq):
    i = pl.program_id(0)
    xn = _rms_norm(x_ref[...], nw_ref[...]).astype(BF16)
    for grp in range(4):
        cols = slice(grp * ATTN_WIDTH, (grp + 1) * ATTN_WIDTH)
        r = jnp.dot(xn, wqk_ref[:, cols], preferred_element_type=F32)
        if grp % 2 == 0:
            r = r * Q_SCALE
        qk_ref[:, cols] = r.astype(BF16)
    for grp in range(2):
        rows = slice(grp * ATTN_WIDTH, (grp + 1) * ATTN_WIDTH)
        vt_ref[rows, :] = lax.dot_general(wvt_ref[rows, :], xn, NT_DIMS,
                                          preferred_element_type=F32).astype(BF16)
    for grp in range(2):
        cols = slice(grp * MLSTM_WIDTH, (grp + 1) * MLSTM_WIDTH)
        mxz_ref[:, cols] = jnp.dot(xn, wm_ref[:, cols], preferred_element_type=F32)

    ff = lax.dot_general(wff_ref[...], xn, NT_DIMS, preferred_element_type=F32)
    csum = _lane_cumsum(_log_sigmoid(ff + bf_ref[...]))

    @pl.when(i % tiles_per_seq == 0)
    def _():
        carry_ref[...] = jnp.zeros_like(carry_ref)

    csum = csum + carry_ref[:, 0:1]
    carry_ref[...] = jnp.broadcast_to(csum[:, -1:], carry_ref.shape)
    cf_ref[...] = -csum


def _proj(x, norm_w, w_qk, w_vt, w_m, w_ff_t, b_f, *, seq, tm=512):
    m, d = x.shape
    nh = w_ff_t.shape[0]
    const = lambda i: (0, 0)
    return pl.pallas_call(
        functools.partial(_proj_kernel, tiles_per_seq=seq // tm),
        grid=(m // tm,),
        in_specs=[
            pl.BlockSpec((tm, d), lambda i: (i, 0)),
            pl.BlockSpec((1, d), const),
            pl.BlockSpec(w_qk.shape, const),
            pl.BlockSpec(w_vt.shape, const),
            pl.BlockSpec(w_m.shape, const),
            pl.BlockSpec(w_ff_t.shape, const),
            pl.BlockSpec((nh, 1), const),
        ],
        out_specs=[
            pl.BlockSpec((tm, w_qk.shape[1]), lambda i: (i, 0)),
            pl.BlockSpec((w_vt.shape[0], tm), lambda i: (0, i)),
            pl.BlockSpec((tm, w_m.shape[1]), lambda i: (i, 0)),
            pl.BlockSpec((nh, tm), lambda i: (0, i)),
        ],
        out_shape=[
            jax.ShapeDtypeStruct((m, w_qk.shape[1]), BF16),
            jax.ShapeDtypeStruct((w_vt.shape[0], m), BF16),
            jax.ShapeDtypeStruct((m, w_m.shape[1]), F32),
            jax.ShapeDtypeStruct((nh, m), F32),
        ],
        scratch_shapes=[pltpu.VMEM((nh, LANES), F32)],
        compiler_params=_params("arbitrary"),
        name="proj",
    )(x, norm_w, w_qk, w_vt, w_m, w_ff_t, b_f)


def _flash_pair(q_ref, k_ref, vt_ref, bias_ref, cb_ref, qm_ref, s_ref, m_ref, l_ref, acc_ref, *, t, v_rows):
    qi = pl.program_id(2)
    seq = k_ref.shape[1]

    @pl.when(qi == 0)
    def _():
        for c in range(2):
            for blk in range(seq // LANES):
                sl = slice(blk * LANES, (blk + 1) * LANES)
                tile = jnp.broadcast_to(bias_ref[0, c:c + 1, sl] * LOG2E, (LANES, LANES))
                cb_ref[c, sl, :] = tile.T

    q = q_ref[0]
    lane = lax.broadcasted_iota(jnp.int32, q.shape, 1)
    half = LANES // 2
    qm_ref[0] = jnp.where(lane < half, q, jnp.zeros_like(q))
    qm_ref[1] = jnp.where(lane >= half, q, jnp.zeros_like(q))

    m_ref[...] = jnp.full_like(m_ref, -jnp.inf)
    l_ref[...] = jnp.zeros_like(l_ref)
    acc_ref[...] = jnp.zeros_like(acc_ref)

    def score_stage(j, slot):
        k_start = pl.multiple_of(j * t, t)
        k = k_ref[0, pl.ds(k_start, t), :]
        for c in range(2):
            s = lax.dot_general(k, qm_ref[c], NT_DIMS, preferred_element_type=F32)
            s_ref[slot, c] = s + jnp.tile(cb_ref[c, pl.ds(k_start, t), :], (1, t // LANES))

    def softmax_stage(j, slot, masked):
        k_start = pl.multiple_of(j * t, t)
        if masked:
            key = lax.broadcasted_iota(jnp.int32, (t, t), 0)
            qry = lax.broadcasted_iota(jnp.int32, (t, t), 1)
            keep = key <= qry
        for c in range(2):
            rows = v_rows[c]
            s = s_ref[slot, c]
            if masked:
                s = jnp.where(keep, s, -jnp.inf)
            m_old = m_ref[c]
            m_new = jnp.maximum(m_old, jnp.max(s, axis=0, keepdims=True))
            alpha = jnp.exp2(m_old - m_new)
            p = jnp.exp2(s - m_new)
            l_ref[c] = alpha * l_ref[c] + jnp.sum(p, axis=0, keepdims=True)
            pv = jnp.dot(vt_ref[rows, pl.ds(k_start, t)], p.astype(BF16), preferred_element_type=F32)
            acc_ref[c, rows, :] = alpha * acc_ref[c, rows, :] + pv
            m_ref[c] = m_new

    score_stage(0, 0)

    def body(pair, carry):
        j = 2 * pair
        score_stage(j + 1, 1)
        softmax_stage(j, 0, masked=False)
        score_stage(j + 2, 0)
        softmax_stage(j + 1, 1, masked=False)
        return carry

    lax.fori_loop(0, qi // 2, body, 0)

    @pl.when(qi % 2 == 1)
    def _():
        score_stage(qi, 1)
        softmax_stage(qi - 1, 0, masked=False)
        softmax_stage(qi, 1, masked=True)

    @pl.when(qi % 2 == 0)
    def _():
        softmax_stage(qi, 0, masked=True)

    return [acc_ref[c, v_rows[c], :] * (1.0 / l_ref[c]) for c in range(2)]


def _diff_attn_kernel(q_ref, k_ref, vt_ref, bias_ref, lq1_ref, lk1_ref, lq2_ref, lk2_ref, sub_ref,
                      o_ref, cb_ref, qm_ref, s_ref, m_ref, l_ref, acc_ref, *, t, lam_init):
    full = slice(0, LANES)
    a1, a2 = _flash_pair(q_ref, k_ref, vt_ref, bias_ref, cb_ref, qm_ref, s_ref, m_ref, l_ref, acc_ref,
                         t=t, v_rows=(full, full))
    lam = (jnp.exp(jnp.sum(lq1_ref[...] * lk1_ref[...], axis=1, keepdims=True))
           - jnp.exp(jnp.sum(lq2_ref[...] * lk2_ref[...], axis=1, keepdims=True)) + lam_init)
    d = (a1 - lam * a2).T
    o_ref[0] = (_rms_norm(d, sub_ref[...]) * (1.0 - lam_init)).astype(o_ref.dtype)


def _fox_attn_kernel(q_ref, k_ref, vt_ref, bias_ref, o_ref, cb_ref, qm_ref, s_ref, m_ref, l_ref, acc_ref, *, t):
    half = LANES // 2
    a_even, a_odd = _flash_pair(q_ref, k_ref, vt_ref, bias_ref, cb_ref, qm_ref, s_ref, m_ref, l_ref, acc_ref,
                                t=t, v_rows=(slice(0, half), slice(half, LANES)))
    o_ref[0] = jnp.concatenate([a_even, a_odd], axis=0).T.astype(o_ref.dtype)


def _attention(kernel, qk, vt, bias, extra, *, q_col, v_row, bias_per_batch, t=512, name):
    b, s, _ = qk.shape
    groups = ATTN_WIDTH // LANES
    if bias_per_batch:
        bias_map = lambda bi, g, i: (g, 0, bi)
    else:
        bias_map = lambda bi, g, i: (g, 0, 0)
    in_specs = [
        pl.BlockSpec((1, t, LANES), lambda bi, g, i: (bi, i, q_col + g)),
        pl.BlockSpec((1, s, LANES), lambda bi, g, i: (bi, 0, q_col + groups + g)),
        pl.BlockSpec((LANES, s), lambda bi, g, i: (v_row + g, bi)),
        pl.BlockSpec((1, 2, s), bias_map),
    ] + [pl.BlockSpec(e.shape, lambda bi, g, i: (0, 0)) for e in extra]
    return pl.pallas_call(
        functools.partial(kernel, t=t),
        grid=(b, groups, s // t),
        in_specs=in_specs,
        out_specs=pl.BlockSpec((1, t, LANES), lambda bi, g, i: (bi, i, g)),
        out_shape=jax.ShapeDtypeStruct((b, s, ATTN_WIDTH), BF16),
        scratch_shapes=[pltpu.VMEM((2, s, LANES), F32), pltpu.VMEM((2, t, LANES), BF16),
                        pltpu.VMEM((2, 2, t, t), F32), pltpu.VMEM((2, 1, t), F32),
                        pltpu.VMEM((2, 1, t), F32), pltpu.VMEM((2, LANES, t), F32)],
        compiler_params=_params("parallel", "parallel", "arbitrary"),
        name=name,
    )(qk, qk, vt, bias, *extra)


def _mlstm_kernel(cur_ref, prev_ref, z_ref, cw_ref, cb_ref, wqt_ref, wk_ref, wvt_ref, wif_ref, bif_ref,
                  nrm_ref, skip_ref, o_ref, xbuf_ref, ct_ref, n_ref, m_ref):
    ci = pl.program_id(1)

    @pl.when(ci == 0)
    def _():
        ct_ref[...] = jnp.zeros_like(ct_ref)
        n_ref[...] = jnp.zeros_like(n_ref)
        m_ref[...] = jnp.zeros_like(m_ref)

    for r in range(cur_ref.shape[0]):
        one = pl.ds(r, 1)
        _mlstm_chunk(ci, cur_ref.at[one], prev_ref.at[one], z_ref.at[one], cw_ref, cb_ref, wqt_ref, wk_ref,
                     wvt_ref, wif_ref, bif_ref, nrm_ref, skip_ref, o_ref.at[one], xbuf_ref.at[r],
                     ct_ref.at[r], n_ref.at[r], m_ref.at[r])


def _mlstm_chunk(ci, cur_ref, prev_ref, z_ref, cw_ref, cb_ref, wqt_ref, wk_ref, wvt_ref, wif_ref, bif_ref,
                 nrm_ref, skip_ref, o_ref, xbuf_ref, ct_ref, n_ref, m_ref):
    L = MLSTM_CHUNK
    dh = MLSTM_HEAD_DIM
    nh = MLSTM_HEADS

    cur = cur_ref[0]
    xbuf_ref[0:SUBLANES, :] = jnp.where(ci > 0, prev_ref[0], 0.0)
    xbuf_ref[SUBLANES:, :] = cur
    conv = cb_ref[...]
    for j in range(MLSTM_CONV):
        off = SUBLANES - (MLSTM_CONV - 1) + j
        conv = conv + cw_ref[j:j + 1, :] * xbuf_ref[off:off + L, :]
    xc = _silu(conv)

    qts, kss, vts = [], [], []
    ift = bif_ref[...]
    for h in range(nh):
        sl = slice(h * dh, (h + 1) * dh)
        xh = xc[:, sl].astype(BF16)
        vh = cur[:, sl].astype(BF16)
        qt = lax.dot_general(wqt_ref[h], xh, NT_DIMS, preferred_element_type=F32).astype(BF16)
        k = jnp.dot(xh, wk_ref[h], preferred_element_type=F32)
        vt = lax.dot_general(wvt_ref[h], vh, NT_DIMS, preferred_element_type=F32).astype(BF16)
        ift = ift + jnp.dot(wif_ref[:, sl], qt, preferred_element_type=F32)
        ift = ift + lax.dot_general(wif_ref[:, MLSTM_WIDTH + h * dh:MLSTM_WIDTH + (h + 1) * dh],
                                    k.astype(BF16), NT_DIMS, preferred_element_type=F32)
        ift = ift + jnp.dot(wif_ref[:, 2 * MLSTM_WIDTH + h * dh:2 * MLSTM_WIDTH + (h + 1) * dh], vt,
                            preferred_element_type=F32)
        qts.append(qt)
        kss.append((k * (dh ** -0.5)).astype(BF16))
        vts.append(vt)

    key = lax.broadcasted_iota(jnp.int32, (L, L), 0)
    qry = lax.broadcasted_iota(jnp.int32, (L, L), 1)
    causal = key <= qry
    li_rows = ift[0:nh]
    lf = _log_sigmoid(ift)
    tri = jnp.where(causal, 1.0, 0.0).astype(BF16)
    b_all = jnp.zeros_like(lf)
    rem = lf
    for _ in range(3):
        part = rem.astype(BF16)
        b_all = b_all + jnp.dot(part, tri, preferred_element_type=F32)
        rem = rem - part.astype(F32)
    b_rows = b_all[nh:2 * nh]

    for h in range(nh):
        sl = slice(h * dh, (h + 1) * dh)
        li = li_rows[h:h + 1]
        b = b_rows[h:h + 1]
        a = li - b
        a_keys = jnp.broadcast_to(a, (L, L)).T
        b_last = b[:, L - 1:L]
        m_prev = m_ref[h:h + 1, 0:1]
        qt, ks, vt = qts[h], kss[h], vts[h]
        ct_state = ct_ref[h]
        n_state = n_ref[h]

        d = jnp.where(causal, b + a_keys, -jnp.inf)
        inter = b + m_prev
        m_t = jnp.maximum(inter, jnp.max(d, axis=0, keepdims=True))
        w_inter = jnp.exp(inter - m_t)
        s = jnp.dot(ks, qt, preferred_element_type=F32) * jnp.exp(d - m_t)
        num = (w_inter * jnp.dot(ct_state.astype(BF16), qt, preferred_element_type=F32)
               + jnp.dot(vt, s.astype(BF16), preferred_element_type=F32))
        den = (w_inter * jnp.dot(n_state.astype(BF16), qt, preferred_element_type=F32)[0:1]
               + jnp.sum(s, axis=0, keepdims=True))
        hh = num * (1.0 / jnp.maximum(jnp.abs(den), jnp.exp(-m_t)))

        g = b_last + a
        m_new = jnp.maximum(b_last + m_prev, jnp.max(g, axis=1, keepdims=True))
        wk = jnp.exp(g - m_new)
        dec = jnp.exp(b_last + m_prev - m_new)
        ct_ref[h] = dec * ct_state + jnp.dot((vt.astype(F32) * wk).astype(BF16), ks,
                                             preferred_element_type=F32)
        n_ref[h] = dec * n_state + jnp.dot(jnp.broadcast_to(wk, (SUBLANES, L)).astype(BF16), ks,
                                           preferred_element_type=F32)
        m_ref[h:h + 1, :] = jnp.broadcast_to(m_new, (1, LANES))

        mu = jnp.mean(hh, axis=0, keepdims=True)
        cen = hh - mu
        var = jnp.mean(cen * cen, axis=0, keepdims=True)
        hn = (cen * lax.rsqrt(var + LN_EPS)).T * nrm_ref[:, sl]
        o_ref[0, :, sl] = ((hn + skip_ref[:, sl] * xc[:, sl]) * _silu(z_ref[0, :, sl])).astype(o_ref.dtype)


def _mlstm(mxz, conv_w, conv_b, wq, wk, wv, w_if_t, b_if, norm_w, skip_w, *, nb=2):
    b, s, _ = mxz.shape
    L = MLSTM_CHUNK
    w = MLSTM_WIDTH
    rows_per_chunk = L // SUBLANES
    const2 = lambda bi, c: (0, 0)
    const3 = lambda bi, c: (0, 0, 0)
    return pl.pallas_call(
        _mlstm_kernel,
        grid=(b // nb, s // L),
        in_specs=[
            pl.BlockSpec((nb, L, w), lambda bi, c: (bi, c, 0)),
            pl.BlockSpec((nb, SUBLANES, w), lambda bi, c: (bi, jnp.maximum(c * rows_per_chunk - 1, 0), 0)),
            pl.BlockSpec((nb, L, w), lambda bi, c: (bi, c, 1)),
            pl.BlockSpec(conv_w.shape, const2),
            pl.BlockSpec(conv_b.shape, const2),
            pl.BlockSpec(wq.shape, const3),
            pl.BlockSpec(wk.shape, const3),
            pl.BlockSpec(wv.shape, const3),
            pl.BlockSpec(w_if_t.shape, const2),
            pl.BlockSpec(b_if.shape, const2),
            pl.BlockSpec(norm_w.shape, const2),
            pl.BlockSpec(skip_w.shape, const2),
        ],
        out_specs=pl.BlockSpec((nb, L, w), lambda bi, c: (bi, c, 0)),
        out_shape=jax.ShapeDtypeStruct((b, s, w), BF16),
        scratch_shapes=[
            pltpu.VMEM((nb, SUBLANES + L, w), F32),
            pltpu.VMEM((nb, MLSTM_HEADS, MLSTM_HEAD_DIM, MLSTM_HEAD_DIM), F32),
            pltpu.VMEM((nb, MLSTM_HEADS, SUBLANES, MLSTM_HEAD_DIM), F32),
            pltpu.VMEM((nb, SUBLANES, LANES), F32),
        ],
        compiler_params=_params("parallel", "arbitrary"),
        name="mlstm",
    )(mxz, mxz, mxz, conv_w, conv_b, wq, wk, wv, w_if_t, b_if, norm_w, skip_w)


def _merge_kernel(x_ref, nw_ref, wg_ref, gb_ref, od_ref, of_ref, om_ref, wbd_ref, wbf_ref, wbm_ref,
                  wo_ref, o_ref):
    x = x_ref[...]
    d = x.shape[1]
    xn = _rms_norm(x, nw_ref[...]).astype(BF16)
    merged = None
    for br, (b_ref, w_ref) in enumerate(((od_ref, wbd_ref), (of_ref, wbf_ref), (om_ref, wbm_ref))):
        cols = slice(br * d, (br + 1) * d)
        gate = jax.nn.sigmoid(jnp.dot(xn, wg_ref[:, cols], preferred_element_type=F32) + gb_ref[:, cols])
        term = gate * jnp.dot(b_ref[...], w_ref[...], preferred_element_type=F32)
        merged = term if merged is None else merged + term
    o_ref[...] = x + jnp.dot(merged.astype(BF16), wo_ref[...], preferred_element_type=F32)


def _merge(x, norm_w, w_g, gate_b, o_diff, o_fox, o_mlstm, wbd, wbf, wbm, w_out, *, tm=512):
    m, d = x.shape
    const = lambda i: (0, 0)
    row = lambda i: (i, 0)
    return pl.pallas_call(
        _merge_kernel,
        grid=(m // tm,),
        in_specs=[
            pl.BlockSpec((tm, d), row),
            pl.BlockSpec((1, d), const),
            pl.BlockSpec(w_g.shape, const),
            pl.BlockSpec(gate_b.shape, const),
            pl.BlockSpec((tm, ATTN_WIDTH), row),
            pl.BlockSpec((tm, ATTN_WIDTH), row),
            pl.BlockSpec((tm, MLSTM_WIDTH), row),
            pl.BlockSpec(wbd.shape, const),
            pl.BlockSpec(wbf.shape, const),
            pl.BlockSpec(wbm.shape, const),
            pl.BlockSpec(w_out.shape, const),
        ],
        out_specs=pl.BlockSpec((tm, d), row),
        out_shape=jax.ShapeDtypeStruct((m, d), F32),
        compiler_params=_params("parallel"),
        name="merge",
    )(x, norm_w, w_g, gate_b, o_diff, o_fox, o_mlstm, wbd, wbf, wbm, w_out)


def _alibi_bias(seq):
    slopes = np.array([2.0 ** (-8.0 * (h + 1) / DIFF_HEADS) for h in range(DIFF_HEADS)], dtype=np.float32)
    pos = np.arange(seq, dtype=np.float32)
    per_head = slopes[:, None] * pos[None, :]
    return jnp.asarray(np.repeat(per_head[:, None, :], 2, axis=1))


def kernel(x, ffn1_norm, ffn1_w_gate, ffn1_w_up, ffn1_w_down, mix_norm, w_in, gate_bias, diff_lq1, diff_lk1, diff_lq2, diff_lk2, diff_subln, fox_b_f, mlstm_conv_w, mlstm_conv_b, mlstm_wq, mlstm_wk, mlstm_wv, mlstm_w_if, mlstm_b_if, mlstm_norm, mlstm_skip, w_branch_diff, w_branch_fox, w_branch_mlstm, w_out, ffn2_norm, ffn2_w_gate, ffn2_w_up, ffn2_w_down, final_norm):
    batch, seq, d = x.shape
    depth = w_in.shape[0]
    m = batch * seq
    bf = lambda a: a.astype(BF16)
    row = lambda a: a.reshape(1, -1)
    aw = ATTN_WIDTH
    ff0 = 6 * aw
    mx0 = ff0 + FOX_HEADS
    g0 = mx0 + 2 * MLSTM_WIDTH
    blocks = aw // LANES
    alibi = _alibi_bias(seq)

    x = x.reshape(m, d)
    for l in range(depth):
        x = _ffn(x, row(ffn1_norm[l]), ffn1_w_gate, ffn1_w_up, ffn1_w_down, l)

        w_in_l = w_in[l]
        w_qk = bf(jnp.concatenate([w_in_l[:, 0:2 * aw], w_in_l[:, 3 * aw:5 * aw]], axis=1))
        w_vt = bf(jnp.concatenate([w_in_l[:, 2 * aw:3 * aw], w_in_l[:, 5 * aw:6 * aw]], axis=1).T)
        qk, vt, mxz, cf = _proj(x, row(mix_norm[l]), w_qk, w_vt, bf(w_in_l[:, mx0:g0]),
                                bf(w_in_l[:, ff0:mx0].T), fox_b_f[l].reshape(-1, 1), seq=seq)
        qk = qk.reshape(batch, seq, 4 * aw)
        lam_init = 0.8 - 0.6 * math.exp(-0.3 * l)
        o_diff = _attention(
            functools.partial(_diff_attn_kernel, lam_init=lam_init), qk, vt, alibi,
            [row(diff_lq1[l]), row(diff_lk1[l]), row(diff_lq2[l]), row(diff_lk2[l]), row(diff_subln[l])],
            q_col=0, v_row=0, bias_per_batch=False, name="diff_attn")
        o_fox = _attention(_fox_attn_kernel, qk, vt, cf.reshape(FOX_HEADS // 2, 2, m), [],
                           q_col=2 * blocks, v_row=blocks, bias_per_batch=True, name="fox_attn")
        o_mlstm = _mlstm(mxz.reshape(batch, seq, 2 * MLSTM_WIDTH), mlstm_conv_w[l], row(mlstm_conv_b[l]),
                         bf(jnp.swapaxes(mlstm_wq[l], 1, 2)), bf(mlstm_wk[l]),
                         bf(jnp.swapaxes(mlstm_wv[l], 1, 2)), bf(mlstm_w_if[l].T),
                         mlstm_b_if[l].reshape(-1, 1), row(mlstm_norm[l]), row(mlstm_skip[l]))
        x = _merge(x, row(mix_norm[l]), bf(w_in_l[:, g0:]), gate_bias[l].reshape(1, -1),
                   o_diff.reshape(m, aw), o_fox.reshape(m, aw),
                   o_mlstm.reshape(m, MLSTM_WIDTH), bf(w_branch_diff[l]), bf(w_branch_fox[l]),
                   bf(w_branch_mlstm[l]), bf(w_out[l]))

        last = l == depth - 1
        x = _ffn(x, row(ffn2_norm[l]), ffn2_w_gate, ffn2_w_up, ffn2_w_down, l,
                 row(final_norm) if last else None)
    return x.reshape(batch, seq, d)
```

```python
import functools
import math

import numpy as np
import jax
import jax.numpy as jnp
from jax import lax
from jax.experimental import pallas as pl
from jax.experimental.pallas import tpu as pltpu

D_MODEL = 1024
D_FF = 2816
RMS_EPS = 1e-6
LN_EPS = 1e-5
N_BRANCH = 3
DIFF_HEADS = 4
ATTN_HEAD_DIM = 64
FOX_HEADS = 8
MLSTM_HEADS = 4
MLSTM_HEAD_DIM = 128
MLSTM_WIDTH = MLSTM_HEADS * MLSTM_HEAD_DIM
MLSTM_CONV = 4
MLSTM_CHUNK = 128
ATTN_WIDTH = 512
LANES = 128
SUBLANES = 8
BF16_ROWS = 16
SCORE_SLOTS = 3
BIAS_TERMS = 3
VMEM_LIMIT_BYTES = 56 * 1024 * 1024
LOG2E = math.log2(math.e)
Q_SCALE = ATTN_HEAD_DIM ** -0.5 * LOG2E

F32 = jnp.float32
BF16 = jnp.bfloat16
NT_DIMS = (((1,), (1,)), ((), ()))
TN_DIMS = (((0,), (0,)), ((), ()))


def _params(*semantics):
    return pltpu.CompilerParams(dimension_semantics=semantics, vmem_limit_bytes=VMEM_LIMIT_BYTES)


def _rms_norm(x, w):
    return x * lax.rsqrt(jnp.mean(x * x, axis=-1, keepdims=True) + RMS_EPS) * w


def _silu(x):
    return x * jax.nn.sigmoid(x)


def _log_sigmoid(x):
    return jnp.minimum(x, 0.0) - jnp.log1p(jnp.exp(-jnp.abs(x)))


def _lane_cumsum(x):
    n = x.shape[-1]
    lane = lax.broadcasted_iota(jnp.int32, x.shape, 1)
    shift = 1
    while shift < n:
        x = x + jnp.where(lane >= shift, pltpu.roll(x, shift, axis=1), 0.0)
        shift *= 2
    return x


def _ffn_kernel(*refs, final):
    if final:
        x_ref, nw_ref, wg_ref, wu_ref, wd_ref, fn_ref, o_ref, xn_ref = refs
    else:
        x_ref, nw_ref, wg_ref, wu_ref, wd_ref, o_ref, xn_ref = refs
    f = pl.program_id(1)

    @pl.when(f == 0)
    def _():
        xn_ref[...] = _rms_norm(x_ref[...], nw_ref[...]).astype(BF16)
        o_ref[...] = jnp.zeros_like(o_ref)

    xn = xn_ref[...]
    g = jnp.dot(xn, wg_ref[...].astype(BF16), preferred_element_type=F32)
    u = jnp.dot(xn, wu_ref[...].astype(BF16), preferred_element_type=F32)
    h = (_silu(g) * u).astype(BF16)
    o_ref[...] += jnp.dot(h, wd_ref[...].astype(BF16), preferred_element_type=F32)

    @pl.when(f == pl.num_programs(1) - 1)
    def _():
        y = x_ref[...] + 0.5 * o_ref[...]
        if final:
            y = _rms_norm(y, fn_ref[...])
        o_ref[...] = y


def _ffn(x, norm_w, w_gate, w_up, w_down, layer, final_norm=None, *, tm=2048, tf=256):
    m, d = x.shape
    dff = w_gate.shape[2]
    final = final_norm is not None
    in_specs = [
        pl.BlockSpec((tm, d), lambda i, f: (i, 0)),
        pl.BlockSpec((1, d), lambda i, f: (0, 0)),
        pl.BlockSpec((None, d, tf), lambda i, f: (layer, 0, f)),
        pl.BlockSpec((None, d, tf), lambda i, f: (layer, 0, f)),
        pl.BlockSpec((None, tf, d), lambda i, f: (layer, f, 0)),
    ]
    args = [x, norm_w, w_gate, w_up, w_down]
    if final:
        in_specs.append(pl.BlockSpec((1, d), lambda i, f: (0, 0)))
        args.append(final_norm)
    return pl.pallas_call(
        functools.partial(_ffn_kernel, final=final),
        grid=(m // tm, dff // tf),
        in_specs=in_specs,
        out_specs=pl.BlockSpec((tm, d), lambda i, f: (i, 0)),
        out_shape=jax.ShapeDtypeStruct((m, d), F32),
        scratch_shapes=[pltpu.VMEM((tm, d), BF16)],
        compiler_params=_params("parallel", "arbitrary"),
        name="ffn_final" if final else "ffn",
    )(*args)


def _proj_kernel(x_ref, nw_ref, wqk_ref, wvt_ref, wm_ref, wff_ref, bf_ref,
                 qk_ref, vt_ref, mxz_ref, cf_ref, carry_ref, *, tiles_per_seq):
    i = pl.program_id(0)
    xn = _rms_norm(x_ref[...], nw_ref[...]).astype(BF16)
    for grp in range(4):
        cols = slice(grp * ATTN_WIDTH, (grp + 1) * ATTN_WIDTH)
        r = jnp.dot(xn, wqk_ref[:, cols], preferred_element_type=F32)
        if grp % 2 == 0:
            r = r * Q_SCALE
        qk_ref[:, cols] = r.astype(BF16)
    for grp in range(2):
        rows = slice(grp * ATTN_WIDTH, (grp + 1) * ATTN_WIDTH)
        vt_ref[rows, :] = lax.dot_general(wvt_ref[rows, :], xn, NT_DIMS,
                                          preferred_element_type=F32).astype(BF16)
    for grp in range(2):
        cols = slice(grp * MLSTM_WIDTH, (grp + 1) * MLSTM_WIDTH)
        mxz_ref[:, cols] = jnp.dot(xn, wm_ref[:, cols], preferred_element_type=F32)

    ff = lax.dot_general(wff_ref[...], xn, NT_DIMS, preferred_element_type=F32)
    csum = _lane_cumsum(_log_sigmoid(ff + bf_ref[...]))

    @pl.when(i % tiles_per_seq == 0)
    def _():
        carry_ref[...] = jnp.zeros_like(carry_ref)

    csum = csum + carry_ref[:, 0:1]
    carry_ref[...] = jnp.broadcast_to(csum[:, -1:], carry_ref.shape)
    cf_ref[...] = -csum


def _proj(x, norm_w, w_qk, w_vt, w_m, w_ff_t, b_f, *, seq, tm=512):
    m, d = x.shape
    nh = w_ff_t.shape[0]
    const = lambda i: (0, 0)
    return pl.pallas_call(
        functools.partial(_proj_kernel, tiles_per_seq=seq // tm),
        grid=(m // tm,),
        in_specs=[
            pl.BlockSpec((tm, d), lambda i: (i, 0)),
            pl.BlockSpec((1, d), const),
            pl.BlockSpec(w_qk.shape, const),
            pl.BlockSpec(w_vt.shape, const),
            pl.BlockSpec(w_m.shape, const),
            pl.BlockSpec(w_ff_t.shape, const),
            pl.BlockSpec((nh, 1), const),
        ],
        out_specs=[
            pl.BlockSpec((tm, w_qk.shape[1]), lambda i: (i, 0)),
            pl.BlockSpec((w_vt.shape[0], tm), lambda i: (0, i)),
            pl.BlockSpec((tm, w_m.shape[1]), lambda i: (i, 0)),
            pl.BlockSpec((nh, tm), lambda i: (0, i)),
        ],
        out_shape=[
            jax.ShapeDtypeStruct((m, w_qk.shape[1]), BF16),
            jax.ShapeDtypeStruct((w_vt.shape[0], m), BF16),
            jax.ShapeDtypeStruct((m, w_m.shape[1]), F32),
            jax.ShapeDtypeStruct((nh, m), F32),
        ],
        scratch_shapes=[pltpu.VMEM((nh, LANES), F32)],
        compiler_params=_params("arbitrary"),
        name="proj",
    )(x, norm_w, w_qk, w_vt, w_m, w_ff_t, b_f)


def _flash_pair(q_ref, k_ref, vt_ref, bias_ref, kaug_ref, vaug_ref, qm_ref, s_ref, acc_ref, finish, *, t, v_rows):
    seq = k_ref.shape[1]
    dr = v_rows[0].stop - v_rows[0].start
    nq = seq // t
    half = LANES // 2

    kaug_ref[:, 0:LANES] = k_ref[0]
    lane = lax.broadcasted_iota(jnp.int32, (LANES, LANES), 1)
    for blk in range(seq // LANES):
        sl = slice(blk * LANES, (blk + 1) * LANES)
        ext = jnp.zeros((LANES, LANES), F32)
        for c in range(2):
            rem = jnp.broadcast_to(bias_ref[0, c:c + 1, sl] * LOG2E, (LANES, LANES)).T
            for term in range(BIAS_TERMS):
                part = rem.astype(BF16).astype(F32)
                ext = jnp.where(lane == BIAS_TERMS * c + term, part, ext)
                rem = rem - part
        kaug_ref[sl, LANES:] = ext.astype(BF16)
    for c in range(2):
        vaug_ref[c, 0:dr, :] = vt_ref[v_rows[c], :]
        vaug_ref[c, dr:, :] = jnp.ones((BF16_ROWS, seq), BF16)

    qlane = lax.broadcasted_iota(jnp.int32, (seq, LANES), 1)
    q = q_ref[0]
    for c in range(2):
        qm_ref[c, :, 0:LANES] = jnp.where((qlane >= c * half) & (qlane < (c + 1) * half), q, jnp.zeros_like(q))
        picks_bias = (qlane >= BIAS_TERMS * c) & (qlane < BIAS_TERMS * (c + 1))
        qm_ref[c, :, LANES:] = jnp.where(picks_bias, 1.0, 0.0).astype(BF16)

    key = lax.broadcasted_iota(jnp.int32, (t, t), 0)
    qry = lax.broadcasted_iota(jnp.int32, (t, t), 1)
    keep = key <= qry

    step = 0
    for i in range(nq):
        qs = slice(i * t, (i + 1) * t)
        m = [jnp.full((1, t), -jnp.inf, F32) for _ in range(2)]
        for j in range(i + 1):
            ks = slice(j * t, (j + 1) * t)
            slot = step % s_ref.shape[0]
            step += 1
            for c in range(2):
                s_ref[slot, c] = lax.dot_general(kaug_ref[ks, :], qm_ref[c, qs, :], NT_DIMS,
                                                 preferred_element_type=F32)
            for c in range(2):
                s = s_ref[slot, c]
                if j == i:
                    s = jnp.where(keep, s, -jnp.inf)
                m_new = jnp.maximum(m[c], jnp.max(s, axis=0, keepdims=True))
                p = jnp.exp2(s - m_new).astype(BF16)
                pv = jnp.dot(vaug_ref[c, :, ks], p, preferred_element_type=F32)
                if j == 0:
                    acc_ref[i, c] = pv
                else:
                    acc_ref[i, c] = jnp.exp2(m[c] - m_new) * acc_ref[i, c] + pv
                m[c] = m_new
        finish(i, *[acc_ref[i, c, 0:dr, :] * (1.0 / acc_ref[i, c, dr:dr + 1, :]) for c in range(2)])


def _diff_attn_kernel(q_ref, k_ref, vt_ref, bias_ref, lq1_ref, lk1_ref, lq2_ref, lk2_ref, sub_ref,
                      o_ref, *scratch, t, lam_init):
    lam = (jnp.exp(jnp.sum(lq1_ref[...] * lk1_ref[...], axis=1, keepdims=True))
           - jnp.exp(jnp.sum(lq2_ref[...] * lk2_ref[...], axis=1, keepdims=True)) + lam_init)

    def finish(i, a1, a2):
        d = (a1 - lam * a2).T
        o_ref[0, i * t:(i + 1) * t, :] = (_rms_norm(d, sub_ref[...]) * (1.0 - lam_init)).astype(o_ref.dtype)

    full = slice(0, LANES)
    _flash_pair(q_ref, k_ref, vt_ref, bias_ref, *scratch, finish, t=t, v_rows=(full, full))


def _fox_attn_kernel(q_ref, k_ref, vt_ref, bias_ref, o_ref, *scratch, t):
    def finish(i, a_even, a_odd):
        o_ref[0, i * t:(i + 1) * t, :] = jnp.concatenate([a_even, a_odd], axis=0).T.astype(o_ref.dtype)

    half = LANES // 2
    _flash_pair(q_ref, k_ref, vt_ref, bias_ref, *scratch, finish, t=t,
                v_rows=(slice(0, half), slice(half, LANES)))


def _attention(kernel, qk, vt, bias, extra, *, q_col, v_row, v_rows, bias_per_batch, t=512, name):
    b, s, _ = qk.shape
    groups = ATTN_WIDTH // LANES
    if bias_per_batch:
        bias_map = lambda bi, g: (g, 0, bi)
    else:
        bias_map = lambda bi, g: (g, 0, 0)
    in_specs = [
        pl.BlockSpec((1, s, LANES), lambda bi, g: (bi, 0, q_col + g)),
        pl.BlockSpec((1, s, LANES), lambda bi, g: (bi, 0, q_col + groups + g)),
        pl.BlockSpec((LANES, s), lambda bi, g: (v_row + g, bi)),
        pl.BlockSpec((1, 2, s), bias_map),
    ] + [pl.BlockSpec(e.shape, lambda bi, g: (0, 0)) for e in extra]
    rows = v_rows + BF16_ROWS
    return pl.pallas_call(
        functools.partial(kernel, t=t),
        grid=(b, groups),
        in_specs=in_specs,
        out_specs=pl.BlockSpec((1, s, LANES), lambda bi, g: (bi, 0, g)),
        out_shape=jax.ShapeDtypeStruct((b, s, ATTN_WIDTH), BF16),
        scratch_shapes=[pltpu.VMEM((s, 2 * LANES), BF16), pltpu.VMEM((2, rows, s), BF16),
                        pltpu.VMEM((2, s, 2 * LANES), BF16), pltpu.VMEM((SCORE_SLOTS, 2, t, t), F32),
                        pltpu.VMEM((s // t, 2, rows, t), F32)],
        compiler_params=_params("parallel", "parallel"),
        name=name,
    )(qk, qk, vt, bias, *extra)


def _mlstm_kernel(cur_ref, prev_ref, z_ref, cw_ref, cb_ref, wqt_ref, wk_ref, wvt_ref, wif_ref, bif_ref,
                  nrm_ref, skip_ref, o_ref, xbuf_ref, ct_ref, n_ref, m_ref):
    ci = pl.program_id(1)

    @pl.when(ci == 0)
    def _():
        ct_ref[...] = jnp.zeros_like(ct_ref)
        n_ref[...] = jnp.zeros_like(n_ref)
        m_ref[...] = jnp.zeros_like(m_ref)

    for r in range(cur_ref.shape[0]):
        one = pl.ds(r, 1)
        _mlstm_chunk(ci, cur_ref.at[one], prev_ref.at[one], z_ref.at[one], cw_ref, cb_ref, wqt_ref, wk_ref,
                     wvt_ref, wif_ref, bif_ref, nrm_ref, skip_ref, o_ref.at[one], xbuf_ref.at[r],
                     ct_ref.at[r], n_ref.at[r], m_ref.at[r])


def _mlstm_chunk(ci, cur_ref, prev_ref, z_ref, cw_ref, cb_ref, wqt_ref, wk_ref, wvt_ref, wif_ref, bif_ref,
                 nrm_ref, skip_ref, o_ref, xbuf_ref, ct_ref, n_ref, m_ref):
    L = MLSTM_CHUNK
    dh = MLSTM_HEAD_DIM
    nh = MLSTM_HEADS

    cur = cur_ref[0]
    xbuf_ref[0:SUBLANES, :] = jnp.where(ci > 0, prev_ref[0], 0.0)
    xbuf_ref[SUBLANES:, :] = cur
    conv = cb_ref[...]
    for j in range(MLSTM_CONV):
        off = SUBLANES - (MLSTM_CONV - 1) + j
        conv = conv + cw_ref[j:j + 1, :] * xbuf_ref[off:off + L, :]
    xc = _silu(conv)

    qts, kss, vts = [], [], []
    ift = bif_ref[...]
    for h in range(nh):
        sl = slice(h * dh, (h + 1) * dh)
        xh = xc[:, sl].astype(BF16)
        vh = cur[:, sl].astype(BF16)
        qt = lax.dot_general(wqt_ref[h], xh, NT_DIMS, preferred_element_type=F32).astype(BF16)
        k = jnp.dot(xh, wk_ref[h], preferred_element_type=F32)
        vt = lax.dot_general(wvt_ref[h], vh, NT_DIMS, preferred_element_type=F32).astype(BF16)
        ift = ift + jnp.dot(wif_ref[:, sl], qt, preferred_element_type=F32)
        ift = ift + lax.dot_general(wif_ref[:, MLSTM_WIDTH + h * dh:MLSTM_WIDTH + (h + 1) * dh],
                                    k.astype(BF16), NT_DIMS, preferred_element_type=F32)
        ift = ift + jnp.dot(wif_ref[:, 2 * MLSTM_WIDTH + h * dh:2 * MLSTM_WIDTH + (h + 1) * dh], vt,
                            preferred_element_type=F32)
        qts.append(qt)
        kss.append((k * (dh ** -0.5)).astype(BF16))
        vts.append(vt)

    key = lax.broadcasted_iota(jnp.int32, (L, L), 0)
    qry = lax.broadcasted_iota(jnp.int32, (L, L), 1)
    causal = key <= qry
    li_rows = ift[0:nh]
    lf = _log_sigmoid(ift)
    tri = jnp.where(causal, 1.0, 0.0).astype(BF16)
    b_all = jnp.zeros_like(lf)
    rem = lf
    for _ in range(3):
        part = rem.astype(BF16)
        b_all = b_all + jnp.dot(part, tri, preferred_element_type=F32)
        rem = rem - part.astype(F32)
    b_rows = b_all[nh:2 * nh]

    for h in range(nh):
        sl = slice(h * dh, (h + 1) * dh)
        li = li_rows[h:h + 1]
        b = b_rows[h:h + 1]
        a = li - b
        a_keys = jnp.broadcast_to(a, (L, L)).T
        b_last = b[:, L - 1:L]
        m_prev = m_ref[h:h + 1, 0:1]
        qt, ks, vt = qts[h], kss[h], vts[h]
        ct_state = ct_ref[h]
        n_state = n_ref[h]

        d = jnp.where(causal, b + a_keys, -jnp.inf)
        inter = b + m_prev
        m_t = jnp.maximum(inter, jnp.max(d, axis=0, keepdims=True))
        w_inter = jnp.exp(inter - m_t)
        s = jnp.dot(ks, qt, preferred_element_type=F32) * jnp.exp(d - m_t)
        num = (w_inter * jnp.dot(ct_state.astype(BF16), qt, preferred_element_type=F32)
               + jnp.dot(vt, s.astype(BF16), preferred_element_type=F32))
        den = (w_inter * jnp.dot(n_state.astype(BF16), qt, preferred_element_type=F32)[0:1]
               + jnp.sum(s, axis=0, keepdims=True))
        hh = num * (1.0 / jnp.maximum(jnp.abs(den), jnp.exp(-m_t)))

        g = b_last + a
        m_new = jnp.maximum(b_last + m_prev, jnp.max(g, axis=1, keepdims=True))
        wk = jnp.exp(g - m_new)
        dec = jnp.exp(b_last + m_prev - m_new)
        ct_ref[h] = dec * ct_state + jnp.dot((vt.astype(F32) * wk).astype(BF16), ks,
                                             preferred_element_type=F32)
        n_ref[h] = dec * n_state + jnp.dot(jnp.broadcast_to(wk, (SUBLANES, L)).astype(BF16), ks,
                                           preferred_element_type=F32)
        m_ref[h:h + 1, :] = jnp.broadcast_to(m_new, (1, LANES))

        mu = jnp.mean(hh, axis=0, keepdims=True)
        cen = hh - mu
        var = jnp.mean(cen * cen, axis=0, keepdims=True)
        hn = (cen * lax.rsqrt(var + LN_EPS)).T * nrm_ref[:, sl]
        o_ref[0, :, sl] = ((hn + skip_ref[:, sl] * xc[:, sl]) * _silu(z_ref[0, :, sl])).astype(o_ref.dtype)


def _mlstm(mxz, conv_w, conv_b, wq, wk, wv, w_if_t, b_if, norm_w, skip_w, *, nb=2):
    b, s, _ = mxz.shape
    L = MLSTM_CHUNK
    w = MLSTM_WIDTH
    rows_per_chunk = L // SUBLANES
    const2 = lambda bi, c: (0, 0)
    const3 = lambda bi, c: (0, 0, 0)
    return pl.pallas_call(
        _mlstm_kernel,
        grid=(b // nb, s // L),
        in_specs=[
            pl.BlockSpec((nb, L, w), lambda bi, c: (bi, c, 0)),
            pl.BlockSpec((nb, SUBLANES, w), lambda bi, c: (bi, jnp.maximum(c * rows_per_chunk - 1, 0), 0)),
            pl.BlockSpec((nb, L, w), lambda bi, c: (bi, c, 1)),
            pl.BlockSpec(conv_w.shape, const2),
            pl.BlockSpec(conv_b.shape, const2),
            pl.BlockSpec(wq.shape, const3),
            pl.BlockSpec(wk.shape, const3),
            pl.BlockSpec(wv.shape, const3),
            pl.BlockSpec(w_if_t.shape, const2),
            pl.BlockSpec(b_if.shape, const2),
            pl.BlockSpec(norm_w.shape, const2),
            pl.BlockSpec(skip_w.shape, const2),
        ],
        out_specs=pl.BlockSpec((nb, L, w), lambda bi, c: (bi, c, 0)),
        out_shape=jax.ShapeDtypeStruct((b, s, w), BF16),
        scratch_shapes=[
            pltpu.VMEM((nb, SUBLANES + L, w), F32),
            pltpu.VMEM((nb, MLSTM_HEADS, MLSTM_HEAD_DIM, MLSTM_HEAD_DIM), F32),
            pltpu.VMEM((nb, MLSTM_HEADS, SUBLANES, MLSTM_HEAD_DIM), F32),
            pltpu.VMEM((nb, SUBLANES, LANES), F32),
        ],
        compiler_params=_params("parallel", "arbitrary"),
        name="mlstm",
    )(mxz, mxz, mxz, conv_w, conv_b, wq, wk, wv, w_if_t, b_if, norm_w, skip_w)


def _merge_kernel(x_ref, nw_ref, wg_ref, gb_ref, od_ref, of_ref, om_ref, wbd_ref, wbf_ref, wbm_ref,
                  wo_ref, o_ref):
    x = x_ref[...]
    d = x.shape[1]
    xn = _rms_norm(x, nw_ref[...]).astype(BF16)
    merged = None
    for br, (b_ref, w_ref) in enumerate(((od_ref, wbd_ref), (of_ref, wbf_ref), (om_ref, wbm_ref))):
        cols = slice(br * d, (br + 1) * d)
        gate = jax.nn.sigmoid(jnp.dot(xn, wg_ref[:, cols], preferred_element_type=F32) + gb_ref[:, cols])
        term = gate * jnp.dot(b_ref[...], w_ref[...], preferred_element_type=F32)
        merged = term if merged is None else merged + term
    o_ref[...] = x + jnp.dot(merged.astype(BF16), wo_ref[...], preferred_element_type=F32)


def _merge(x, norm_w, w_g, gate_b, o_diff, o_fox, o_mlstm, wbd, wbf, wbm, w_out, *, tm=512):
    m, d = x.shape
    const = lambda i: (0, 0)
    row = lambda i: (i, 0)
    return pl.pallas_call(
        _merge_kernel,
        grid=(m // tm,),
        in_specs=[
            pl.BlockSpec((tm, d), row),
            pl.BlockSpec((1, d), const),
            pl.BlockSpec(w_g.shape, const),
            pl.BlockSpec(gate_b.shape, const),
            pl.BlockSpec((tm, ATTN_WIDTH), row),
            pl.BlockSpec((tm, ATTN_WIDTH), row),
            pl.BlockSpec((tm, MLSTM_WIDTH), row),
            pl.BlockSpec(wbd.shape, const),
            pl.BlockSpec(wbf.shape, const),
            pl.BlockSpec(wbm.shape, const),
            pl.BlockSpec(w_out.shape, const),
        ],
        out_specs=pl.BlockSpec((tm, d), row),
        out_shape=jax.ShapeDtypeStruct((m, d), F32),
        compiler_params=_params("parallel"),
        name="merge",
    )(x, norm_w, w_g, gate_b, o_diff, o_fox, o_mlstm, wbd, wbf, wbm, w_out)


def _alibi_bias(seq):
    slopes = np.array([2.0 ** (-8.0 * (h + 1) / DIFF_HEADS) for h in range(DIFF_HEADS)], dtype=np.float32)
    pos = np.arange(seq, dtype=np.float32)
    per_head = slopes[:, None] * pos[None, :]
    return jnp.asarray(np.repeat(per_head[:, None, :], 2, axis=1))


def kernel(x, ffn1_norm, ffn1_w_gate, ffn1_w_up, ffn1_w_down, mix_norm, w_in, gate_bias, diff_lq1, diff_lk1, diff_lq2, diff_lk2, diff_subln, fox_b_f, mlstm_conv_w, mlstm_conv_b, mlstm_wq, mlstm_wk, mlstm_wv, mlstm_w_if, mlstm_b_if, mlstm_norm, mlstm_skip, w_branch_diff, w_branch_fox, w_branch_mlstm, w_out, ffn2_norm, ffn2_w_gate, ffn2_w_up, ffn2_w_down, final_norm):
    batch, seq, d = x.shape
    depth = w_in.shape[0]
    m = batch * seq
    bf = lambda a: a.astype(BF16)
    row = lambda a: a.reshape(1, -1)
    aw = ATTN_WIDTH
    ff0 = 6 * aw
    mx0 = ff0 + FOX_HEADS
    g0 = mx0 + 2 * MLSTM_WIDTH
    blocks = aw // LANES
    alibi = _alibi_bias(seq)

    x = x.reshape(m, d)
    for l in range(depth):
        x = _ffn(x, row(ffn1_norm[l]), ffn1_w_gate, ffn1_w_up, ffn1_w_down, l)

        w_in_l = w_in[l]
        w_qk = bf(jnp.concatenate([w_in_l[:, 0:2 * aw], w_in_l[:, 3 * aw:5 * aw]], axis=1))
        w_vt = bf(jnp.concatenate([w_in_l[:, 2 * aw:3 * aw], w_in_l[:, 5 * aw:6 * aw]], axis=1).T)
        qk, vt, mxz, cf = _proj(x, row(mix_norm[l]), w_qk, w_vt, bf(w_in_l[:, mx0:g0]),
                                bf(w_in_l[:, ff0:mx0].T), fox_b_f[l].reshape(-1, 1), seq=seq)
        qk = qk.reshape(batch, seq, 4 * aw)
        lam_init = 0.8 - 0.6 * math.exp(-0.3 * l)
        o_diff = _attention(
            functools.partial(_diff_attn_kernel, lam_init=lam_init), qk, vt, alibi,
            [row(diff_lq1[l]), row(diff_lk1[l]), row(diff_lq2[l]), row(diff_lk2[l]), row(diff_subln[l])],
            q_col=0, v_row=0, v_rows=LANES, bias_per_batch=False, name="diff_attn")
        o_fox = _attention(_fox_attn_kernel, qk, vt, cf.reshape(FOX_HEADS // 2, 2, m), [],
                           q_col=2 * blocks, v_row=blocks, v_rows=LANES // 2, bias_per_batch=True,
                           name="fox_attn")
        o_mlstm = _mlstm(mxz.reshape(batch, seq, 2 * MLSTM_WIDTH), mlstm_conv_w[l], row(mlstm_conv_b[l]),
                         bf(jnp.swapaxes(mlstm_wq[l], 1, 2)), bf(mlstm_wk[l]),
                         bf(jnp.swapaxes(mlstm_wv[l], 1, 2)), bf(mlstm_w_if[l].T),
                         mlstm_b_if[l].reshape(-1, 1), row(mlstm_norm[l]), row(mlstm_skip[l]))
        x = _merge(x, row(mix_norm[l]), bf(w_in_l[:, g0:]), gate_bias[l].reshape(1, -1),
                   o_diff.reshape(m, aw), o_fox.reshape(m, aw),
                   o_mlstm.reshape(m, MLSTM_WIDTH), bf(w_branch_diff[l]), bf(w_branch_fox[l]),
                   bf(w_branch_mlstm[l]), bf(w_out[l]))

        last = l == depth - 1
        x = _ffn(x, row(ffn2_norm[l]), ffn2_w_gate, ffn2_w_up, ffn2_w_down, l,
                 row(final_norm) if last else None)
    return x.reshape(batch, seq, d)
```

```python
import functools
import math

import numpy as np
import jax
import jax.numpy as jnp
from jax import lax
from jax.experimental import pallas as pl
from jax.experimental.pallas import tpu as pltpu

D_MODEL = 1024
D_FF = 2816
RMS_EPS = 1e-6
LN_EPS = 1e-5
N_BRANCH = 3
DIFF_HEADS = 4
ATTN_HEAD_DIM = 64
FOX_HEADS = 8
MLSTM_HEADS = 4
MLSTM_HEAD_DIM = 128
MLSTM_WIDTH = MLSTM_HEADS * MLSTM_HEAD_DIM
MLSTM_CONV = 4
MLSTM_CHUNK = 128
ATTN_WIDTH = 512
LANES = 128
SUBLANES = 8
BF16_ROWS = 16
SCORE_SLOTS = 4
BIAS_TERMS = 3
VMEM_LIMIT_BYTES = 56 * 1024 * 1024
LOG2E = math.log2(math.e)
Q_SCALE = ATTN_HEAD_DIM ** -0.5 * LOG2E

F32 = jnp.float32
BF16 = jnp.bfloat16
NT_DIMS = (((1,), (1,)), ((), ()))
TN_DIMS = (((0,), (0,)), ((), ()))


def _params(*semantics):
    return pltpu.CompilerParams(dimension_semantics=semantics, vmem_limit_bytes=VMEM_LIMIT_BYTES)


def _rms_norm(x, w):
    return x * lax.rsqrt(jnp.mean(x * x, axis=-1, keepdims=True) + RMS_EPS) * w


def _silu(x):
    return x * jax.nn.sigmoid(x)


def _log_sigmoid(x):
    return jnp.minimum(x, 0.0) - jnp.log1p(jnp.exp(-jnp.abs(x)))


def _lane_cumsum(x):
    n = x.shape[-1]
    lane = lax.broadcasted_iota(jnp.int32, x.shape, 1)
    shift = 1
    while shift < n:
        x = x + jnp.where(lane >= shift, pltpu.roll(x, shift, axis=1), 0.0)
        shift *= 2
    return x


def _ffn_kernel(*refs, final):
    if final:
        x_ref, nw_ref, wg_ref, wu_ref, wd_ref, fn_ref, o_ref, xn_ref = refs
    else:
        x_ref, nw_ref, wg_ref, wu_ref, wd_ref, o_ref, xn_ref = refs
    f = pl.program_id(1)

    @pl.when(f == 0)
    def _():
        xn_ref[...] = _rms_norm(x_ref[...], nw_ref[...]).astype(BF16)
        o_ref[...] = jnp.zeros_like(o_ref)

    xn = xn_ref[...]
    g = jnp.dot(xn, wg_ref[...].astype(BF16), preferred_element_type=F32)
    u = jnp.dot(xn, wu_ref[...].astype(BF16), preferred_element_type=F32)
    h = (_silu(g) * u).astype(BF16)
    o_ref[...] += jnp.dot(h, wd_ref[...].astype(BF16), preferred_element_type=F32)

    @pl.when(f == pl.num_programs(1) - 1)
    def _():
        y = x_ref[...] + 0.5 * o_ref[...]
        if final:
            y = _rms_norm(y, fn_ref[...])
        o_ref[...] = y


def _ffn(x, norm_w, w_gate, w_up, w_down, layer, final_norm=None, *, tm=2048, tf=256):
    m, d = x.shape
    dff = w_gate.shape[2]
    final = final_norm is not None
    in_specs = [
        pl.BlockSpec((tm, d), lambda i, f: (i, 0)),
        pl.BlockSpec((1, d), lambda i, f: (0, 0)),
        pl.BlockSpec((None, d, tf), lambda i, f: (layer, 0, f)),
        pl.BlockSpec((None, d, tf), lambda i, f: (layer, 0, f)),
        pl.BlockSpec((None, tf, d), lambda i, f: (layer, f, 0)),
    ]
    args = [x, norm_w, w_gate, w_up, w_down]
    if final:
        in_specs.append(pl.BlockSpec((1, d), lambda i, f: (0, 0)))
        args.append(final_norm)
    return pl.pallas_call(
        functools.partial(_ffn_kernel, final=final),
        grid=(m // tm, dff // tf),
        in_specs=in_specs,
        out_specs=pl.BlockSpec((tm, d), lambda i, f: (i, 0)),
        out_shape=jax.ShapeDtypeStruct((m, d), F32),
        scratch_shapes=[pltpu.VMEM((tm, d), BF16)],
        compiler_params=_params("parallel", "arbitrary"),
        name="ffn_final" if final else "ffn",
    )(*args)


def _proj_kernel(x_ref, nw_ref, wqk_ref, wvt_ref, wm_ref, wff_ref, bf_ref,
                 qk_ref, vt_ref, mxz_ref, cf_ref, carry_ref, *, tiles_per_seq):
    i = pl.program_id(0)
    xn = _rms_norm(x_ref[...], nw_ref[...]).astype(BF16)
    for grp in range(4):
        cols = slice(grp * ATTN_WIDTH, (grp + 1) * ATTN_WIDTH)
        r = jnp.dot(xn, wqk_ref[:, cols], preferred_element_type=F32)
        if grp % 2 == 0:
            r = r * Q_SCALE
        qk_ref[:, cols] = r.astype(BF16)
    for grp in range(2):
        rows = slice(grp * ATTN_WIDTH, (grp + 1) * ATTN_WIDTH)
        vt_ref[rows, :] = lax.dot_general(wvt_ref[rows, :], xn, NT_DIMS,
                                          preferred_element_type=F32).astype(BF16)
    for grp in range(2):
        cols = slice(grp * MLSTM_WIDTH, (grp + 1) * MLSTM_WIDTH)
        mxz_ref[:, cols] = jnp.dot(xn, wm_ref[:, cols], preferred_element_type=F32)

    ff = lax.dot_general(wff_ref[...], xn, NT_DIMS, preferred_element_type=F32)
    csum = _lane_cumsum(_log_sigmoid(ff + bf_ref[...]))

    @pl.when(i % tiles_per_seq == 0)
    def _():
        carry_ref[...] = jnp.zeros_like(carry_ref)

    csum = csum + carry_ref[:, 0:1]
    carry_ref[...] = jnp.broadcast_to(csum[:, -1:], carry_ref.shape)
    cf_ref[...] = -csum


def _proj(x, norm_w, w_qk, w_vt, w_m, w_ff_t, b_f, *, seq, tm=512):
    m, d = x.shape
    nh = w_ff_t.shape[0]
    const = lambda i: (0, 0)
    return pl.pallas_call(
        functools.partial(_proj_kernel, tiles_per_seq=seq // tm),
        grid=(m // tm,),
        in_specs=[
            pl.BlockSpec((tm, d), lambda i: (i, 0)),
            pl.BlockSpec((1, d), const),
            pl.BlockSpec(w_qk.shape, const),
            pl.BlockSpec(w_vt.shape, const),
            pl.BlockSpec(w_m.shape, const),
            pl.BlockSpec(w_ff_t.shape, const),
            pl.BlockSpec((nh, 1), const),
        ],
        out_specs=[
            pl.BlockSpec((tm, w_qk.shape[1]), lambda i: (i, 0)),
            pl.BlockSpec((w_vt.shape[0], tm), lambda i: (0, i)),
            pl.BlockSpec((tm, w_m.shape[1]), lambda i: (i, 0)),
            pl.BlockSpec((nh, tm), lambda i: (0, i)),
        ],
        out_shape=[
            jax.ShapeDtypeStruct((m, w_qk.shape[1]), BF16),
            jax.ShapeDtypeStruct((w_vt.shape[0], m), BF16),
            jax.ShapeDtypeStruct((m, w_m.shape[1]), F32),
            jax.ShapeDtypeStruct((nh, m), F32),
        ],
        scratch_shapes=[pltpu.VMEM((nh, LANES), F32)],
        compiler_params=_params("arbitrary"),
        name="proj",
    )(x, norm_w, w_qk, w_vt, w_m, w_ff_t, b_f)


def _flash_pair(q_ref, k_ref, vt_ref, bias_ref, kaug_ref, vaug_ref, qm_ref, s_ref, acc_ref, finish, *, tq, tk, v_rows):
    seq = k_ref.shape[1]
    dr = v_rows[0].stop - v_rows[0].start
    half = LANES // 2

    kaug_ref[:, 0:LANES] = k_ref[0]
    lane = lax.broadcasted_iota(jnp.int32, (LANES, LANES), 1)
    for blk in range(seq // LANES):
        sl = slice(blk * LANES, (blk + 1) * LANES)
        ext = jnp.zeros((LANES, LANES), F32)
        for c in range(2):
            rem = jnp.broadcast_to(bias_ref[0, c:c + 1, sl] * LOG2E, (LANES, LANES)).T
            for term in range(BIAS_TERMS):
                part = rem.astype(BF16).astype(F32)
                ext = jnp.where(lane == BIAS_TERMS * c + term, part, ext)
                rem = rem - part
        kaug_ref[sl, LANES:] = ext.astype(BF16)
    for c in range(2):
        vaug_ref[c, 0:dr, :] = vt_ref[v_rows[c], :]
        vaug_ref[c, dr:, :] = jnp.ones((BF16_ROWS, seq), BF16)

    qlane = lax.broadcasted_iota(jnp.int32, (seq, LANES), 1)
    q = q_ref[0]
    for c in range(2):
        qm_ref[c, :, 0:LANES] = jnp.where((qlane >= c * half) & (qlane < (c + 1) * half), q, jnp.zeros_like(q))
        picks_bias = (qlane >= BIAS_TERMS * c) & (qlane < BIAS_TERMS * (c + 1))
        qm_ref[c, :, LANES:] = jnp.where(picks_bias, 1.0, 0.0).astype(BF16)

    work = []
    for i in range(seq // tq):
        n_full = (i * tq) // tk
        lead = i * tq - n_full * tk
        work += [(i, j * tk, tk, None, j == 0, False) for j in range(n_full)]
        work.append((i, n_full * tk, lead + tq, lead, n_full == 0, True))

    def scores(n):
        i, k0, size, _, _, _ = work[n]
        for c in range(2):
            s_ref[n % s_ref.shape[0], c, 0:size, :] = lax.dot_general(
                kaug_ref[k0:k0 + size, :], qm_ref[c, i * tq:(i + 1) * tq, :], NT_DIMS,
                preferred_element_type=F32)

    scores(0)
    scores(1)
    m = [None, None]
    for n, (i, k0, size, lead, first, last) in enumerate(work):
        if n + 2 < len(work):
            scores(n + 2)
        for c in range(2):
            s = s_ref[n % s_ref.shape[0], c, 0:size, :]
            if lead is not None:
                key = lax.broadcasted_iota(jnp.int32, (size, tq), 0)
                qry = lax.broadcasted_iota(jnp.int32, (size, tq), 1)
                s = jnp.where(key <= qry + lead, s, -jnp.inf)
            m_blk = jnp.max(s, axis=0, keepdims=True)
            m_new = m_blk if first else jnp.maximum(m[c], m_blk)
            p = jnp.exp2(s - m_new).astype(BF16)
            pv = jnp.dot(vaug_ref[c, :, k0:k0 + size], p, preferred_element_type=F32)
            if first:
                acc_ref[i, c] = pv
            else:
                acc_ref[i, c] = jnp.exp2(m[c] - m_new) * acc_ref[i, c] + pv
            m[c] = m_new
        if last:
            finish(i, *[acc_ref[i, c, 0:dr, :] * (1.0 / acc_ref[i, c, dr:dr + 1, :]) for c in range(2)])


def _diff_attn_kernel(q_ref, k_ref, vt_ref, bias_ref, lq1_ref, lk1_ref, lq2_ref, lk2_ref, sub_ref,
                      o_ref, *scratch, tq, tk, lam_init):
    lam = (jnp.exp(jnp.sum(lq1_ref[...] * lk1_ref[...], axis=1, keepdims=True))
           - jnp.exp(jnp.sum(lq2_ref[...] * lk2_ref[...], axis=1, keepdims=True)) + lam_init)

    def finish(i, a1, a2):
        d = (a1 - lam * a2).T
        o_ref[0, i * tq:(i + 1) * tq, :] = (_rms_norm(d, sub_ref[...]) * (1.0 - lam_init)).astype(o_ref.dtype)

    full = slice(0, LANES)
    _flash_pair(q_ref, k_ref, vt_ref, bias_ref, *scratch, finish, tq=tq, tk=tk, v_rows=(full, full))


def _fox_attn_kernel(q_ref, k_ref, vt_ref, bias_ref, o_ref, *scratch, tq, tk):
    def finish(i, a_even, a_odd):
        o_ref[0, i * tq:(i + 1) * tq, :] = jnp.concatenate([a_even, a_odd], axis=0).T.astype(o_ref.dtype)

    half = LANES // 2
    _flash_pair(q_ref, k_ref, vt_ref, bias_ref, *scratch, finish, tq=tq, tk=tk,
                v_rows=(slice(0, half), slice(half, LANES)))


def _attention(kernel, qk, vt, bias, extra, *, q_col, v_row, v_rows, bias_per_batch, tq=256, tk=512, name):
    b, s, _ = qk.shape
    groups = ATTN_WIDTH // LANES
    if bias_per_batch:
        bias_map = lambda bi, g: (g, 0, bi)
    else:
        bias_map = lambda bi, g: (g, 0, 0)
    in_specs = [
        pl.BlockSpec((1, s, LANES), lambda bi, g: (bi, 0, q_col + g)),
        pl.BlockSpec((1, s, LANES), lambda bi, g: (bi, 0, q_col + groups + g)),
        pl.BlockSpec((LANES, s), lambda bi, g: (v_row + g, bi)),
        pl.BlockSpec((1, 2, s), bias_map),
    ] + [pl.BlockSpec(e.shape, lambda bi, g: (0, 0)) for e in extra]
    rows = v_rows + BF16_ROWS
    return pl.pallas_call(
        functools.partial(kernel, tq=tq, tk=tk),
        grid=(b, groups),
        in_specs=in_specs,
        out_specs=pl.BlockSpec((1, s, LANES), lambda bi, g: (bi, 0, g)),
        out_shape=jax.ShapeDtypeStruct((b, s, ATTN_WIDTH), BF16),
        scratch_shapes=[pltpu.VMEM((s, 2 * LANES), BF16), pltpu.VMEM((2, rows, s), BF16),
                        pltpu.VMEM((2, s, 2 * LANES), BF16), pltpu.VMEM((SCORE_SLOTS, 2, tk, tq), F32),
                        pltpu.VMEM((s // tq, 2, rows, tq), F32)],
        compiler_params=_params("parallel", "parallel"),
        name=name,
    )(qk, qk, vt, bias, *extra)


def _mlstm_kernel(cur_ref, prev_ref, z_ref, cw_ref, cb_ref, wqt_ref, wk_ref, wvt_ref, wif_ref, bif_ref,
                  nrm_ref, skip_ref, o_ref, xbuf_ref, ct_ref, n_ref, m_ref):
    ci = pl.program_id(1)

    @pl.when(ci == 0)
    def _():
        ct_ref[...] = jnp.zeros_like(ct_ref)
        n_ref[...] = jnp.zeros_like(n_ref)
        m_ref[...] = jnp.zeros_like(m_ref)

    for r in range(cur_ref.shape[0]):
        one = pl.ds(r, 1)
        _mlstm_chunk(ci, cur_ref.at[one], prev_ref.at[one], z_ref.at[one], cw_ref, cb_ref, wqt_ref, wk_ref,
                     wvt_ref, wif_ref, bif_ref, nrm_ref, skip_ref, o_ref.at[one], xbuf_ref.at[r],
                     ct_ref.at[r], n_ref.at[r], m_ref.at[r])


def _mlstm_chunk(ci, cur_ref, prev_ref, z_ref, cw_ref, cb_ref, wqt_ref, wk_ref, wvt_ref, wif_ref, bif_ref,
                 nrm_ref, skip_ref, o_ref, xbuf_ref, ct_ref, n_ref, m_ref):
    L = MLSTM_CHUNK
    dh = MLSTM_HEAD_DIM
    nh = MLSTM_HEADS

    cur = cur_ref[0]
    xbuf_ref[0:SUBLANES, :] = jnp.where(ci > 0, prev_ref[0], 0.0)
    xbuf_ref[SUBLANES:, :] = cur
    conv = cb_ref[...]
    for j in range(MLSTM_CONV):
        off = SUBLANES - (MLSTM_CONV - 1) + j
        conv = conv + cw_ref[j:j + 1, :] * xbuf_ref[off:off + L, :]
    xc = _silu(conv)

    qts, kss, vts = [], [], []
    ift = bif_ref[...]
    for h in range(nh):
        sl = slice(h * dh, (h + 1) * dh)
        xh = xc[:, sl].astype(BF16)
        vh = cur[:, sl].astype(BF16)
        qt = lax.dot_general(wqt_ref[h], xh, NT_DIMS, preferred_element_type=F32).astype(BF16)
        k = jnp.dot(xh, wk_ref[h], preferred_element_type=F32)
        vt = lax.dot_general(wvt_ref[h], vh, NT_DIMS, preferred_element_type=F32).astype(BF16)
        ift = ift + jnp.dot(wif_ref[:, sl], qt, preferred_element_type=F32)
        ift = ift + lax.dot_general(wif_ref[:, MLSTM_WIDTH + h * dh:MLSTM_WIDTH + (h + 1) * dh],
                                    k.astype(BF16), NT_DIMS, preferred_element_type=F32)
        ift = ift + jnp.dot(wif_ref[:, 2 * MLSTM_WIDTH + h * dh:2 * MLSTM_WIDTH + (h + 1) * dh], vt,
                            preferred_element_type=F32)
        qts.append(qt)
        kss.append((k * (dh ** -0.5)).astype(BF16))
        vts.append(vt)

    key = lax.broadcasted_iota(jnp.int32, (L, L), 0)
    qry = lax.broadcasted_iota(jnp.int32, (L, L), 1)
    causal = key <= qry
    li_rows = ift[0:nh]
    lf = _log_sigmoid(ift)
    tri = jnp.where(causal, 1.0, 0.0).astype(BF16)
    b_all = jnp.zeros_like(lf)
    rem = lf
    for _ in range(3):
        part = rem.astype(BF16)
        b_all = b_all + jnp.dot(part, tri, preferred_element_type=F32)
        rem = rem - part.astype(F32)
    b_rows = b_all[nh:2 * nh]

    for h in range(nh):
        sl = slice(h * dh, (h + 1) * dh)
        li = li_rows[h:h + 1]
        b = b_rows[h:h + 1]
        a = li - b
        a_keys = jnp.broadcast_to(a, (L, L)).T
        b_last = b[:, L - 1:L]
        m_prev = m_ref[h:h + 1, 0:1]
        qt, ks, vt = qts[h], kss[h], vts[h]
        ct_state = ct_ref[h]
        n_state = n_ref[h]

        d = jnp.where(causal, b + a_keys, -jnp.inf)
        inter = b + m_prev
        m_t = jnp.maximum(inter, jnp.max(d, axis=0, keepdims=True))
        w_inter = jnp.exp(inter - m_t)
        s = jnp.dot(ks, qt, preferred_element_type=F32) * jnp.exp(d - m_t)
        num = (w_inter * jnp.dot(ct_state.astype(BF16), qt, preferred_element_type=F32)
               + jnp.dot(vt, s.astype(BF16), preferred_element_type=F32))
        den = (w_inter * jnp.dot(n_state.astype(BF16), qt, preferred_element_type=F32)[0:1]
               + jnp.sum(s, axis=0, keepdims=True))
        hh = num * (1.0 / jnp.maximum(jnp.abs(den), jnp.exp(-m_t)))

        g = b_last + a
        m_new = jnp.maximum(b_last + m_prev, jnp.max(g, axis=1, keepdims=True))
        wk = jnp.exp(g - m_new)
        dec = jnp.exp(b_last + m_prev - m_new)
        ct_ref[h] = dec * ct_state + jnp.dot((vt.astype(F32) * wk).astype(BF16), ks,
                                             preferred_element_type=F32)
        n_ref[h] = dec * n_state + jnp.dot(jnp.broadcast_to(wk, (SUBLANES, L)).astype(BF16), ks,
                                           preferred_element_type=F32)
        m_ref[h:h + 1, :] = jnp.broadcast_to(m_new, (1, LANES))

        mu = jnp.mean(hh, axis=0, keepdims=True)
        cen = hh - mu
        var = jnp.mean(cen * cen, axis=0, keepdims=True)
        hn = (cen * lax.rsqrt(var + LN_EPS)).T * nrm_ref[:, sl]
        o_ref[0, :, sl] = ((hn + skip_ref[:, sl] * xc[:, sl]) * _silu(z_ref[0, :, sl])).astype(o_ref.dtype)


def _mlstm(mxz, conv_w, conv_b, wq, wk, wv, w_if_t, b_if, norm_w, skip_w, *, nb=2):
    b, s, _ = mxz.shape
    L = MLSTM_CHUNK
    w = MLSTM_WIDTH
    rows_per_chunk = L // SUBLANES
    const2 = lambda bi, c: (0, 0)
    const3 = lambda bi, c: (0, 0, 0)
    return pl.pallas_call(
        _mlstm_kernel,
        grid=(b // nb, s // L),
        in_specs=[
            pl.BlockSpec((nb, L, w), lambda bi, c: (bi, c, 0)),
            pl.BlockSpec((nb, SUBLANES, w), lambda bi, c: (bi, jnp.maximum(c * rows_per_chunk - 1, 0), 0)),
            pl.BlockSpec((nb, L, w), lambda bi, c: (bi, c, 1)),
            pl.BlockSpec(conv_w.shape, const2),
            pl.BlockSpec(conv_b.shape, const2),
            pl.BlockSpec(wq.shape, const3),
            pl.BlockSpec(wk.shape, const3),
            pl.BlockSpec(wv.shape, const3),
            pl.BlockSpec(w_if_t.shape, const2),
            pl.BlockSpec(b_if.shape, const2),
            pl.BlockSpec(norm_w.shape, const2),
            pl.BlockSpec(skip_w.shape, const2),
        ],
        out_specs=pl.BlockSpec((nb, L, w), lambda bi, c: (bi, c, 0)),
        out_shape=jax.ShapeDtypeStruct((b, s, w), BF16),
        scratch_shapes=[
            pltpu.VMEM((nb, SUBLANES + L, w), F32),
            pltpu.VMEM((nb, MLSTM_HEADS, MLSTM_HEAD_DIM, MLSTM_HEAD_DIM), F32),
            pltpu.VMEM((nb, MLSTM_HEADS, SUBLANES, MLSTM_HEAD_DIM), F32),
            pltpu.VMEM((nb, SUBLANES, LANES), F32),
        ],
        compiler_params=_params("parallel", "arbitrary"),
        name="mlstm",
    )(mxz, mxz, mxz, conv_w, conv_b, wq, wk, wv, w_if_t, b_if, norm_w, skip_w)


def _merge_kernel(x_ref, nw_ref, wg_ref, gb_ref, od_ref, of_ref, om_ref, wbd_ref, wbf_ref, wbm_ref,
                  wo_ref, o_ref):
    x = x_ref[...]
    d = x.shape[1]
    xn = _rms_norm(x, nw_ref[...]).astype(BF16)
    merged = None
    for br, (b_ref, w_ref) in enumerate(((od_ref, wbd_ref), (of_ref, wbf_ref), (om_ref, wbm_ref))):
        cols = slice(br * d, (br + 1) * d)
        gate = jax.nn.sigmoid(jnp.dot(xn, wg_ref[:, cols], preferred_element_type=F32) + gb_ref[:, cols])
        term = gate * jnp.dot(b_ref[...], w_ref[...], preferred_element_type=F32)
        merged = term if merged is None else merged + term
    o_ref[...] = x + jnp.dot(merged.astype(BF16), wo_ref[...], preferred_element_type=F32)


def _merge(x, norm_w, w_g, gate_b, o_diff, o_fox, o_mlstm, wbd, wbf, wbm, w_out, *, tm=512):
    m, d = x.shape
    const = lambda i: (0, 0)
    row = lambda i: (i, 0)
    return pl.pallas_call(
        _merge_kernel,
        grid=(m // tm,),
        in_specs=[
            pl.BlockSpec((tm, d), row),
            pl.BlockSpec((1, d), const),
            pl.BlockSpec(w_g.shape, const),
            pl.BlockSpec(gate_b.shape, const),
            pl.BlockSpec((tm, ATTN_WIDTH), row),
            pl.BlockSpec((tm, ATTN_WIDTH), row),
            pl.BlockSpec((tm, MLSTM_WIDTH), row),
            pl.BlockSpec(wbd.shape, const),
            pl.BlockSpec(wbf.shape, const),
            pl.BlockSpec(wbm.shape, const),
            pl.BlockSpec(w_out.shape, const),
        ],
        out_specs=pl.BlockSpec((tm, d), row),
        out_shape=jax.ShapeDtypeStruct((m, d), F32),
        compiler_params=_params("parallel"),
        name="merge",
    )(x, norm_w, w_g, gate_b, o_diff, o_fox, o_mlstm, wbd, wbf, wbm, w_out)


def _alibi_bias(seq):
    slopes = np.array([2.0 ** (-8.0 * (h + 1) / DIFF_HEADS) for h in range(DIFF_HEADS)], dtype=np.float32)
    pos = np.arange(seq, dtype=np.float32)
    per_head = slopes[:, None] * pos[None, :]
    return jnp.asarray(np.repeat(per_head[:, None, :], 2, axis=1))


def kernel(x, ffn1_norm, ffn1_w_gate, ffn1_w_up, ffn1_w_down, mix_norm, w_in, gate_bias, diff_lq1, diff_lk1, diff_lq2, diff_lk2, diff_subln, fox_b_f, mlstm_conv_w, mlstm_conv_b, mlstm_wq, mlstm_wk, mlstm_wv, mlstm_w_if, mlstm_b_if, mlstm_norm, mlstm_skip, w_branch_diff, w_branch_fox, w_branch_mlstm, w_out, ffn2_norm, ffn2_w_gate, ffn2_w_up, ffn2_w_down, final_norm):
    batch, seq, d = x.shape
    depth = w_in.shape[0]
    m = batch * seq
    bf = lambda a: a.astype(BF16)
    row = lambda a: a.reshape(1, -1)
    aw = ATTN_WIDTH
    ff0 = 6 * aw
    mx0 = ff0 + FOX_HEADS
    g0 = mx0 + 2 * MLSTM_WIDTH
    blocks = aw // LANES
    alibi = _alibi_bias(seq)

    x = x.reshape(m, d)
    for l in range(depth):
        x = _ffn(x, row(ffn1_norm[l]), ffn1_w_gate, ffn1_w_up, ffn1_w_down, l)

        w_in_l = w_in[l]
        w_qk = bf(jnp.concatenate([w_in_l[:, 0:2 * aw], w_in_l[:, 3 * aw:5 * aw]], axis=1))
        w_vt = bf(jnp.concatenate([w_in_l[:, 2 * aw:3 * aw], w_in_l[:, 5 * aw:6 * aw]], axis=1).T)
        qk, vt, mxz, cf = _proj(x, row(mix_norm[l]), w_qk, w_vt, bf(w_in_l[:, mx0:g0]),
                                bf(w_in_l[:, ff0:mx0].T), fox_b_f[l].reshape(-1, 1), seq=seq)
        qk = qk.reshape(batch, seq, 4 * aw)
        lam_init = 0.8 - 0.6 * math.exp(-0.3 * l)
        o_diff = _attention(
            functools.partial(_diff_attn_kernel, lam_init=lam_init), qk, vt, alibi,
            [row(diff_lq1[l]), row(diff_lk1[l]), row(diff_lq2[l]), row(diff_lk2[l]), row(diff_subln[l])],
            q_col=0, v_row=0, v_rows=LANES, bias_per_batch=False, name="diff_attn")
        o_fox = _attention(_fox_attn_kernel, qk, vt, cf.reshape(FOX_HEADS // 2, 2, m), [],
                           q_col=2 * blocks, v_row=blocks, v_rows=LANES // 2, bias_per_batch=True,
                           name="fox_attn")
        o_mlstm = _mlstm(mxz.reshape(batch, seq, 2 * MLSTM_WIDTH), mlstm_conv_w[l], row(mlstm_conv_b[l]),
                         bf(jnp.swapaxes(mlstm_wq[l], 1, 2)), bf(mlstm_wk[l]),
                         bf(jnp.swapaxes(mlstm_wv[l], 1, 2)), bf(mlstm_w_if[l].T),
                         mlstm_b_if[l].reshape(-1, 1), row(mlstm_norm[l]), row(mlstm_skip[l]))
        x = _merge(x, row(mix_norm[l]), bf(w_in_l[:, g0:]), gate_bias[l].reshape(1, -1),
                   o_diff.reshape(m, aw), o_fox.reshape(m, aw),
                   o_mlstm.reshape(m, MLSTM_WIDTH), bf(w_branch_diff[l]), bf(w_branch_fox[l]),
                   bf(w_branch_mlstm[l]), bf(w_out[l]))

        last = l == depth - 1
        x = _ffn(x, row(ffn2_norm[l]), ffn2_w_gate, ffn2_w_up, ffn2_w_down, l,
                 row(final_norm) if last else None)
    return x.reshape(batch, seq, d)
```

```python
import functools
import math

import numpy as np
import jax
import jax.numpy as jnp
from jax import lax
from jax.experimental import pallas as pl
from jax.experimental.pallas import tpu as pltpu

D_MODEL = 1024
D_FF = 2816
RMS_EPS = 1e-6
LN_EPS = 1e-5
N_BRANCH = 3
DIFF_HEADS = 4
ATTN_HEAD_DIM = 64
FOX_HEADS = 8
MLSTM_HEADS = 4
MLSTM_HEAD_DIM = 128
MLSTM_WIDTH = MLSTM_HEADS * MLSTM_HEAD_DIM
MLSTM_CONV = 4
MLSTM_CHUNK = 128
MLSTM_STAGE_AHEAD = 2
ATTN_WIDTH = 512
LANES = 128
SUBLANES = 8
BF16_ROWS = 16
SCORE_SLOTS = 4
BIAS_TERMS = 3
VMEM_LIMIT_BYTES = 56 * 1024 * 1024
LOG2E = math.log2(math.e)
Q_SCALE = ATTN_HEAD_DIM ** -0.5 * LOG2E

F32 = jnp.float32
BF16 = jnp.bfloat16
NT_DIMS = (((1,), (1,)), ((), ()))
TN_DIMS = (((0,), (0,)), ((), ()))


def _params(*semantics):
    return pltpu.CompilerParams(dimension_semantics=semantics, vmem_limit_bytes=VMEM_LIMIT_BYTES)


def _rms_norm(x, w):
    return x * lax.rsqrt(jnp.mean(x * x, axis=-1, keepdims=True) + RMS_EPS) * w


def _silu(x):
    return x * jax.nn.sigmoid(x)


def _log_sigmoid(x):
    return jnp.minimum(x, 0.0) - jnp.log1p(jnp.exp(-jnp.abs(x)))


def _lane_cumsum(x):
    n = x.shape[-1]
    lane = lax.broadcasted_iota(jnp.int32, x.shape, 1)
    shift = 1
    while shift < n:
        x = x + jnp.where(lane >= shift, pltpu.roll(x, shift, axis=1), 0.0)
        shift *= 2
    return x


def _ffn_kernel(*refs, final):
    if final:
        x_ref, nw_ref, wg_ref, wu_ref, wd_ref, fn_ref, o_ref, xn_ref = refs
    else:
        x_ref, nw_ref, wg_ref, wu_ref, wd_ref, o_ref, xn_ref = refs
    f = pl.program_id(1)

    @pl.when(f == 0)
    def _():
        xn_ref[...] = _rms_norm(x_ref[...], nw_ref[...]).astype(BF16)
        o_ref[...] = jnp.zeros_like(o_ref)

    xn = xn_ref[...]
    g = jnp.dot(xn, wg_ref[...].astype(BF16), preferred_element_type=F32)
    u = jnp.dot(xn, wu_ref[...].astype(BF16), preferred_element_type=F32)
    h = (_silu(g) * u).astype(BF16)
    o_ref[...] += jnp.dot(h, wd_ref[...].astype(BF16), preferred_element_type=F32)

    @pl.when(f == pl.num_programs(1) - 1)
    def _():
        y = x_ref[...] + 0.5 * o_ref[...]
        if final:
            y = _rms_norm(y, fn_ref[...])
        o_ref[...] = y


def _ffn(x, norm_w, w_gate, w_up, w_down, layer, final_norm=None, *, tm=2048, tf=256):
    m, d = x.shape
    dff = w_gate.shape[2]
    final = final_norm is not None
    in_specs = [
        pl.BlockSpec((tm, d), lambda i, f: (i, 0)),
        pl.BlockSpec((1, d), lambda i, f: (0, 0)),
        pl.BlockSpec((None, d, tf), lambda i, f: (layer, 0, f)),
        pl.BlockSpec((None, d, tf), lambda i, f: (layer, 0, f)),
        pl.BlockSpec((None, tf, d), lambda i, f: (layer, f, 0)),
    ]
    args = [x, norm_w, w_gate, w_up, w_down]
    if final:
        in_specs.append(pl.BlockSpec((1, d), lambda i, f: (0, 0)))
        args.append(final_norm)
    return pl.pallas_call(
        functools.partial(_ffn_kernel, final=final),
        grid=(m // tm, dff // tf),
        in_specs=in_specs,
        out_specs=pl.BlockSpec((tm, d), lambda i, f: (i, 0)),
        out_shape=jax.ShapeDtypeStruct((m, d), F32),
        scratch_shapes=[pltpu.VMEM((tm, d), BF16)],
        compiler_params=_params("parallel", "arbitrary"),
        name="ffn_final" if final else "ffn",
    )(*args)


def _proj_kernel(x_ref, nw_ref, wqk_ref, wvt_ref, wm_ref, wff_ref, bf_ref,
                 qk_ref, vt_ref, mxz_ref, cf_ref, carry_ref, *, tiles_per_seq):
    i = pl.program_id(0)
    xn = _rms_norm(x_ref[...], nw_ref[...]).astype(BF16)
    for grp in range(4):
        cols = slice(grp * ATTN_WIDTH, (grp + 1) * ATTN_WIDTH)
        r = jnp.dot(xn, wqk_ref[:, cols], preferred_element_type=F32)
        if grp % 2 == 0:
            r = r * Q_SCALE
        qk_ref[:, cols] = r.astype(BF16)
    for grp in range(2):
        rows = slice(grp * ATTN_WIDTH, (grp + 1) * ATTN_WIDTH)
        vt_ref[rows, :] = lax.dot_general(wvt_ref[rows, :], xn, NT_DIMS,
                                          preferred_element_type=F32).astype(BF16)
    for grp in range(2):
        cols = slice(grp * MLSTM_WIDTH, (grp + 1) * MLSTM_WIDTH)
        mxz_ref[:, cols] = jnp.dot(xn, wm_ref[:, cols], preferred_element_type=F32)

    ff = lax.dot_general(wff_ref[...], xn, NT_DIMS, preferred_element_type=F32)
    csum = _lane_cumsum(_log_sigmoid(ff + bf_ref[...]))

    @pl.when(i % tiles_per_seq == 0)
    def _():
        carry_ref[...] = jnp.zeros_like(carry_ref)

    csum = csum + carry_ref[:, 0:1]
    carry_ref[...] = jnp.broadcast_to(csum[:, -1:], carry_ref.shape)
    cf_ref[...] = -csum


def _proj(x, norm_w, w_qk, w_vt, w_m, w_ff_t, b_f, *, seq, tm=512):
    m, d = x.shape
    nh = w_ff_t.shape[0]
    const = lambda i: (0, 0)
    return pl.pallas_call(
        functools.partial(_proj_kernel, tiles_per_seq=seq // tm),
        grid=(m // tm,),
        in_specs=[
            pl.BlockSpec((tm, d), lambda i: (i, 0)),
            pl.BlockSpec((1, d), const),
            pl.BlockSpec(w_qk.shape, const),
            pl.BlockSpec(w_vt.shape, const),
            pl.BlockSpec(w_m.shape, const),
            pl.BlockSpec(w_ff_t.shape, const),
            pl.BlockSpec((nh, 1), const),
        ],
        out_specs=[
            pl.BlockSpec((tm, w_qk.shape[1]), lambda i: (i, 0)),
            pl.BlockSpec((w_vt.shape[0], tm), lambda i: (0, i)),
            pl.BlockSpec((tm, w_m.shape[1]), lambda i: (i, 0)),
            pl.BlockSpec((nh, tm), lambda i: (0, i)),
        ],
        out_shape=[
            jax.ShapeDtypeStruct((m, w_qk.shape[1]), BF16),
            jax.ShapeDtypeStruct((w_vt.shape[0], m), BF16),
            jax.ShapeDtypeStruct((m, w_m.shape[1]), F32),
            jax.ShapeDtypeStruct((nh, m), F32),
        ],
        scratch_shapes=[pltpu.VMEM((nh, LANES), F32)],
        compiler_params=_params("arbitrary"),
        name="proj",
    )(x, norm_w, w_qk, w_vt, w_m, w_ff_t, b_f)


def _flash_pair(q_ref, k_ref, vt_ref, bias_ref, kaug_ref, vaug_ref, qm_ref, s_ref, acc_ref, finish, *, tq, tk, v_rows):
    seq = k_ref.shape[1]
    dr = v_rows[0].stop - v_rows[0].start
    half = LANES // 2

    kaug_ref[:, 0:LANES] = k_ref[0]
    lane = lax.broadcasted_iota(jnp.int32, (LANES, LANES), 1)
    for blk in range(seq // LANES):
        sl = slice(blk * LANES, (blk + 1) * LANES)
        ext = jnp.zeros((LANES, LANES), F32)
        for c in range(2):
            rem = jnp.broadcast_to(bias_ref[0, c:c + 1, sl] * LOG2E, (LANES, LANES)).T
            for term in range(BIAS_TERMS):
                part = rem.astype(BF16).astype(F32)
                ext = jnp.where(lane == BIAS_TERMS * c + term, part, ext)
                rem = rem - part
        kaug_ref[sl, LANES:] = ext.astype(BF16)
    for c in range(2):
        vaug_ref[c, 0:dr, :] = vt_ref[v_rows[c], :]
        vaug_ref[c, dr:, :] = jnp.ones((BF16_ROWS, seq), BF16)

    qlane = lax.broadcasted_iota(jnp.int32, (seq, LANES), 1)
    q = q_ref[0]
    for c in range(2):
        qm_ref[c, :, 0:LANES] = jnp.where((qlane >= c * half) & (qlane < (c + 1) * half), q, jnp.zeros_like(q))
        picks_bias = (qlane >= BIAS_TERMS * c) & (qlane < BIAS_TERMS * (c + 1))
        qm_ref[c, :, LANES:] = jnp.where(picks_bias, 1.0, 0.0).astype(BF16)

    work = []
    for i in range(seq // tq):
        n_full = (i * tq) // tk
        lead = i * tq - n_full * tk
        work += [(i, j * tk, tk, None, j == 0, False) for j in range(n_full)]
        work.append((i, n_full * tk, lead + tq, lead, n_full == 0, True))

    def scores(n):
        i, k0, size, _, _, _ = work[n]
        for c in range(2):
            s_ref[n % s_ref.shape[0], c, 0:size, :] = lax.dot_general(
                kaug_ref[k0:k0 + size, :], qm_ref[c, i * tq:(i + 1) * tq, :], NT_DIMS,
                preferred_element_type=F32)

    scores(0)
    scores(1)
    m = [None, None]
    for n, (i, k0, size, lead, first, last) in enumerate(work):
        if n + 2 < len(work):
            scores(n + 2)
        for c in range(2):
            s = s_ref[n % s_ref.shape[0], c, 0:size, :]
            if lead is not None:
                key = lax.broadcasted_iota(jnp.int32, (size, tq), 0)
                qry = lax.broadcasted_iota(jnp.int32, (size, tq), 1)
                s = jnp.where(key <= qry + lead, s, -jnp.inf)
            m_blk = jnp.max(s, axis=0, keepdims=True)
            m_new = m_blk if first else jnp.maximum(m[c], m_blk)
            p = jnp.exp2(s - m_new).astype(BF16)
            pv = jnp.dot(vaug_ref[c, :, k0:k0 + size], p, preferred_element_type=F32)
            if first:
                acc_ref[i, c] = pv
            else:
                acc_ref[i, c] = jnp.exp2(m[c] - m_new) * acc_ref[i, c] + pv
            m[c] = m_new
        if last:
            finish(i, *[acc_ref[i, c, 0:dr, :] * (1.0 / acc_ref[i, c, dr:dr + 1, :]) for c in range(2)])


def _diff_attn_kernel(q_ref, k_ref, vt_ref, bias_ref, lq1_ref, lk1_ref, lq2_ref, lk2_ref, sub_ref,
                      o_ref, *scratch, tq, tk, lam_init):
    lam = (jnp.exp(jnp.sum(lq1_ref[...] * lk1_ref[...], axis=1, keepdims=True))
           - jnp.exp(jnp.sum(lq2_ref[...] * lk2_ref[...], axis=1, keepdims=True)) + lam_init)

    def finish(i, a1, a2):
        d = (a1 - lam * a2).T
        o_ref[0, i * tq:(i + 1) * tq, :] = (_rms_norm(d, sub_ref[...]) * (1.0 - lam_init)).astype(o_ref.dtype)

    full = slice(0, LANES)
    _flash_pair(q_ref, k_ref, vt_ref, bias_ref, *scratch, finish, tq=tq, tk=tk, v_rows=(full, full))


def _fox_attn_kernel(q_ref, k_ref, vt_ref, bias_ref, o_ref, *scratch, tq, tk):
    def finish(i, a_even, a_odd):
        o_ref[0, i * tq:(i + 1) * tq, :] = jnp.concatenate([a_even, a_odd], axis=0).T.astype(o_ref.dtype)

    half = LANES // 2
    _flash_pair(q_ref, k_ref, vt_ref, bias_ref, *scratch, finish, tq=tq, tk=tk,
                v_rows=(slice(0, half), slice(half, LANES)))


def _attention(kernel, qk, vt, bias, extra, *, q_col, v_row, v_rows, bias_per_batch, tq=256, tk=512, name):
    b, s, _ = qk.shape
    groups = ATTN_WIDTH // LANES
    if bias_per_batch:
        bias_map = lambda bi, g: (g, 0, bi)
    else:
        bias_map = lambda bi, g: (g, 0, 0)
    in_specs = [
        pl.BlockSpec((1, s, LANES), lambda bi, g: (bi, 0, q_col + g)),
        pl.BlockSpec((1, s, LANES), lambda bi, g: (bi, 0, q_col + groups + g)),
        pl.BlockSpec((LANES, s), lambda bi, g: (v_row + g, bi)),
        pl.BlockSpec((1, 2, s), bias_map),
    ] + [pl.BlockSpec(e.shape, lambda bi, g: (0, 0)) for e in extra]
    rows = v_rows + BF16_ROWS
    return pl.pallas_call(
        functools.partial(kernel, tq=tq, tk=tk),
        grid=(b, groups),
        in_specs=in_specs,
        out_specs=pl.BlockSpec((1, s, LANES), lambda bi, g: (bi, 0, g)),
        out_shape=jax.ShapeDtypeStruct((b, s, ATTN_WIDTH), BF16),
        scratch_shapes=[pltpu.VMEM((s, 2 * LANES), BF16), pltpu.VMEM((2, rows, s), BF16),
                        pltpu.VMEM((2, s, 2 * LANES), BF16), pltpu.VMEM((SCORE_SLOTS, 2, tk, tq), F32),
                        pltpu.VMEM((s // tq, 2, rows, tq), F32)],
        compiler_params=_params("parallel", "parallel"),
        name=name,
    )(qk, qk, vt, bias, *extra)


def _mlstm_kernel(cur_ref, prev_ref, z_ref, cw_ref, cb_ref, wqt_ref, wk_ref, wvt_ref, wif_ref, bif_ref,
                  nrm_ref, skip_ref, o_ref, xbuf_ref, ct_ref, n_ref, m_ref):
    ci = pl.program_id(1)
    nb, span, _ = cur_ref.shape
    chunks = span // MLSTM_CHUNK

    @pl.when(ci == 0)
    def _():
        ct_ref[...] = jnp.zeros_like(ct_ref)
        n_ref[...] = jnp.zeros_like(n_ref)
        m_ref[...] = jnp.zeros_like(m_ref)

    for r in range(nb):
        xbuf_ref[r, 0:SUBLANES, :] = jnp.where(ci > 0, prev_ref[r], 0.0)
        xbuf_ref[r, SUBLANES:, :] = cur_ref[r]

    work = [(r, k) for k in range(chunks) for r in range(nb)]
    refs = (cw_ref, cb_ref, wqt_ref, wk_ref, wvt_ref, wif_ref, bif_ref)
    ahead = min(MLSTM_STAGE_AHEAD, len(work))
    staged = {n: _mlstm_gates(xbuf_ref.at[work[n][0]], work[n][1], *refs) for n in range(ahead)}
    for n, (r, k) in enumerate(work):
        if n + ahead < len(work):
            staged[n + ahead] = _mlstm_gates(xbuf_ref.at[work[n + ahead][0]], work[n + ahead][1], *refs)
        rows = slice(k * MLSTM_CHUNK, (k + 1) * MLSTM_CHUNK)
        _mlstm_recur(staged.pop(n), z_ref.at[r, rows], nrm_ref, skip_ref, o_ref.at[r, rows],
                     ct_ref.at[r], n_ref.at[r], m_ref.at[r])


def _mlstm_gates(xbuf_ref, k, cw_ref, cb_ref, wqt_ref, wk_ref, wvt_ref, wif_ref, bif_ref):
    L = MLSTM_CHUNK
    dh = MLSTM_HEAD_DIM
    nh = MLSTM_HEADS

    base = SUBLANES + k * L
    cur = xbuf_ref[base:base + L, :]
    conv = cb_ref[...]
    for j in range(MLSTM_CONV):
        off = base - (MLSTM_CONV - 1) + j
        conv = conv + cw_ref[j:j + 1, :] * xbuf_ref[off:off + L, :]
    xc = _silu(conv)

    qts, kss, vts = [], [], []
    ift = bif_ref[...]
    for h in range(nh):
        sl = slice(h * dh, (h + 1) * dh)
        xh = xc[:, sl].astype(BF16)
        vh = cur[:, sl].astype(BF16)
        qt = lax.dot_general(wqt_ref[h], xh, NT_DIMS, preferred_element_type=F32).astype(BF16)
        kk = jnp.dot(xh, wk_ref[h], preferred_element_type=F32)
        vt = lax.dot_general(wvt_ref[h], vh, NT_DIMS, preferred_element_type=F32).astype(BF16)
        ift = ift + jnp.dot(wif_ref[:, sl], qt, preferred_element_type=F32)
        ift = ift + lax.dot_general(wif_ref[:, MLSTM_WIDTH + h * dh:MLSTM_WIDTH + (h + 1) * dh],
                                    kk.astype(BF16), NT_DIMS, preferred_element_type=F32)
        ift = ift + jnp.dot(wif_ref[:, 2 * MLSTM_WIDTH + h * dh:2 * MLSTM_WIDTH + (h + 1) * dh], vt,
                            preferred_element_type=F32)
        qts.append(qt)
        kss.append((kk * (dh ** -0.5)).astype(BF16))
        vts.append(vt)

    key = lax.broadcasted_iota(jnp.int32, (L, L), 0)
    qry = lax.broadcasted_iota(jnp.int32, (L, L), 1)
    lf = _log_sigmoid(ift)
    tri = jnp.where(key <= qry, 1.0, 0.0).astype(BF16)
    b_all = jnp.zeros_like(lf)
    rem = lf
    for _ in range(3):
        part = rem.astype(BF16)
        b_all = b_all + jnp.dot(part, tri, preferred_element_type=F32)
        rem = rem - part.astype(F32)
    return dict(xc=xc, qts=qts, kss=kss, vts=vts, li_rows=ift[0:nh], b_rows=b_all[nh:2 * nh])


def _mlstm_recur(st, z_ref, nrm_ref, skip_ref, o_ref, ct_ref, n_ref, m_ref):
    L = MLSTM_CHUNK
    dh = MLSTM_HEAD_DIM
    key = lax.broadcasted_iota(jnp.int32, (L, L), 0)
    qry = lax.broadcasted_iota(jnp.int32, (L, L), 1)
    causal = key <= qry
    xc = st["xc"]

    heads = range(MLSTM_HEADS)
    qts, kss, vts = st["qts"], st["kss"], st["vts"]
    ct_state = [ct_ref[h] for h in heads]
    n_state = [n_ref[h] for h in heads]
    m_prev = [m_ref[h:h + 1, 0:1] for h in heads]
    qk = [jnp.dot(kss[h], qts[h], preferred_element_type=F32) for h in heads]
    cq = [jnp.dot(ct_state[h].astype(BF16), qts[h], preferred_element_type=F32) for h in heads]
    nq = [jnp.dot(n_state[h].astype(BF16), qts[h], preferred_element_type=F32)[0:1] for h in heads]

    li = [st["li_rows"][h:h + 1] for h in heads]
    b = [st["b_rows"][h:h + 1] for h in heads]
    a = [li[h] - b[h] for h in heads]
    a_keys = [jnp.broadcast_to(a[h], (L, L)).T for h in heads]
    d = [jnp.where(causal, b[h] + a_keys[h], -jnp.inf) for h in heads]
    inter = [b[h] + m_prev[h] for h in heads]
    m_t = [jnp.maximum(inter[h], jnp.max(d[h], axis=0, keepdims=True)) for h in heads]
    w_inter = [jnp.exp(inter[h] - m_t[h]) for h in heads]
    s = [qk[h] * jnp.exp(d[h] - m_t[h]) for h in heads]
    sv = [jnp.dot(vts[h], s[h].astype(BF16), preferred_element_type=F32) for h in heads]

    b_last = [b[h][:, L - 1:L] for h in heads]
    g = [b_last[h] + a[h] for h in heads]
    m_new = [jnp.maximum(b_last[h] + m_prev[h], jnp.max(g[h], axis=1, keepdims=True)) for h in heads]
    wk = [jnp.exp(g[h] - m_new[h]) for h in heads]
    dec = [jnp.exp(b_last[h] + m_prev[h] - m_new[h]) for h in heads]
    for h in heads:
        ct_ref[h] = dec[h] * ct_state[h] + jnp.dot((vts[h].astype(F32) * wk[h]).astype(BF16), kss[h],
                                                   preferred_element_type=F32)
        n_ref[h] = dec[h] * n_state[h] + jnp.dot(jnp.broadcast_to(wk[h], (SUBLANES, L)).astype(BF16),
                                                 kss[h], preferred_element_type=F32)
        m_ref[h:h + 1, :] = jnp.broadcast_to(m_new[h], (1, LANES))

    for h in heads:
        sl = slice(h * dh, (h + 1) * dh)
        num = w_inter[h] * cq[h] + sv[h]
        den = w_inter[h] * nq[h] + jnp.sum(s[h], axis=0, keepdims=True)
        hh = num * (1.0 / jnp.maximum(jnp.abs(den), jnp.exp(-m_t[h])))
        mu = jnp.mean(hh, axis=0, keepdims=True)
        cen = hh - mu
        var = jnp.mean(cen * cen, axis=0, keepdims=True)
        hn = (cen * lax.rsqrt(var + LN_EPS)).T * nrm_ref[:, sl]
        o_ref[:, sl] = ((hn + skip_ref[:, sl] * xc[:, sl]) * _silu(z_ref[:, sl])).astype(o_ref.dtype)


def _mlstm(mxz, conv_w, conv_b, wq, wk, wv, w_if_t, b_if, norm_w, skip_w, *, nb=2, chunks=4):
    b, s, _ = mxz.shape
    span = chunks * MLSTM_CHUNK
    w = MLSTM_WIDTH
    rows_per_span = span // SUBLANES
    const2 = lambda bi, c: (0, 0)
    const3 = lambda bi, c: (0, 0, 0)
    return pl.pallas_call(
        _mlstm_kernel,
        grid=(b // nb, s // span),
        in_specs=[
            pl.BlockSpec((nb, span, w), lambda bi, c: (bi, c, 0)),
            pl.BlockSpec((nb, SUBLANES, w), lambda bi, c: (bi, jnp.maximum(c * rows_per_span - 1, 0), 0)),
            pl.BlockSpec((nb, span, w), lambda bi, c: (bi, c, 1)),
            pl.BlockSpec(conv_w.shape, const2),
            pl.BlockSpec(conv_b.shape, const2),
            pl.BlockSpec(wq.shape, const3),
            pl.BlockSpec(wk.shape, const3),
            pl.BlockSpec(wv.shape, const3),
            pl.BlockSpec(w_if_t.shape, const2),
            pl.BlockSpec(b_if.shape, const2),
            pl.BlockSpec(norm_w.shape, const2),
            pl.BlockSpec(skip_w.shape, const2),
        ],
        out_specs=pl.BlockSpec((nb, span, w), lambda bi, c: (bi, c, 0)),
        out_shape=jax.ShapeDtypeStruct((b, s, w), BF16),
        scratch_shapes=[
            pltpu.VMEM((nb, SUBLANES + span, w), F32),
            pltpu.VMEM((nb, MLSTM_HEADS, MLSTM_HEAD_DIM, MLSTM_HEAD_DIM), F32),
            pltpu.VMEM((nb, MLSTM_HEADS, SUBLANES, MLSTM_HEAD_DIM), F32),
            pltpu.VMEM((nb, SUBLANES, LANES), F32),
        ],
        compiler_params=_params("parallel", "arbitrary"),
        name="mlstm",
    )(mxz, mxz, mxz, conv_w, conv_b, wq, wk, wv, w_if_t, b_if, norm_w, skip_w)


def _merge_kernel(x_ref, nw_ref, wg_ref, gb_ref, od_ref, of_ref, om_ref, wbd_ref, wbf_ref, wbm_ref,
                  wo_ref, o_ref):
    x = x_ref[...]
    d = x.shape[1]
    xn = _rms_norm(x, nw_ref[...]).astype(BF16)
    merged = None
    for br, (b_ref, w_ref) in enumerate(((od_ref, wbd_ref), (of_ref, wbf_ref), (om_ref, wbm_ref))):
        cols = slice(br * d, (br + 1) * d)
        gate = jax.nn.sigmoid(jnp.dot(xn, wg_ref[:, cols], preferred_element_type=F32) + gb_ref[:, cols])
        term = gate * jnp.dot(b_ref[...], w_ref[...], preferred_element_type=F32)
        merged = term if merged is None else merged + term
    o_ref[...] = x + jnp.dot(merged.astype(BF16), wo_ref[...], preferred_element_type=F32)


def _merge(x, norm_w, w_g, gate_b, o_diff, o_fox, o_mlstm, wbd, wbf, wbm, w_out, *, tm=512):
    m, d = x.shape
    const = lambda i: (0, 0)
    row = lambda i: (i, 0)
    return pl.pallas_call(
        _merge_kernel,
        grid=(m // tm,),
        in_specs=[
            pl.BlockSpec((tm, d), row),
            pl.BlockSpec((1, d), const),
            pl.BlockSpec(w_g.shape, const),
            pl.BlockSpec(gate_b.shape, const),
            pl.BlockSpec((tm, ATTN_WIDTH), row),
            pl.BlockSpec((tm, ATTN_WIDTH), row),
            pl.BlockSpec((tm, MLSTM_WIDTH), row),
            pl.BlockSpec(wbd.shape, const),
            pl.BlockSpec(wbf.shape, const),
            pl.BlockSpec(wbm.shape, const),
            pl.BlockSpec(w_out.shape, const),
        ],
        out_specs=pl.BlockSpec((tm, d), row),
        out_shape=jax.ShapeDtypeStruct((m, d), F32),
        compiler_params=_params("parallel"),
        name="merge",
    )(x, norm_w, w_g, gate_b, o_diff, o_fox, o_mlstm, wbd, wbf, wbm, w_out)


def _alibi_bias(seq):
    slopes = np.array([2.0 ** (-8.0 * (h + 1) / DIFF_HEADS) for h in range(DIFF_HEADS)], dtype=np.float32)
    pos = np.arange(seq, dtype=np.float32)
    per_head = slopes[:, None] * pos[None, :]
    return jnp.asarray(np.repeat(per_head[:, None, :], 2, axis=1))


def kernel(x, ffn1_norm, ffn1_w_gate, ffn1_w_up, ffn1_w_down, mix_norm, w_in, gate_bias, diff_lq1, diff_lk1, diff_lq2, diff_lk2, diff_subln, fox_b_f, mlstm_conv_w, mlstm_conv_b, mlstm_wq, mlstm_wk, mlstm_wv, mlstm_w_if, mlstm_b_if, mlstm_norm, mlstm_skip, w_branch_diff, w_branch_fox, w_branch_mlstm, w_out, ffn2_norm, ffn2_w_gate, ffn2_w_up, ffn2_w_down, final_norm):
    batch, seq, d = x.shape
    depth = w_in.shape[0]
    m = batch * seq
    bf = lambda a: a.astype(BF16)
    row = lambda a: a.reshape(1, -1)
    aw = ATTN_WIDTH
    ff0 = 6 * aw
    mx0 = ff0 + FOX_HEADS
    g0 = mx0 + 2 * MLSTM_WIDTH
    blocks = aw // LANES
    alibi = _alibi_bias(seq)

    x = x.reshape(m, d)
    for l in range(depth):
        x = _ffn(x, row(ffn1_norm[l]), ffn1_w_gate, ffn1_w_up, ffn1_w_down, l)

        w_in_l = w_in[l]
        w_qk = bf(jnp.concatenate([w_in_l[:, 0:2 * aw], w_in_l[:, 3 * aw:5 * aw]], axis=1))
        w_vt = bf(jnp.concatenate([w_in_l[:, 2 * aw:3 * aw], w_in_l[:, 5 * aw:6 * aw]], axis=1).T)
        qk, vt, mxz, cf = _proj(x, row(mix_norm[l]), w_qk, w_vt, bf(w_in_l[:, mx0:g0]),
                                bf(w_in_l[:, ff0:mx0].T), fox_b_f[l].reshape(-1, 1), seq=seq)
        qk = qk.reshape(batch, seq, 4 * aw)
        lam_init = 0.8 - 0.6 * math.exp(-0.3 * l)
        o_diff = _attention(
            functools.partial(_diff_attn_kernel, lam_init=lam_init), qk, vt, alibi,
            [row(diff_lq1[l]), row(diff_lk1[l]), row(diff_lq2[l]), row(diff_lk2[l]), row(diff_subln[l])],
            q_col=0, v_row=0, v_rows=LANES, bias_per_batch=False, name="diff_attn")
        o_fox = _attention(_fox_attn_kernel, qk, vt, cf.reshape(FOX_HEADS // 2, 2, m), [],
                           q_col=2 * blocks, v_row=blocks, v_rows=LANES // 2, bias_per_batch=True,
                           name="fox_attn")
        o_mlstm = _mlstm(mxz.reshape(batch, seq, 2 * MLSTM_WIDTH), mlstm_conv_w[l], row(mlstm_conv_b[l]),
                         bf(jnp.swapaxes(mlstm_wq[l], 1, 2)), bf(mlstm_wk[l]),
                         bf(jnp.swapaxes(mlstm_wv[l], 1, 2)), bf(mlstm_w_if[l].T),
                         mlstm_b_if[l].reshape(-1, 1), row(mlstm_norm[l]), row(mlstm_skip[l]))
        x = _merge(x, row(mix_norm[l]), bf(w_in_l[:, g0:]), gate_bias[l].reshape(1, -1),
                   o_diff.reshape(m, aw), o_fox.reshape(m, aw),
                   o_mlstm.reshape(m, MLSTM_WIDTH), bf(w_branch_diff[l]), bf(w_branch_fox[l]),
                   bf(w_branch_mlstm[l]), bf(w_out[l]))

        last = l == depth - 1
        x = _ffn(x, row(ffn2_norm[l]), ffn2_w_gate, ffn2_w_up, ffn2_w_down, l,
                 row(final_norm) if last else None)
    return x.reshape(batch, seq, d)
```

```python
import functools
import math

import numpy as np
import jax
import jax.numpy as jnp
from jax import lax
from jax.experimental import pallas as pl
from jax.experimental.pallas import tpu as pltpu

D_MODEL = 1024
D_FF = 2816
RMS_EPS = 1e-6
LN_EPS = 1e-5
N_BRANCH = 3
DIFF_HEADS = 4
ATTN_HEAD_DIM = 64
FOX_HEADS = 8
MLSTM_HEADS = 4
MLSTM_HEAD_DIM = 128
MLSTM_WIDTH = MLSTM_HEADS * MLSTM_HEAD_DIM
MLSTM_CONV = 4
MLSTM_CHUNK = 128
MLSTM_STAGE_AHEAD = 2
ATTN_WIDTH = 512
LANES = 128
SUBLANES = 8
BF16_ROWS = 16
SCORE_SLOTS = 4
BIAS_TERMS = 3
VMEM_LIMIT_BYTES = 56 * 1024 * 1024
LOG2E = math.log2(math.e)
Q_SCALE = ATTN_HEAD_DIM ** -0.5 * LOG2E

F32 = jnp.float32
BF16 = jnp.bfloat16
NT_DIMS = (((1,), (1,)), ((), ()))
TN_DIMS = (((0,), (0,)), ((), ()))


def _params(*semantics):
    return pltpu.CompilerParams(dimension_semantics=semantics, vmem_limit_bytes=VMEM_LIMIT_BYTES)


def _rms_norm(x, w):
    return x * lax.rsqrt(jnp.mean(x * x, axis=-1, keepdims=True) + RMS_EPS) * w


def _silu(x):
    return x * jax.nn.sigmoid(x)


def _log_sigmoid(x):
    return jnp.minimum(x, 0.0) - jnp.log1p(jnp.exp(-jnp.abs(x)))


def _lane_cumsum(x):
    n = x.shape[-1]
    lane = lax.broadcasted_iota(jnp.int32, x.shape, 1)
    shift = 1
    while shift < n:
        x = x + jnp.where(lane >= shift, pltpu.roll(x, shift, axis=1), 0.0)
        shift *= 2
    return x


def _ffn_kernel(*refs, final):
    if final:
        x_ref, nw_ref, wg_ref, wu_ref, wd_ref, fn_ref, o_ref, xn_ref = refs
    else:
        x_ref, nw_ref, wg_ref, wu_ref, wd_ref, o_ref, xn_ref = refs
    f = pl.program_id(1)

    @pl.when(f == 0)
    def _():
        xn_ref[...] = _rms_norm(x_ref[...], nw_ref[...]).astype(BF16)
        o_ref[...] = jnp.zeros_like(o_ref)

    xn = xn_ref[...]
    g = jnp.dot(xn, wg_ref[...].astype(BF16), preferred_element_type=F32)
    u = jnp.dot(xn, wu_ref[...].astype(BF16), preferred_element_type=F32)
    h = (_silu(g) * u).astype(BF16)
    o_ref[...] += jnp.dot(h, wd_ref[...].astype(BF16), preferred_element_type=F32)

    @pl.when(f == pl.num_programs(1) - 1)
    def _():
        y = x_ref[...] + 0.5 * o_ref[...]
        if final:
            y = _rms_norm(y, fn_ref[...])
        o_ref[...] = y


def _ffn(x, norm_w, w_gate, w_up, w_down, layer, final_norm=None, *, tm=2048, tf=256):
    m, d = x.shape
    dff = w_gate.shape[2]
    final = final_norm is not None
    in_specs = [
        pl.BlockSpec((tm, d), lambda i, f: (i, 0)),
        pl.BlockSpec((1, d), lambda i, f: (0, 0)),
        pl.BlockSpec((None, d, tf), lambda i, f: (layer, 0, f)),
        pl.BlockSpec((None, d, tf), lambda i, f: (layer, 0, f)),
        pl.BlockSpec((None, tf, d), lambda i, f: (layer, f, 0)),
    ]
    args = [x, norm_w, w_gate, w_up, w_down]
    if final:
        in_specs.append(pl.BlockSpec((1, d), lambda i, f: (0, 0)))
        args.append(final_norm)
    return pl.pallas_call(
        functools.partial(_ffn_kernel, final=final),
        grid=(m // tm, dff // tf),
        in_specs=in_specs,
        out_specs=pl.BlockSpec((tm, d), lambda i, f: (i, 0)),
        out_shape=jax.ShapeDtypeStruct((m, d), F32),
        scratch_shapes=[pltpu.VMEM((tm, d), BF16)],
        compiler_params=_params("parallel", "arbitrary"),
        name="ffn_final" if final else "ffn",
    )(*args)


def _proj_kernel(x_ref, nw_ref, wqk_ref, wvt_ref, wm_ref, wff_ref, bf_ref,
                 qk_ref, vt_ref, mxz_ref, cf_ref, carry_ref, *, tiles_per_seq):
    i = pl.program_id(0)
    @pl.when(i == 0)
    def _():
        carry_ref[...] = jnp.zeros_like(carry_ref)

    xn = _rms_norm(x_ref[...], nw_ref[...]).astype(BF16)

    ff = lax.dot_general(wff_ref[...], xn, NT_DIMS, preferred_element_type=F32)
    csum = _lane_cumsum(_log_sigmoid(ff + bf_ref[...]))
    csum = csum + jnp.where(i % tiles_per_seq == 0, 0.0, carry_ref[:, 0:1])
    carry_ref[...] = jnp.broadcast_to(csum[:, -1:], carry_ref.shape)
    cf_ref[...] = -csum

    for grp in range(4):
        cols = slice(grp * ATTN_WIDTH, (grp + 1) * ATTN_WIDTH)
        r = jnp.dot(xn, wqk_ref[:, cols], preferred_element_type=F32)
        if grp % 2 == 0:
            r = r * Q_SCALE
        qk_ref[:, cols] = r.astype(BF16)
    for grp in range(2):
        rows = slice(grp * ATTN_WIDTH, (grp + 1) * ATTN_WIDTH)
        vt_ref[rows, :] = lax.dot_general(wvt_ref[rows, :], xn, NT_DIMS,
                                          preferred_element_type=F32).astype(BF16)
    for grp in range(2):
        cols = slice(grp * MLSTM_WIDTH, (grp + 1) * MLSTM_WIDTH)
        mxz_ref[:, cols] = jnp.dot(xn, wm_ref[:, cols], preferred_element_type=F32)


def _proj(x, norm_w, w_qk, w_vt, w_m, w_ff_t, b_f, *, seq, tm=512):
    m, d = x.shape
    nh = w_ff_t.shape[0]
    const = lambda i: (0, 0)
    return pl.pallas_call(
        functools.partial(_proj_kernel, tiles_per_seq=seq // tm),
        grid=(m // tm,),
        in_specs=[
            pl.BlockSpec((tm, d), lambda i: (i, 0)),
            pl.BlockSpec((1, d), const),
            pl.BlockSpec(w_qk.shape, const),
            pl.BlockSpec(w_vt.shape, const),
            pl.BlockSpec(w_m.shape, const),
            pl.BlockSpec(w_ff_t.shape, const),
            pl.BlockSpec((nh, 1), const),
        ],
        out_specs=[
            pl.BlockSpec((tm, w_qk.shape[1]), lambda i: (i, 0)),
            pl.BlockSpec((w_vt.shape[0], tm), lambda i: (0, i)),
            pl.BlockSpec((tm, w_m.shape[1]), lambda i: (i, 0)),
            pl.BlockSpec((nh, tm), lambda i: (0, i)),
        ],
        out_shape=[
            jax.ShapeDtypeStruct((m, w_qk.shape[1]), BF16),
            jax.ShapeDtypeStruct((w_vt.shape[0], m), BF16),
            jax.ShapeDtypeStruct((m, w_m.shape[1]), F32),
            jax.ShapeDtypeStruct((nh, m), F32),
        ],
        scratch_shapes=[pltpu.VMEM((nh, LANES), F32)],
        compiler_params=_params("arbitrary"),
        name="proj",
    )(x, norm_w, w_qk, w_vt, w_m, w_ff_t, b_f)


def _flash_pair(q_ref, k_ref, vt_ref, bias_ref, kaug_ref, vaug_ref, qm_ref, s_ref, acc_ref, finish, *, tq, tk, v_rows):
    seq = k_ref.shape[1]
    dr = v_rows[0].stop - v_rows[0].start
    half = LANES // 2

    kaug_ref[:, 0:LANES] = k_ref[0]
    terms = []
    for c in range(2):
        rem = bias_ref[0, c:c + 1, :] * LOG2E
        for _ in range(BIAS_TERMS):
            part = rem.astype(BF16).astype(F32)
            terms.append(part)
            rem = rem - part
    term_rows = jnp.concatenate(terms + [jnp.zeros((LANES - 2 * BIAS_TERMS, seq), F32)], axis=0)
    for blk in range(seq // LANES):
        sl = slice(blk * LANES, (blk + 1) * LANES)
        kaug_ref[sl, LANES:] = term_rows[:, sl].T.astype(BF16)
    for c in range(2):
        vaug_ref[c, 0:dr, :] = vt_ref[v_rows[c], :]
        vaug_ref[c, dr:, :] = jnp.ones((BF16_ROWS, seq), BF16)

    qlane = lax.broadcasted_iota(jnp.int32, (seq, LANES), 1)
    q = q_ref[0]
    for c in range(2):
        qm_ref[c, :, 0:LANES] = jnp.where((qlane >= c * half) & (qlane < (c + 1) * half), q, jnp.zeros_like(q))
        picks_bias = (qlane >= BIAS_TERMS * c) & (qlane < BIAS_TERMS * (c + 1))
        qm_ref[c, :, LANES:] = jnp.where(picks_bias, 1.0, 0.0).astype(BF16)

    work = []
    for i in range(seq // tq):
        n_full = (i * tq) // tk
        lead = i * tq - n_full * tk
        work += [(i, j * tk, tk, None, j == 0, False) for j in range(n_full)]
        work.append((i, n_full * tk, lead + tq, lead, n_full == 0, True))

    def scores(n):
        i, k0, size, _, _, _ = work[n]
        for c in range(2):
            s_ref[n % s_ref.shape[0], c, 0:size, :] = lax.dot_general(
                kaug_ref[k0:k0 + size, :], qm_ref[c, i * tq:(i + 1) * tq, :], NT_DIMS,
                preferred_element_type=F32)

    scores(0)
    scores(1)
    m = [None, None]
    for n, (i, k0, size, lead, first, last) in enumerate(work):
        if n + 2 < len(work):
            scores(n + 2)
        for c in range(2):
            s = s_ref[n % s_ref.shape[0], c, 0:size, :]
            if lead is not None:
                key = lax.broadcasted_iota(jnp.int32, (size, tq), 0)
                qry = lax.broadcasted_iota(jnp.int32, (size, tq), 1)
                s = jnp.where(key <= qry + lead, s, -jnp.inf)
            m_blk = jnp.max(s, axis=0, keepdims=True)
            m_new = m_blk if first else jnp.maximum(m[c], m_blk)
            p = jnp.exp2(s - m_new).astype(BF16)
            pv = jnp.dot(vaug_ref[c, :, k0:k0 + size], p, preferred_element_type=F32)
            if first:
                acc_ref[i, c] = pv
            else:
                acc_ref[i, c] = jnp.exp2(m[c] - m_new) * acc_ref[i, c] + pv
            m[c] = m_new
        if last:
            finish(i, *[acc_ref[i, c, 0:dr, :] * (1.0 / acc_ref[i, c, dr:dr + 1, :]) for c in range(2)])


def _diff_attn_kernel(q_ref, k_ref, vt_ref, bias_ref, lq1_ref, lk1_ref, lq2_ref, lk2_ref, sub_ref,
                      o_ref, *scratch, tq, tk, lam_init):
    lam = (jnp.exp(jnp.sum(lq1_ref[...] * lk1_ref[...], axis=1, keepdims=True))
           - jnp.exp(jnp.sum(lq2_ref[...] * lk2_ref[...], axis=1, keepdims=True)) + lam_init)

    def finish(i, a1, a2):
        d = (a1 - lam * a2).T
        o_ref[0, i * tq:(i + 1) * tq, :] = (_rms_norm(d, sub_ref[...]) * (1.0 - lam_init)).astype(o_ref.dtype)

    full = slice(0, LANES)
    _flash_pair(q_ref, k_ref, vt_ref, bias_ref, *scratch, finish, tq=tq, tk=tk, v_rows=(full, full))


def _fox_attn_kernel(q_ref, k_ref, vt_ref, bias_ref, o_ref, *scratch, tq, tk):
    def finish(i, a_even, a_odd):
        o_ref[0, i * tq:(i + 1) * tq, :] = jnp.concatenate([a_even, a_odd], axis=0).T.astype(o_ref.dtype)

    half = LANES // 2
    _flash_pair(q_ref, k_ref, vt_ref, bias_ref, *scratch, finish, tq=tq, tk=tk,
                v_rows=(slice(0, half), slice(half, LANES)))


def _attention(kernel, qk, vt, bias, extra, *, q_col, v_row, v_rows, bias_per_batch, tq=256, tk=512, name):
    b, s, _ = qk.shape
    groups = ATTN_WIDTH // LANES
    if bias_per_batch:
        bias_map = lambda bi, g: (g, 0, bi)
    else:
        bias_map = lambda bi, g: (g, 0, 0)
    in_specs = [
        pl.BlockSpec((1, s, LANES), lambda bi, g: (bi, 0, q_col + g)),
        pl.BlockSpec((1, s, LANES), lambda bi, g: (bi, 0, q_col + groups + g)),
        pl.BlockSpec((LANES, s), lambda bi, g: (v_row + g, bi)),
        pl.BlockSpec((1, 2, s), bias_map),
    ] + [pl.BlockSpec(e.shape, lambda bi, g: (0, 0)) for e in extra]
    rows = v_rows + BF16_ROWS
    return pl.pallas_call(
        functools.partial(kernel, tq=tq, tk=tk),
        grid=(b, groups),
        in_specs=in_specs,
        out_specs=pl.BlockSpec((1, s, LANES), lambda bi, g: (bi, 0, g)),
        out_shape=jax.ShapeDtypeStruct((b, s, ATTN_WIDTH), BF16),
        scratch_shapes=[pltpu.VMEM((s, 2 * LANES), BF16), pltpu.VMEM((2, rows, s), BF16),
                        pltpu.VMEM((2, s, 2 * LANES), BF16), pltpu.VMEM((SCORE_SLOTS, 2, tk, tq), F32),
                        pltpu.VMEM((s // tq, 2, rows, tq), F32)],
        compiler_params=_params("parallel", "parallel"),
        name=name,
    )(qk, qk, vt, bias, *extra)


def _mlstm_kernel(cur_ref, prev_ref, z_ref, cw_ref, cb_ref, wqt_ref, wk_ref, wvt_ref, wif_ref, bif_ref,
                  nrm_ref, skip_ref, o_ref, xbuf_ref, ct_ref, n_ref, m_ref):
    ci = pl.program_id(1)
    nb, span, _ = cur_ref.shape
    chunks = span // MLSTM_CHUNK

    @pl.when(ci == 0)
    def _():
        ct_ref[...] = jnp.zeros_like(ct_ref)
        n_ref[...] = jnp.zeros_like(n_ref)
        m_ref[...] = jnp.zeros_like(m_ref)

    for r in range(nb):
        xbuf_ref[r, 0:SUBLANES, :] = jnp.where(ci > 0, prev_ref[r], 0.0)
        xbuf_ref[r, SUBLANES:, :] = cur_ref[r]

    work = [(r, k) for k in range(chunks) for r in range(nb)]
    refs = (cw_ref, cb_ref, wqt_ref, wk_ref, wvt_ref, wif_ref, bif_ref)
    ahead = min(MLSTM_STAGE_AHEAD, len(work))
    staged = {n: _mlstm_gates(xbuf_ref.at[work[n][0]], work[n][1], *refs) for n in range(ahead)}
    for n, (r, k) in enumerate(work):
        if n + ahead < len(work):
            staged[n + ahead] = _mlstm_gates(xbuf_ref.at[work[n + ahead][0]], work[n + ahead][1], *refs)
        rows = slice(k * MLSTM_CHUNK, (k + 1) * MLSTM_CHUNK)
        _mlstm_recur(staged.pop(n), z_ref.at[r, rows], nrm_ref, skip_ref, o_ref.at[r, rows],
                     ct_ref.at[r], n_ref.at[r], m_ref.at[r])


def _mlstm_gates(xbuf_ref, k, cw_ref, cb_ref, wqt_ref, wk_ref, wvt_ref, wif_ref, bif_ref):
    L = MLSTM_CHUNK
    dh = MLSTM_HEAD_DIM
    nh = MLSTM_HEADS

    base = SUBLANES + k * L
    cur = xbuf_ref[base:base + L, :]
    conv = cb_ref[...]
    for j in range(MLSTM_CONV):
        off = base - (MLSTM_CONV - 1) + j
        conv = conv + cw_ref[j:j + 1, :] * xbuf_ref[off:off + L, :]
    xc = _silu(conv)

    qts, kss, vts = [], [], []
    ift = bif_ref[...]
    for h in range(nh):
        sl = slice(h * dh, (h + 1) * dh)
        xh = xc[:, sl].astype(BF16)
        vh = cur[:, sl].astype(BF16)
        qt = lax.dot_general(wqt_ref[h], xh, NT_DIMS, preferred_element_type=F32).astype(BF16)
        kk = jnp.dot(xh, wk_ref[h], preferred_element_type=F32)
        vt = lax.dot_general(wvt_ref[h], vh, NT_DIMS, preferred_element_type=F32).astype(BF16)
        ift = ift + jnp.dot(wif_ref[:, sl], qt, preferred_element_type=F32)
        ift = ift + lax.dot_general(wif_ref[:, MLSTM_WIDTH + h * dh:MLSTM_WIDTH + (h + 1) * dh],
                                    kk.astype(BF16), NT_DIMS, preferred_element_type=F32)
        ift = ift + jnp.dot(wif_ref[:, 2 * MLSTM_WIDTH + h * dh:2 * MLSTM_WIDTH + (h + 1) * dh], vt,
                            preferred_element_type=F32)
        qts.append(qt)
        kss.append((kk * (dh ** -0.5)).astype(BF16))
        vts.append(vt)

    key = lax.broadcasted_iota(jnp.int32, (L, L), 0)
    qry = lax.broadcasted_iota(jnp.int32, (L, L), 1)
    lf = _log_sigmoid(ift)
    tri = jnp.where(key <= qry, 1.0, 0.0).astype(BF16)
    b_all = jnp.zeros_like(lf)
    rem = lf
    for _ in range(3):
        part = rem.astype(BF16)
        b_all = b_all + jnp.dot(part, tri, preferred_element_type=F32)
        rem = rem - part.astype(F32)
    return dict(xc=xc, qts=qts, kss=kss, vts=vts, li_rows=ift[0:nh], b_rows=b_all[nh:2 * nh])


def _mlstm_recur(st, z_ref, nrm_ref, skip_ref, o_ref, ct_ref, n_ref, m_ref):
    L = MLSTM_CHUNK
    dh = MLSTM_HEAD_DIM
    key = lax.broadcasted_iota(jnp.int32, (L, L), 0)
    qry = lax.broadcasted_iota(jnp.int32, (L, L), 1)
    causal = key <= qry
    xc = st["xc"]

    heads = range(MLSTM_HEADS)
    qts, kss, vts = st["qts"], st["kss"], st["vts"]
    ct_state = [ct_ref[h] for h in heads]
    n_state = [n_ref[h] for h in heads]
    m_prev = [m_ref[h:h + 1, 0:1] for h in heads]
    qk = [jnp.dot(kss[h], qts[h], preferred_element_type=F32) for h in heads]
    cq = [jnp.dot(ct_state[h].astype(BF16), qts[h], preferred_element_type=F32) for h in heads]
    nq = [jnp.dot(n_state[h].astype(BF16), qts[h], preferred_element_type=F32)[0:1] for h in heads]

    li = [st["li_rows"][h:h + 1] for h in heads]
    b = [st["b_rows"][h:h + 1] for h in heads]
    a = [li[h] - b[h] for h in heads]
    a_keys = [jnp.broadcast_to(a[h], (L, L)).T for h in heads]
    d = [jnp.where(causal, b[h] + a_keys[h], -jnp.inf) for h in heads]
    inter = [b[h] + m_prev[h] for h in heads]
    m_t = [jnp.maximum(inter[h], jnp.max(d[h], axis=0, keepdims=True)) for h in heads]
    w_inter = [jnp.exp(inter[h] - m_t[h]) for h in heads]
    s = [qk[h] * jnp.exp(d[h] - m_t[h]) for h in heads]
    sv = [jnp.dot(vts[h], s[h].astype(BF16), preferred_element_type=F32) for h in heads]

    b_last = [b[h][:, L - 1:L] for h in heads]
    g = [b_last[h] + a[h] for h in heads]
    m_new = [jnp.maximum(b_last[h] + m_prev[h], jnp.max(g[h], axis=1, keepdims=True)) for h in heads]
    wk = [jnp.exp(g[h] - m_new[h]) for h in heads]
    dec = [jnp.exp(b_last[h] + m_prev[h] - m_new[h]) for h in heads]
    for h in heads:
        ct_ref[h] = dec[h] * ct_state[h] + jnp.dot((vts[h].astype(F32) * wk[h]).astype(BF16), kss[h],
                                                   preferred_element_type=F32)
        n_ref[h] = dec[h] * n_state[h] + jnp.dot(jnp.broadcast_to(wk[h], (SUBLANES, L)).astype(BF16),
                                                 kss[h], preferred_element_type=F32)
        m_ref[h:h + 1, :] = jnp.broadcast_to(m_new[h], (1, LANES))

    for h in heads:
        sl = slice(h * dh, (h + 1) * dh)
        num = w_inter[h] * cq[h] + sv[h]
        den = w_inter[h] * nq[h] + jnp.sum(s[h], axis=0, keepdims=True)
        hh = num * (1.0 / jnp.maximum(jnp.abs(den), jnp.exp(-m_t[h])))
        mu = jnp.mean(hh, axis=0, keepdims=True)
        cen = hh - mu
        var = jnp.mean(cen * cen, axis=0, keepdims=True)
        hn = (cen * lax.rsqrt(var + LN_EPS)).T * nrm_ref[:, sl]
        o_ref[:, sl] = ((hn + skip_ref[:, sl] * xc[:, sl]) * _silu(z_ref[:, sl])).astype(o_ref.dtype)


def _mlstm(mxz, conv_w, conv_b, wq, wk, wv, w_if_t, b_if, norm_w, skip_w, *, nb=2, chunks=4):
    b, s, _ = mxz.shape
    span = chunks * MLSTM_CHUNK
    w = MLSTM_WIDTH
    rows_per_span = span // SUBLANES
    const2 = lambda bi, c: (0, 0)
    const3 = lambda bi, c: (0, 0, 0)
    return pl.pallas_call(
        _mlstm_kernel,
        grid=(b // nb, s // span),
        in_specs=[
            pl.BlockSpec((nb, span, w), lambda bi, c: (bi, c, 0)),
            pl.BlockSpec((nb, SUBLANES, w), lambda bi, c: (bi, jnp.maximum(c * rows_per_span - 1, 0), 0)),
            pl.BlockSpec((nb, span, w), lambda bi, c: (bi, c, 1)),
            pl.BlockSpec(conv_w.shape, const2),
            pl.BlockSpec(conv_b.shape, const2),
            pl.BlockSpec(wq.shape, const3),
            pl.BlockSpec(wk.shape, const3),
            pl.BlockSpec(wv.shape, const3),
            pl.BlockSpec(w_if_t.shape, const2),
            pl.BlockSpec(b_if.shape, const2),
            pl.BlockSpec(norm_w.shape, const2),
            pl.BlockSpec(skip_w.shape, const2),
        ],
        out_specs=pl.BlockSpec((nb, span, w), lambda bi, c: (bi, c, 0)),
        out_shape=jax.ShapeDtypeStruct((b, s, w), BF16),
        scratch_shapes=[
            pltpu.VMEM((nb, SUBLANES + span, w), F32),
            pltpu.VMEM((nb, MLSTM_HEADS, MLSTM_HEAD_DIM, MLSTM_HEAD_DIM), F32),
            pltpu.VMEM((nb, MLSTM_HEADS, SUBLANES, MLSTM_HEAD_DIM), F32),
            pltpu.VMEM((nb, SUBLANES, LANES), F32),
        ],
        compiler_params=_params("parallel", "arbitrary"),
        name="mlstm",
    )(mxz, mxz, mxz, conv_w, conv_b, wq, wk, wv, w_if_t, b_if, norm_w, skip_w)


def _merge_kernel(x_ref, nw_ref, wg_ref, gb_ref, od_ref, of_ref, om_ref, wbd_ref, wbf_ref, wbm_ref,
                  wo_ref, o_ref):
    x = x_ref[...]
    d = x.shape[1]
    xn = _rms_norm(x, nw_ref[...]).astype(BF16)
    merged = None
    for br, (b_ref, w_ref) in enumerate(((od_ref, wbd_ref), (of_ref, wbf_ref), (om_ref, wbm_ref))):
        cols = slice(br * d, (br + 1) * d)
        gate = jax.nn.sigmoid(jnp.dot(xn, wg_ref[:, cols], preferred_element_type=F32) + gb_ref[:, cols])
        term = gate * jnp.dot(b_ref[...], w_ref[...], preferred_element_type=F32)
        merged = term if merged is None else merged + term
    o_ref[...] = x + jnp.dot(merged.astype(BF16), wo_ref[...], preferred_element_type=F32)


def _merge(x, norm_w, w_g, gate_b, o_diff, o_fox, o_mlstm, wbd, wbf, wbm, w_out, *, tm=512):
    m, d = x.shape
    const = lambda i: (0, 0)
    row = lambda i: (i, 0)
    return pl.pallas_call(
        _merge_kernel,
        grid=(m // tm,),
        in_specs=[
            pl.BlockSpec((tm, d), row),
            pl.BlockSpec((1, d), const),
            pl.BlockSpec(w_g.shape, const),
            pl.BlockSpec(gate_b.shape, const),
            pl.BlockSpec((tm, ATTN_WIDTH), row),
            pl.BlockSpec((tm, ATTN_WIDTH), row),
            pl.BlockSpec((tm, MLSTM_WIDTH), row),
            pl.BlockSpec(wbd.shape, const),
            pl.BlockSpec(wbf.shape, const),
            pl.BlockSpec(wbm.shape, const),
            pl.BlockSpec(w_out.shape, const),
        ],
        out_specs=pl.BlockSpec((tm, d), row),
        out_shape=jax.ShapeDtypeStruct((m, d), F32),
        compiler_params=_params("parallel"),
        name="merge",
    )(x, norm_w, w_g, gate_b, o_diff, o_fox, o_mlstm, wbd, wbf, wbm, w_out)


def _alibi_bias(seq):
    slopes = np.array([2.0 ** (-8.0 * (h + 1) / DIFF_HEADS) for h in range(DIFF_HEADS)], dtype=np.float32)
    pos = np.arange(seq, dtype=np.float32)
    per_head = slopes[:, None] * pos[None, :]
    return jnp.asarray(np.repeat(per_head[:, None, :], 2, axis=1))


def kernel(x, ffn1_norm, ffn1_w_gate, ffn1_w_up, ffn1_w_down, mix_norm, w_in, gate_bias, diff_lq1, diff_lk1, diff_lq2, diff_lk2, diff_subln, fox_b_f, mlstm_conv_w, mlstm_conv_b, mlstm_wq, mlstm_wk, mlstm_wv, mlstm_w_if, mlstm_b_if, mlstm_norm, mlstm_skip, w_branch_diff, w_branch_fox, w_branch_mlstm, w_out, ffn2_norm, ffn2_w_gate, ffn2_w_up, ffn2_w_down, final_norm):
    batch, seq, d = x.shape
    depth = w_in.shape[0]
    m = batch * seq
    bf = lambda a: a.astype(BF16)
    row = lambda a: a.reshape(1, -1)
    aw = ATTN_WIDTH
    ff0 = 6 * aw
    mx0 = ff0 + FOX_HEADS
    g0 = mx0 + 2 * MLSTM_WIDTH
    blocks = aw // LANES
    alibi = _alibi_bias(seq)

    x = x.reshape(m, d)
    for l in range(depth):
        x = _ffn(x, row(ffn1_norm[l]), ffn1_w_gate, ffn1_w_up, ffn1_w_down, l)

        w_in_l = w_in[l]
        w_qk = bf(jnp.concatenate([w_in_l[:, 0:2 * aw], w_in_l[:, 3 * aw:5 * aw]], axis=1))
        w_vt = bf(jnp.concatenate([w_in_l[:, 2 * aw:3 * aw], w_in_l[:, 5 * aw:6 * aw]], axis=1).T)
        qk, vt, mxz, cf = _proj(x, row(mix_norm[l]), w_qk, w_vt, bf(w_in_l[:, mx0:g0]),
                                bf(w_in_l[:, ff0:mx0].T), fox_b_f[l].reshape(-1, 1), seq=seq)
        qk = qk.reshape(batch, seq, 4 * aw)
        lam_init = 0.8 - 0.6 * math.exp(-0.3 * l)
        o_diff = _attention(
            functools.partial(_diff_attn_kernel, lam_init=lam_init), qk, vt, alibi,
            [row(diff_lq1[l]), row(diff_lk1[l]), row(diff_lq2[l]), row(diff_lk2[l]), row(diff_subln[l])],
            q_col=0, v_row=0, v_rows=LANES, bias_per_batch=False, name="diff_attn")
        o_fox = _attention(_fox_attn_kernel, qk, vt, cf.reshape(FOX_HEADS // 2, 2, m), [],
                           q_col=2 * blocks, v_row=blocks, v_rows=LANES // 2, bias_per_batch=True,
                           name="fox_attn")
        o_mlstm = _mlstm(mxz.reshape(batch, seq, 2 * MLSTM_WIDTH), mlstm_conv_w[l], row(mlstm_conv_b[l]),
                         bf(jnp.swapaxes(mlstm_wq[l], 1, 2)), bf(mlstm_wk[l]),
                         bf(jnp.swapaxes(mlstm_wv[l], 1, 2)), bf(mlstm_w_if[l].T),
                         mlstm_b_if[l].reshape(-1, 1), row(mlstm_norm[l]), row(mlstm_skip[l]))
        x = _merge(x, row(mix_norm[l]), bf(w_in_l[:, g0:]), gate_bias[l].reshape(1, -1),
                   o_diff.reshape(m, aw), o_fox.reshape(m, aw),
                   o_mlstm.reshape(m, MLSTM_WIDTH), bf(w_branch_diff[l]), bf(w_branch_fox[l]),
                   bf(w_branch_mlstm[l]), bf(w_out[l]))

        last = l == depth - 1
        x = _ffn(x, row(ffn2_norm[l]), ffn2_w_gate, ffn2_w_up, ffn2_w_down, l,
                 row(final_norm) if last else None)
    return x.reshape(batch, seq, d)
```

```python
import functools
import math

import numpy as np
import jax
import jax.numpy as jnp
from jax import lax
from jax.experimental import pallas as pl
from jax.experimental.pallas import tpu as pltpu

D_MODEL = 1024
D_FF = 2816
RMS_EPS = 1e-6
LN_EPS = 1e-5
N_BRANCH = 3
DIFF_HEADS = 4
ATTN_HEAD_DIM = 64
FOX_HEADS = 8
MLSTM_HEADS = 4
MLSTM_HEAD_DIM = 128
MLSTM_WIDTH = MLSTM_HEADS * MLSTM_HEAD_DIM
MLSTM_CONV = 4
MLSTM_CHUNK = 128
MLSTM_STAGE_AHEAD = 2
ATTN_WIDTH = 512
LANES = 128
SUBLANES = 8
BF16_ROWS = 16
SCORE_SLOTS = 4
BIAS_TERMS = 3
VMEM_LIMIT_BYTES = 56 * 1024 * 1024
LOG2E = math.log2(math.e)
Q_SCALE = ATTN_HEAD_DIM ** -0.5 * LOG2E

F32 = jnp.float32
BF16 = jnp.bfloat16
NT_DIMS = (((1,), (1,)), ((), ()))
TN_DIMS = (((0,), (0,)), ((), ()))


def _params(*semantics):
    return pltpu.CompilerParams(dimension_semantics=semantics, vmem_limit_bytes=VMEM_LIMIT_BYTES)


def _rms_norm(x, w):
    return x * lax.rsqrt(jnp.mean(x * x, axis=-1, keepdims=True) + RMS_EPS) * w


def _silu(x):
    return x * jax.nn.sigmoid(x)


def _log_sigmoid(x):
    return jnp.minimum(x, 0.0) - jnp.log1p(jnp.exp(-jnp.abs(x)))


def _lane_cumsum(x):
    n = x.shape[-1]
    lane = lax.broadcasted_iota(jnp.int32, x.shape, 1)
    shift = 1
    while shift < n:
        x = x + jnp.where(lane >= shift, pltpu.roll(x, shift, axis=1), 0.0)
        shift *= 2
    return x


def _ffn_kernel(*refs, final):
    if final:
        x_ref, nw_ref, wg_ref, wu_ref, wd_ref, fn_ref, o_ref, xn_ref = refs
    else:
        x_ref, nw_ref, wg_ref, wu_ref, wd_ref, o_ref, xn_ref = refs
    f = pl.program_id(1)

    @pl.when(f == 0)
    def _():
        xn_ref[...] = _rms_norm(x_ref[...], nw_ref[...]).astype(BF16)
        o_ref[...] = jnp.zeros_like(o_ref)

    xn = xn_ref[...]
    g = jnp.dot(xn, wg_ref[...].astype(BF16), preferred_element_type=F32)
    u = jnp.dot(xn, wu_ref[...].astype(BF16), preferred_element_type=F32)
    h = (_silu(g) * u).astype(BF16)
    o_ref[...] += jnp.dot(h, wd_ref[...].astype(BF16), preferred_element_type=F32)

    @pl.when(f == pl.num_programs(1) - 1)
    def _():
        y = x_ref[...] + 0.5 * o_ref[...]
        if final:
            y = _rms_norm(y, fn_ref[...])
        o_ref[...] = y


def _ffn(x, norm_w, w_gate, w_up, w_down, layer, final_norm=None, *, tm=2048, tf=256):
    m, d = x.shape
    dff = w_gate.shape[2]
    final = final_norm is not None
    in_specs = [
        pl.BlockSpec((tm, d), lambda i, f: (i, 0)),
        pl.BlockSpec((1, d), lambda i, f: (0, 0)),
        pl.BlockSpec((None, d, tf), lambda i, f: (layer, 0, f)),
        pl.BlockSpec((None, d, tf), lambda i, f: (layer, 0, f)),
        pl.BlockSpec((None, tf, d), lambda i, f: (layer, f, 0)),
    ]
    args = [x, norm_w, w_gate, w_up, w_down]
    if final:
        in_specs.append(pl.BlockSpec((1, d), lambda i, f: (0, 0)))
        args.append(final_norm)
    return pl.pallas_call(
        functools.partial(_ffn_kernel, final=final),
        grid=(m // tm, dff // tf),
        in_specs=in_specs,
        out_specs=pl.BlockSpec((tm, d), lambda i, f: (i, 0)),
        out_shape=jax.ShapeDtypeStruct((m, d), F32),
        scratch_shapes=[pltpu.VMEM((tm, d), BF16)],
        compiler_params=_params("parallel", "arbitrary"),
        name="ffn_final" if final else "ffn",
    )(*args)


def _proj_kernel(x_ref, nw_ref, wqk_ref, wvt_ref, wm_ref, wff_ref, bf_ref,
                 qk_ref, vt_ref, mxz_ref, cf_ref, carry_ref, *, tiles_per_seq):
    i = pl.program_id(0)
    @pl.when(i == 0)
    def _():
        carry_ref[...] = jnp.zeros_like(carry_ref)

    xn = _rms_norm(x_ref[...], nw_ref[...]).astype(BF16)

    ff = lax.dot_general(wff_ref[...], xn, NT_DIMS, preferred_element_type=F32)
    csum = _lane_cumsum(_log_sigmoid(ff + bf_ref[...]))
    csum = csum + jnp.where(i % tiles_per_seq == 0, 0.0, carry_ref[:, 0:1])
    carry_ref[...] = jnp.broadcast_to(csum[:, -1:], carry_ref.shape)
    cf_ref[...] = -csum

    for grp in range(4):
        cols = slice(grp * ATTN_WIDTH, (grp + 1) * ATTN_WIDTH)
        r = jnp.dot(xn, wqk_ref[:, cols], preferred_element_type=F32)
        if grp % 2 == 0:
            r = r * Q_SCALE
        qk_ref[:, cols] = r.astype(BF16)
    for grp in range(2):
        rows = slice(grp * ATTN_WIDTH, (grp + 1) * ATTN_WIDTH)
        vt_ref[rows, :] = lax.dot_general(wvt_ref[rows, :], xn, NT_DIMS,
                                          preferred_element_type=F32).astype(BF16)
    for grp in range(2):
        cols = slice(grp * MLSTM_WIDTH, (grp + 1) * MLSTM_WIDTH)
        mxz_ref[:, cols] = jnp.dot(xn, wm_ref[:, cols], preferred_element_type=F32)


def _proj(x, norm_w, w_qk, w_vt, w_m, w_ff_t, b_f, *, seq, tm=512):
    m, d = x.shape
    nh = w_ff_t.shape[0]
    const = lambda i: (0, 0)
    return pl.pallas_call(
        functools.partial(_proj_kernel, tiles_per_seq=seq // tm),
        grid=(m // tm,),
        in_specs=[
            pl.BlockSpec((tm, d), lambda i: (i, 0)),
            pl.BlockSpec((1, d), const),
            pl.BlockSpec(w_qk.shape, const),
            pl.BlockSpec(w_vt.shape, const),
            pl.BlockSpec(w_m.shape, const),
            pl.BlockSpec(w_ff_t.shape, const),
            pl.BlockSpec((nh, 1), const),
        ],
        out_specs=[
            pl.BlockSpec((tm, w_qk.shape[1]), lambda i: (i, 0)),
            pl.BlockSpec((w_vt.shape[0], tm), lambda i: (0, i)),
            pl.BlockSpec((tm, w_m.shape[1]), lambda i: (i, 0)),
            pl.BlockSpec((nh, tm), lambda i: (0, i)),
        ],
        out_shape=[
            jax.ShapeDtypeStruct((m, w_qk.shape[1]), BF16),
            jax.ShapeDtypeStruct((w_vt.shape[0], m), BF16),
            jax.ShapeDtypeStruct((m, w_m.shape[1]), F32),
            jax.ShapeDtypeStruct((nh, m), F32),
        ],
        scratch_shapes=[pltpu.VMEM((nh, LANES), F32)],
        compiler_params=_params("arbitrary"),
        name="proj",
    )(x, norm_w, w_qk, w_vt, w_m, w_ff_t, b_f)


def _flash_pair(q_ref, k_ref, vt_ref, bias_ref, kaug_ref, vaug_ref, qm_ref, s_ref, acc_ref, finish, *, tq, tk, v_rows):
    seq = k_ref.shape[1]
    dr = v_rows[0].stop - v_rows[0].start
    half = LANES // 2

    kaug_ref[:, 0:LANES] = k_ref[0]
    terms = []
    for c in range(2):
        rem = bias_ref[0, c:c + 1, :] * LOG2E
        for _ in range(BIAS_TERMS):
            part = rem.astype(BF16).astype(F32)
            terms.append(part)
            rem = rem - part
    term_rows = jnp.concatenate(terms + [jnp.zeros((LANES - 2 * BIAS_TERMS, seq), F32)], axis=0)
    for blk in range(seq // LANES):
        sl = slice(blk * LANES, (blk + 1) * LANES)
        kaug_ref[sl, LANES:] = term_rows[:, sl].T.astype(BF16)
    for c in range(2):
        vaug_ref[c, 0:dr, :] = vt_ref[v_rows[c], :]
        vaug_ref[c, dr:, :] = jnp.ones((BF16_ROWS, seq), BF16)

    qlane = lax.broadcasted_iota(jnp.int32, (seq, LANES), 1)
    q = q_ref[0]
    for c in range(2):
        qm_ref[c, :, 0:LANES] = jnp.where((qlane >= c * half) & (qlane < (c + 1) * half), q, jnp.zeros_like(q))
        picks_bias = (qlane >= BIAS_TERMS * c) & (qlane < BIAS_TERMS * (c + 1))
        qm_ref[c, :, LANES:] = jnp.where(picks_bias, 1.0, 0.0).astype(BF16)

    work = []
    for i in range(seq // tq):
        n_full = (i * tq) // tk
        lead = i * tq - n_full * tk
        work += [(i, j * tk, tk, None, j == 0, False) for j in range(n_full)]
        work.append((i, n_full * tk, lead + tq, lead, n_full == 0, True))

    def scores(n):
        i, k0, size, _, _, _ = work[n]
        for c in range(2):
            s_ref[n % s_ref.shape[0], c, 0:size, :] = lax.dot_general(
                kaug_ref[k0:k0 + size, :], qm_ref[c, i * tq:(i + 1) * tq, :], NT_DIMS,
                preferred_element_type=F32)

    scores(0)
    scores(1)
    m = [None, None]
    for n, (i, k0, size, lead, first, last) in enumerate(work):
        if n + 2 < len(work):
            scores(n + 2)
        for c in range(2):
            s = s_ref[n % s_ref.shape[0], c, 0:size, :]
            if lead is not None:
                key = lax.broadcasted_iota(jnp.int32, (tq, tq), 0)
                qry = lax.broadcasted_iota(jnp.int32, (tq, tq), 1)
                diag = jnp.where(key <= qry, s[lead:, :], -jnp.inf)
                s = diag if lead == 0 else jnp.concatenate([s[0:lead, :], diag], axis=0)
            m_blk = jnp.max(s, axis=0, keepdims=True)
            m_new = m_blk if first else jnp.maximum(m[c], m_blk)
            p = jnp.exp2(s - m_new).astype(BF16)
            pv = jnp.dot(vaug_ref[c, :, k0:k0 + size], p, preferred_element_type=F32)
            if first:
                acc_ref[i, c] = pv
            else:
                acc_ref[i, c] = jnp.exp2(m[c] - m_new) * acc_ref[i, c] + pv
            m[c] = m_new
        if last:
            finish(i, *[acc_ref[i, c, 0:dr, :] * (1.0 / acc_ref[i, c, dr:dr + 1, :]) for c in range(2)])


def _diff_attn_kernel(q_ref, k_ref, vt_ref, bias_ref, lq1_ref, lk1_ref, lq2_ref, lk2_ref, sub_ref,
                      o_ref, *scratch, tq, tk, lam_init):
    lam = (jnp.exp(jnp.sum(lq1_ref[...] * lk1_ref[...], axis=1, keepdims=True))
           - jnp.exp(jnp.sum(lq2_ref[...] * lk2_ref[...], axis=1, keepdims=True)) + lam_init)

    def finish(i, a1, a2):
        d = (a1 - lam * a2).T
        o_ref[0, i * tq:(i + 1) * tq, :] = (_rms_norm(d, sub_ref[...]) * (1.0 - lam_init)).astype(o_ref.dtype)

    full = slice(0, LANES)
    _flash_pair(q_ref, k_ref, vt_ref, bias_ref, *scratch, finish, tq=tq, tk=tk, v_rows=(full, full))


def _fox_attn_kernel(q_ref, k_ref, vt_ref, bias_ref, o_ref, *scratch, tq, tk):
    def finish(i, a_even, a_odd):
        o_ref[0, i * tq:(i + 1) * tq, :] = jnp.concatenate([a_even, a_odd], axis=0).T.astype(o_ref.dtype)

    half = LANES // 2
    _flash_pair(q_ref, k_ref, vt_ref, bias_ref, *scratch, finish, tq=tq, tk=tk,
                v_rows=(slice(0, half), slice(half, LANES)))


def _attention(kernel, qk, vt, bias, extra, *, q_col, v_row, v_rows, bias_per_batch, tq=256, tk=1024, name):
    b, s, _ = qk.shape
    groups = ATTN_WIDTH // LANES
    if bias_per_batch:
        bias_map = lambda bi, g: (g, 0, bi)
    else:
        bias_map = lambda bi, g: (g, 0, 0)
    in_specs = [
        pl.BlockSpec((1, s, LANES), lambda bi, g: (bi, 0, q_col + g)),
        pl.BlockSpec((1, s, LANES), lambda bi, g: (bi, 0, q_col + groups + g)),
        pl.BlockSpec((LANES, s), lambda bi, g: (v_row + g, bi)),
        pl.BlockSpec((1, 2, s), bias_map),
    ] + [pl.BlockSpec(e.shape, lambda bi, g: (0, 0)) for e in extra]
    rows = v_rows + BF16_ROWS
    return pl.pallas_call(
        functools.partial(kernel, tq=tq, tk=tk),
        grid=(b, groups),
        in_specs=in_specs,
        out_specs=pl.BlockSpec((1, s, LANES), lambda bi, g: (bi, 0, g)),
        out_shape=jax.ShapeDtypeStruct((b, s, ATTN_WIDTH), BF16),
        scratch_shapes=[pltpu.VMEM((s, 2 * LANES), BF16), pltpu.VMEM((2, rows, s), BF16),
                        pltpu.VMEM((2, s, 2 * LANES), BF16), pltpu.VMEM((SCORE_SLOTS, 2, tk, tq), F32),
                        pltpu.VMEM((s // tq, 2, rows, tq), F32)],
        compiler_params=_params("parallel", "parallel"),
        name=name,
    )(qk, qk, vt, bias, *extra)


def _mlstm_kernel(cur_ref, prev_ref, z_ref, cw_ref, cb_ref, wqt_ref, wk_ref, wvt_ref, wif_ref, bif_ref,
                  nrm_ref, skip_ref, o_ref, xbuf_ref, ct_ref, n_ref, m_ref):
    ci = pl.program_id(1)
    nb, span, _ = cur_ref.shape
    chunks = span // MLSTM_CHUNK

    @pl.when(ci == 0)
    def _():
        ct_ref[...] = jnp.zeros_like(ct_ref)
        n_ref[...] = jnp.zeros_like(n_ref)
        m_ref[...] = jnp.zeros_like(m_ref)

    for r in range(nb):
        xbuf_ref[r, 0:SUBLANES, :] = jnp.where(ci > 0, prev_ref[r], 0.0)
        xbuf_ref[r, SUBLANES:, :] = cur_ref[r]

    work = [(r, k) for k in range(chunks) for r in range(nb)]
    refs = (cw_ref, cb_ref, wqt_ref, wk_ref, wvt_ref, wif_ref, bif_ref)
    ahead = min(MLSTM_STAGE_AHEAD, len(work))
    staged = {n: _mlstm_gates(xbuf_ref.at[work[n][0]], work[n][1], *refs) for n in range(ahead)}
    for n, (r, k) in enumerate(work):
        if n + ahead < len(work):
            staged[n + ahead] = _mlstm_gates(xbuf_ref.at[work[n + ahead][0]], work[n + ahead][1], *refs)
        rows = slice(k * MLSTM_CHUNK, (k + 1) * MLSTM_CHUNK)
        _mlstm_recur(staged.pop(n), z_ref.at[r, rows], nrm_ref, skip_ref, o_ref.at[r, rows],
                     ct_ref.at[r], n_ref.at[r], m_ref.at[r])


def _mlstm_gates(xbuf_ref, k, cw_ref, cb_ref, wqt_ref, wk_ref, wvt_ref, wif_ref, bif_ref):
    L = MLSTM_CHUNK
    dh = MLSTM_HEAD_DIM
    nh = MLSTM_HEADS

    base = SUBLANES + k * L
    cur = xbuf_ref[base:base + L, :]
    conv = cb_ref[...]
    for j in range(MLSTM_CONV):
        off = base - (MLSTM_CONV - 1) + j
        conv = conv + cw_ref[j:j + 1, :] * xbuf_ref[off:off + L, :]
    xc = _silu(conv)

    qts, kss, vts = [], [], []
    ift = bif_ref[...]
    for h in range(nh):
        sl = slice(h * dh, (h + 1) * dh)
        xh = xc[:, sl].astype(BF16)
        vh = cur[:, sl].astype(BF16)
        qt = lax.dot_general(wqt_ref[h], xh, NT_DIMS, preferred_element_type=F32).astype(BF16)
        kk = jnp.dot(xh, wk_ref[h], preferred_element_type=F32)
        vt = lax.dot_general(wvt_ref[h], vh, NT_DIMS, preferred_element_type=F32).astype(BF16)
        ift = ift + jnp.dot(wif_ref[:, sl], qt, preferred_element_type=F32)
        ift = ift + lax.dot_general(wif_ref[:, MLSTM_WIDTH + h * dh:MLSTM_WIDTH + (h + 1) * dh],
                                    kk.astype(BF16), NT_DIMS, preferred_element_type=F32)
        ift = ift + jnp.dot(wif_ref[:, 2 * MLSTM_WIDTH + h * dh:2 * MLSTM_WIDTH + (h + 1) * dh], vt,
                            preferred_element_type=F32)
        qts.append(qt)
        kss.append((kk * (dh ** -0.5)).astype(BF16))
        vts.append(vt)

    key = lax.broadcasted_iota(jnp.int32, (L, L), 0)
    qry = lax.broadcasted_iota(jnp.int32, (L, L), 1)
    lf = _log_sigmoid(ift)
    tri = jnp.where(key <= qry, 1.0, 0.0).astype(BF16)
    b_all = jnp.zeros_like(lf)
    rem = lf
    for _ in range(3):
        part = rem.astype(BF16)
        b_all = b_all + jnp.dot(part, tri, preferred_element_type=F32)
        rem = rem - part.astype(F32)
    return dict(xc=xc, qts=qts, kss=kss, vts=vts, li_rows=ift[0:nh], b_rows=b_all[nh:2 * nh])


def _mlstm_recur(st, z_ref, nrm_ref, skip_ref, o_ref, ct_ref, n_ref, m_ref):
    L = MLSTM_CHUNK
    dh = MLSTM_HEAD_DIM
    key = lax.broadcasted_iota(jnp.int32, (L, L), 0)
    qry = lax.broadcasted_iota(jnp.int32, (L, L), 1)
    causal = key <= qry
    xc = st["xc"]

    heads = range(MLSTM_HEADS)
    qts, kss, vts = st["qts"], st["kss"], st["vts"]
    ct_state = [ct_ref[h] for h in heads]
    n_state = [n_ref[h] for h in heads]
    m_prev = [m_ref[h:h + 1, 0:1] for h in heads]
    qk = [jnp.dot(kss[h], qts[h], preferred_element_type=F32) for h in heads]
    cq = [jnp.dot(ct_state[h].astype(BF16), qts[h], preferred_element_type=F32) for h in heads]
    nq = [jnp.dot(n_state[h].astype(BF16), qts[h], preferred_element_type=F32)[0:1] for h in heads]

    li = [st["li_rows"][h:h + 1] for h in heads]
    b = [st["b_rows"][h:h + 1] for h in heads]
    a = [li[h] - b[h] for h in heads]
    a_keys = [jnp.broadcast_to(a[h], (L, L)).T for h in heads]
    d = [jnp.where(causal, b[h] + a_keys[h], -jnp.inf) for h in heads]
    inter = [b[h] + m_prev[h] for h in heads]
    m_t = [jnp.maximum(inter[h], jnp.max(d[h], axis=0, keepdims=True)) for h in heads]
    w_inter = [jnp.exp(inter[h] - m_t[h]) for h in heads]
    s = [qk[h] * jnp.exp(d[h] - m_t[h]) for h in heads]
    sv = [jnp.dot(vts[h], s[h].astype(BF16), preferred_element_type=F32) for h in heads]

    b_last = [b[h][:, L - 1:L] for h in heads]
    g = [b_last[h] + a[h] for h in heads]
    m_new = [jnp.maximum(b_last[h] + m_prev[h], jnp.max(g[h], axis=1, keepdims=True)) for h in heads]
    wk = [jnp.exp(g[h] - m_new[h]) for h in heads]
    dec = [jnp.exp(b_last[h] + m_prev[h] - m_new[h]) for h in heads]
    for h in heads:
        ct_ref[h] = dec[h] * ct_state[h] + jnp.dot((vts[h].astype(F32) * wk[h]).astype(BF16), kss[h],
                                                   preferred_element_type=F32)
        n_ref[h] = dec[h] * n_state[h] + jnp.dot(jnp.broadcast_to(wk[h], (SUBLANES, L)).astype(BF16),
                                                 kss[h], preferred_element_type=F32)
        m_ref[h:h + 1, :] = jnp.broadcast_to(m_new[h], (1, LANES))

    for h in heads:
        sl = slice(h * dh, (h + 1) * dh)
        num = w_inter[h] * cq[h] + sv[h]
        den = w_inter[h] * nq[h] + jnp.sum(s[h], axis=0, keepdims=True)
        hh = num * (1.0 / jnp.maximum(jnp.abs(den), jnp.exp(-m_t[h])))
        mu = jnp.mean(hh, axis=0, keepdims=True)
        cen = hh - mu
        var = jnp.mean(cen * cen, axis=0, keepdims=True)
        hn = (cen * lax.rsqrt(var + LN_EPS)).T * nrm_ref[:, sl]
        o_ref[:, sl] = ((hn + skip_ref[:, sl] * xc[:, sl]) * _silu(z_ref[:, sl])).astype(o_ref.dtype)


def _mlstm(mxz, conv_w, conv_b, wq, wk, wv, w_if_t, b_if, norm_w, skip_w, *, nb=2, chunks=4):
    b, s, _ = mxz.shape
    span = chunks * MLSTM_CHUNK
    w = MLSTM_WIDTH
    rows_per_span = span // SUBLANES
    const2 = lambda bi, c: (0, 0)
    const3 = lambda bi, c: (0, 0, 0)
    return pl.pallas_call(
        _mlstm_kernel,
        grid=(b // nb, s // span),
        in_specs=[
            pl.BlockSpec((nb, span, w), lambda bi, c: (bi, c, 0)),
            pl.BlockSpec((nb, SUBLANES, w), lambda bi, c: (bi, jnp.maximum(c * rows_per_span - 1, 0), 0)),
            pl.BlockSpec((nb, span, w), lambda bi, c: (bi, c, 1)),
            pl.BlockSpec(conv_w.shape, const2),
            pl.BlockSpec(conv_b.shape, const2),
            pl.BlockSpec(wq.shape, const3),
            pl.BlockSpec(wk.shape, const3),
            pl.BlockSpec(wv.shape, const3),
            pl.BlockSpec(w_if_t.shape, const2),
            pl.BlockSpec(b_if.shape, const2),
            pl.BlockSpec(norm_w.shape, const2),
            pl.BlockSpec(skip_w.shape, const2),
        ],
        out_specs=pl.BlockSpec((nb, span, w), lambda bi, c: (bi, c, 0)),
        out_shape=jax.ShapeDtypeStruct((b, s, w), BF16),
        scratch_shapes=[
            pltpu.VMEM((nb, SUBLANES + span, w), F32),
            pltpu.VMEM((nb, MLSTM_HEADS, MLSTM_HEAD_DIM, MLSTM_HEAD_DIM), F32),
            pltpu.VMEM((nb, MLSTM_HEADS, SUBLANES, MLSTM_HEAD_DIM), F32),
            pltpu.VMEM((nb, SUBLANES, LANES), F32),
        ],
        compiler_params=_params("parallel", "arbitrary"),
        name="mlstm",
    )(mxz, mxz, mxz, conv_w, conv_b, wq, wk, wv, w_if_t, b_if, norm_w, skip_w)


def _merge_kernel(x_ref, nw_ref, wg_ref, gb_ref, od_ref, of_ref, om_ref, wbd_ref, wbf_ref, wbm_ref,
                  wo_ref, o_ref):
    x = x_ref[...]
    d = x.shape[1]
    xn = _rms_norm(x, nw_ref[...]).astype(BF16)
    merged = None
    for br, (b_ref, w_ref) in enumerate(((od_ref, wbd_ref), (of_ref, wbf_ref), (om_ref, wbm_ref))):
        cols = slice(br * d, (br + 1) * d)
        gate = jax.nn.sigmoid(jnp.dot(xn, wg_ref[:, cols], preferred_element_type=F32) + gb_ref[:, cols])
        term = gate * jnp.dot(b_ref[...], w_ref[...], preferred_element_type=F32)
        merged = term if merged is None else merged + term
    o_ref[...] = x + jnp.dot(merged.astype(BF16), wo_ref[...], preferred_element_type=F32)


def _merge(x, norm_w, w_g, gate_b, o_diff, o_fox, o_mlstm, wbd, wbf, wbm, w_out, *, tm=512):
    m, d = x.shape
    const = lambda i: (0, 0)
    row = lambda i: (i, 0)
    return pl.pallas_call(
        _merge_kernel,
        grid=(m // tm,),
        in_specs=[
            pl.BlockSpec((tm, d), row),
            pl.BlockSpec((1, d), const),
            pl.BlockSpec(w_g.shape, const),
            pl.BlockSpec(gate_b.shape, const),
            pl.BlockSpec((tm, ATTN_WIDTH), row),
            pl.BlockSpec((tm, ATTN_WIDTH), row),
            pl.BlockSpec((tm, MLSTM_WIDTH), row),
            pl.BlockSpec(wbd.shape, const),
            pl.BlockSpec(wbf.shape, const),
            pl.BlockSpec(wbm.shape, const),
            pl.BlockSpec(w_out.shape, const),
        ],
        out_specs=pl.BlockSpec((tm, d), row),
        out_shape=jax.ShapeDtypeStruct((m, d), F32),
        compiler_params=_params("parallel"),
        name="merge",
    )(x, norm_w, w_g, gate_b, o_diff, o_fox, o_mlstm, wbd, wbf, wbm, w_out)


def _alibi_bias(seq):
    slopes = np.array([2.0 ** (-8.0 * (h + 1) / DIFF_HEADS) for h in range(DIFF_HEADS)], dtype=np.float32)
    pos = np.arange(seq, dtype=np.float32)
    per_head = slopes[:, None] * pos[None, :]
    return jnp.asarray(np.repeat(per_head[:, None, :], 2, axis=1))


def kernel(x, ffn1_norm, ffn1_w_gate, ffn1_w_up, ffn1_w_down, mix_norm, w_in, gate_bias, diff_lq1, diff_lk1, diff_lq2, diff_lk2, diff_subln, fox_b_f, mlstm_conv_w, mlstm_conv_b, mlstm_wq, mlstm_wk, mlstm_wv, mlstm_w_if, mlstm_b_if, mlstm_norm, mlstm_skip, w_branch_diff, w_branch_fox, w_branch_mlstm, w_out, ffn2_norm, ffn2_w_gate, ffn2_w_up, ffn2_w_down, final_norm):
    batch, seq, d = x.shape
    depth = w_in.shape[0]
    m = batch * seq
    bf = lambda a: a.astype(BF16)
    row = lambda a: a.reshape(1, -1)
    aw = ATTN_WIDTH
    ff0 = 6 * aw
    mx0 = ff0 + FOX_HEADS
    g0 = mx0 + 2 * MLSTM_WIDTH
    blocks = aw // LANES
    alibi = _alibi_bias(seq)

    x = x.reshape(m, d)
    for l in range(depth):
        x = _ffn(x, row(ffn1_norm[l]), ffn1_w_gate, ffn1_w_up, ffn1_w_down, l)

        w_in_l = w_in[l]
        w_qk = bf(jnp.concatenate([w_in_l[:, 0:2 * aw], w_in_l[:, 3 * aw:5 * aw]], axis=1))
        w_vt = bf(jnp.concatenate([w_in_l[:, 2 * aw:3 * aw], w_in_l[:, 5 * aw:6 * aw]], axis=1).T)
        qk, vt, mxz, cf = _proj(x, row(mix_norm[l]), w_qk, w_vt, bf(w_in_l[:, mx0:g0]),
                                bf(w_in_l[:, ff0:mx0].T), fox_b_f[l].reshape(-1, 1), seq=seq)
        qk = qk.reshape(batch, seq, 4 * aw)
        lam_init = 0.8 - 0.6 * math.exp(-0.3 * l)
        o_diff = _attention(
            functools.partial(_diff_attn_kernel, lam_init=lam_init), qk, vt, alibi,
            [row(diff_lq1[l]), row(diff_lk1[l]), row(diff_lq2[l]), row(diff_lk2[l]), row(diff_subln[l])],
            q_col=0, v_row=0, v_rows=LANES, bias_per_batch=False, name="diff_attn")
        o_fox = _attention(_fox_attn_kernel, qk, vt, cf.reshape(FOX_HEADS // 2, 2, m), [],
                           q_col=2 * blocks, v_row=blocks, v_rows=LANES // 2, bias_per_batch=True,
                           name="fox_attn")
        o_mlstm = _mlstm(mxz.reshape(batch, seq, 2 * MLSTM_WIDTH), mlstm_conv_w[l], row(mlstm_conv_b[l]),
                         bf(jnp.swapaxes(mlstm_wq[l], 1, 2)), bf(mlstm_wk[l]),
                         bf(jnp.swapaxes(mlstm_wv[l], 1, 2)), bf(mlstm_w_if[l].T),
                         mlstm_b_if[l].reshape(-1, 1), row(mlstm_norm[l]), row(mlstm_skip[l]))
        x = _merge(x, row(mix_norm[l]), bf(w_in_l[:, g0:]), gate_bias[l].reshape(1, -1),
                   o_diff.reshape(m, aw), o_fox.reshape(m, aw),
                   o_mlstm.reshape(m, MLSTM_WIDTH), bf(w_branch_diff[l]), bf(w_branch_fox[l]),
                   bf(w_branch_mlstm[l]), bf(w_out[l]))

        last = l == depth - 1
        x = _ffn(x, row(ffn2_norm[l]), ffn2_w_gate, ffn2_w_up, ffn2_w_down, l,
                 row(final_norm) if last else None)
    return x.reshape(batch, seq, d)
```

```python
import functools
import math

import numpy as np
import jax
import jax.numpy as jnp
from jax import lax
from jax.experimental import pallas as pl
from jax.experimental.pallas import tpu as pltpu

D_MODEL = 1024
D_FF = 2816
RMS_EPS = 1e-6
LN_EPS = 1e-5
N_BRANCH = 3
DIFF_HEADS = 4
ATTN_HEAD_DIM = 64
FOX_HEADS = 8
MLSTM_HEADS = 4
MLSTM_HEAD_DIM = 128
MLSTM_WIDTH = MLSTM_HEADS * MLSTM_HEAD_DIM
MLSTM_CONV = 4
MLSTM_CHUNK = 128
MLSTM_STAGE_AHEAD = 2
ATTN_WIDTH = 512
LANES = 128
SUBLANES = 8
BF16_ROWS = 16
SCORE_SLOTS = 4
FFN_WEIGHT_SLOTS = 3
FFN_ROW_BLOCK = 512
BIAS_TERMS = 3
VMEM_LIMIT_BYTES = 56 * 1024 * 1024
LOG2E = math.log2(math.e)
Q_SCALE = ATTN_HEAD_DIM ** -0.5 * LOG2E

F32 = jnp.float32
BF16 = jnp.bfloat16
NT_DIMS = (((1,), (1,)), ((), ()))
TN_DIMS = (((0,), (0,)), ((), ()))


def _params(*semantics):
    return pltpu.CompilerParams(dimension_semantics=semantics, vmem_limit_bytes=VMEM_LIMIT_BYTES)


def _rms_norm(x, w):
    return x * lax.rsqrt(jnp.mean(x * x, axis=-1, keepdims=True) + RMS_EPS) * w


def _silu(x):
    return x * jax.nn.sigmoid(x)


def _log_sigmoid(x):
    return jnp.minimum(x, 0.0) - jnp.log1p(jnp.exp(-jnp.abs(x)))


def _lane_cumsum(x):
    n = x.shape[-1]
    lane = lax.broadcasted_iota(jnp.int32, x.shape, 1)
    shift = 1
    while shift < n:
        x = x + jnp.where(lane >= shift, pltpu.roll(x, shift, axis=1), 0.0)
        shift *= 2
    return x


def _ffn_kernel(*refs, final, layer, tf):
    if final:
        x_ref, nw_ref, wg_hbm, wu_hbm, wd_hbm, fn_ref, o_ref, wg_buf, wu_buf, wd_buf, sem = refs
    else:
        x_ref, nw_ref, wg_hbm, wu_hbm, wd_hbm, o_ref, wg_buf, wu_buf, wd_buf, sem = refs
    nf = wg_hbm.shape[2] // tf
    slots = wg_buf.shape[0]

    def copies(f):
        slot = f % slots
        cols = pl.ds(f * tf, tf)
        return (pltpu.make_async_copy(wg_hbm.at[layer, :, cols], wg_buf.at[slot], sem.at[0, slot]),
                pltpu.make_async_copy(wu_hbm.at[layer, :, cols], wu_buf.at[slot], sem.at[1, slot]),
                pltpu.make_async_copy(wd_hbm.at[layer, cols, :], wd_buf.at[slot], sem.at[2, slot]))

    for f in range(min(slots - 1, nf)):
        for cp in copies(f):
            cp.start()

    tm = x_ref.shape[0]
    rbs = [slice(r, r + FFN_ROW_BLOCK) for r in range(0, tm, FFN_ROW_BLOCK)]
    xn = [None] * len(rbs)
    for f in range(nf):
        if f + slots - 1 < nf:
            for cp in copies(f + slots - 1):
                cp.start()
        for cp in copies(f):
            cp.wait()
        slot = f % slots
        wg = wg_buf[slot].astype(BF16)
        wu = wu_buf[slot].astype(BF16)
        wd = wd_buf[slot].astype(BF16)
        for r, rb in enumerate(rbs):
            if f == 0:
                xn[r] = _rms_norm(x_ref[rb, :], nw_ref[...]).astype(BF16)
            g = jnp.dot(xn[r], wg, preferred_element_type=F32)
            u = jnp.dot(xn[r], wu, preferred_element_type=F32)
            h = (_silu(g) * u).astype(BF16)
            part = jnp.dot(h, wd, preferred_element_type=F32)
            if f == 0:
                o_ref[rb, :] = part
            elif f < nf - 1:
                o_ref[rb, :] += part
            else:
                y = x_ref[rb, :] + 0.5 * (o_ref[rb, :] + part)
                if final:
                    y = _rms_norm(y, fn_ref[...])
                o_ref[rb, :] = y


def _ffn(x, norm_w, w_gate, w_up, w_down, layer, final_norm=None, *, tm=2048, tf=256):
    m, d = x.shape
    final = final_norm is not None
    in_specs = [
        pl.BlockSpec((tm, d), lambda i: (i, 0)),
        pl.BlockSpec((1, d), lambda i: (0, 0)),
        pl.BlockSpec(memory_space=pl.ANY),
        pl.BlockSpec(memory_space=pl.ANY),
        pl.BlockSpec(memory_space=pl.ANY),
    ]
    args = [x, norm_w, w_gate, w_up, w_down]
    if final:
        in_specs.append(pl.BlockSpec((1, d), lambda i: (0, 0)))
        args.append(final_norm)
    return pl.pallas_call(
        functools.partial(_ffn_kernel, final=final, layer=layer, tf=tf),
        grid=(m // tm,),
        in_specs=in_specs,
        out_specs=pl.BlockSpec((tm, d), lambda i: (i, 0)),
        out_shape=jax.ShapeDtypeStruct((m, d), F32),
        scratch_shapes=[pltpu.VMEM((FFN_WEIGHT_SLOTS, d, tf), F32), pltpu.VMEM((FFN_WEIGHT_SLOTS, d, tf), F32),
                        pltpu.VMEM((FFN_WEIGHT_SLOTS, tf, d), F32),
                        pltpu.SemaphoreType.DMA((3, FFN_WEIGHT_SLOTS))],
        compiler_params=_params("parallel"),
        name="ffn_final" if final else "ffn",
    )(*args)


def _proj_kernel(x_ref, nw_ref, wqk_ref, wvt_ref, wm_ref, wff_ref, bf_ref,
                 qk_ref, vt_ref, mxz_ref, cf_ref, carry_ref, *, tiles_per_seq):
    i = pl.program_id(0)
    @pl.when(i == 0)
    def _():
        carry_ref[...] = jnp.zeros_like(carry_ref)

    xn = _rms_norm(x_ref[...], nw_ref[...]).astype(BF16)

    ff = lax.dot_general(wff_ref[...], xn, NT_DIMS, preferred_element_type=F32)
    csum = _lane_cumsum(_log_sigmoid(ff + bf_ref[...]))
    csum = csum + jnp.where(i % tiles_per_seq == 0, 0.0, carry_ref[:, 0:1])
    carry_ref[...] = jnp.broadcast_to(csum[:, -1:], carry_ref.shape)
    cf_ref[...] = -csum

    for grp in range(4):
        cols = slice(grp * ATTN_WIDTH, (grp + 1) * ATTN_WIDTH)
        r = jnp.dot(xn, wqk_ref[:, cols], preferred_element_type=F32)
        if grp % 2 == 0:
            r = r * Q_SCALE
        qk_ref[:, cols] = r.astype(BF16)
    for grp in range(2):
        rows = slice(grp * ATTN_WIDTH, (grp + 1) * ATTN_WIDTH)
        vt_ref[rows, :] = lax.dot_general(wvt_ref[rows, :], xn, NT_DIMS,
                                          preferred_element_type=F32).astype(BF16)
    for grp in range(2):
        cols = slice(grp * MLSTM_WIDTH, (grp + 1) * MLSTM_WIDTH)
        mxz_ref[:, cols] = jnp.dot(xn, wm_ref[:, cols], preferred_element_type=F32)


def _proj(x, norm_w, w_qk, w_vt, w_m, w_ff_t, b_f, *, seq, tm=512):
    m, d = x.shape
    nh = w_ff_t.shape[0]
    const = lambda i: (0, 0)
    return pl.pallas_call(
        functools.partial(_proj_kernel, tiles_per_seq=seq // tm),
        grid=(m // tm,),
        in_specs=[
            pl.BlockSpec((tm, d), lambda i: (i, 0)),
            pl.BlockSpec((1, d), const),
            pl.BlockSpec(w_qk.shape, const),
            pl.BlockSpec(w_vt.shape, const),
            pl.BlockSpec(w_m.shape, const),
            pl.BlockSpec(w_ff_t.shape, const),
            pl.BlockSpec((nh, 1), const),
        ],
        out_specs=[
            pl.BlockSpec((tm, w_qk.shape[1]), lambda i: (i, 0)),
            pl.BlockSpec((w_vt.shape[0], tm), lambda i: (0, i)),
            pl.BlockSpec((tm, w_m.shape[1]), lambda i: (i, 0)),
            pl.BlockSpec((nh, tm), lambda i: (0, i)),
        ],
        out_shape=[
            jax.ShapeDtypeStruct((m, w_qk.shape[1]), BF16),
            jax.ShapeDtypeStruct((w_vt.shape[0], m), BF16),
            jax.ShapeDtypeStruct((m, w_m.shape[1]), F32),
            jax.ShapeDtypeStruct((nh, m), F32),
        ],
        scratch_shapes=[pltpu.VMEM((nh, LANES), F32)],
        compiler_params=_params("arbitrary"),
        name="proj",
    )(x, norm_w, w_qk, w_vt, w_m, w_ff_t, b_f)


def _flash_pair(q_ref, k_ref, vt_ref, bias_ref, kaug_ref, vaug_ref, qm_ref, s_ref, acc_ref, finish, *, tq, tk, v_rows):
    seq = k_ref.shape[1]
    dr = v_rows[0].stop - v_rows[0].start
    half = LANES // 2

    kaug_ref[:, 0:LANES] = k_ref[0]
    terms = []
    for c in range(2):
        rem = bias_ref[0, c:c + 1, :] * LOG2E
        for _ in range(BIAS_TERMS):
            part = rem.astype(BF16).astype(F32)
            terms.append(part)
            rem = rem - part
    term_rows = jnp.concatenate(terms + [jnp.zeros((LANES - 2 * BIAS_TERMS, seq), F32)], axis=0)
    for blk in range(seq // LANES):
        sl = slice(blk * LANES, (blk + 1) * LANES)
        kaug_ref[sl, LANES:] = term_rows[:, sl].T.astype(BF16)
    for c in range(2):
        vaug_ref[c, 0:dr, :] = vt_ref[v_rows[c], :]
        vaug_ref[c, dr:, :] = jnp.ones((BF16_ROWS, seq), BF16)

    qlane = lax.broadcasted_iota(jnp.int32, (seq, LANES), 1)
    q = q_ref[0]
    for c in range(2):
        qm_ref[c, :, 0:LANES] = jnp.where((qlane >= c * half) & (qlane < (c + 1) * half), q, jnp.zeros_like(q))
        picks_bias = (qlane >= BIAS_TERMS * c) & (qlane < BIAS_TERMS * (c + 1))
        qm_ref[c, :, LANES:] = jnp.where(picks_bias, 1.0, 0.0).astype(BF16)

    work = []
    for i in range(seq // tq):
        n_full = (i * tq) // tk
        lead = i * tq - n_full * tk
        work += [(i, j * tk, tk, None, j == 0, False) for j in range(n_full)]
        work.append((i, n_full * tk, lead + tq, lead, n_full == 0, True))

    def scores(n):
        i, k0, size, _, _, _ = work[n]
        for c in range(2):
            s_ref[n % s_ref.shape[0], c, 0:size, :] = lax.dot_general(
                kaug_ref[k0:k0 + size, :], qm_ref[c, i * tq:(i + 1) * tq, :], NT_DIMS,
                preferred_element_type=F32)

    scores(0)
    scores(1)
    m = [None, None]
    for n, (i, k0, size, lead, first, last) in enumerate(work):
        if n + 2 < len(work):
            scores(n + 2)
        for c in range(2):
            s = s_ref[n % s_ref.shape[0], c, 0:size, :]
            if lead is not None:
                key = lax.broadcasted_iota(jnp.int32, (tq, tq), 0)
                qry = lax.broadcasted_iota(jnp.int32, (tq, tq), 1)
                diag = jnp.where(key <= qry, s[lead:, :], -jnp.inf)
                s = diag if lead == 0 else jnp.concatenate([s[0:lead, :], diag], axis=0)
            m_blk = jnp.max(s, axis=0, keepdims=True)
            m_new = m_blk if first else jnp.maximum(m[c], m_blk)
            p = jnp.exp2(s - m_new).astype(BF16)
            pv = jnp.dot(vaug_ref[c, :, k0:k0 + size], p, preferred_element_type=F32)
            if first:
                acc_ref[i, c] = pv
            else:
                acc_ref[i, c] = jnp.exp2(m[c] - m_new) * acc_ref[i, c] + pv
            m[c] = m_new
        if last:
            finish(i, *[acc_ref[i, c, 0:dr, :] * (1.0 / acc_ref[i, c, dr:dr + 1, :]) for c in range(2)])


def _diff_attn_kernel(q_ref, k_ref, vt_ref, bias_ref, lq1_ref, lk1_ref, lq2_ref, lk2_ref, sub_ref,
                      o_ref, *scratch, tq, tk, lam_init):
    lam = (jnp.exp(jnp.sum(lq1_ref[...] * lk1_ref[...], axis=1, keepdims=True))
           - jnp.exp(jnp.sum(lq2_ref[...] * lk2_ref[...], axis=1, keepdims=True)) + lam_init)

    def finish(i, a1, a2):
        d = (a1 - lam * a2).T
        o_ref[0, i * tq:(i + 1) * tq, :] = (_rms_norm(d, sub_ref[...]) * (1.0 - lam_init)).astype(o_ref.dtype)

    full = slice(0, LANES)
    _flash_pair(q_ref, k_ref, vt_ref, bias_ref, *scratch, finish, tq=tq, tk=tk, v_rows=(full, full))


def _fox_attn_kernel(q_ref, k_ref, vt_ref, bias_ref, o_ref, *scratch, tq, tk):
    def finish(i, a_even, a_odd):
        o_ref[0, i * tq:(i + 1) * tq, :] = jnp.concatenate([a_even, a_odd], axis=0).T.astype(o_ref.dtype)

    half = LANES // 2
    _flash_pair(q_ref, k_ref, vt_ref, bias_ref, *scratch, finish, tq=tq, tk=tk,
                v_rows=(slice(0, half), slice(half, LANES)))


def _attention(kernel, qk, vt, bias, extra, *, q_col, v_row, v_rows, bias_per_batch, tq=256, tk=1024, name):
    b, s, _ = qk.shape
    groups = ATTN_WIDTH // LANES
    if bias_per_batch:
        bias_map = lambda bi, g: (g, 0, bi)
    else:
        bias_map = lambda bi, g: (g, 0, 0)
    in_specs = [
        pl.BlockSpec((1, s, LANES), lambda bi, g: (bi, 0, q_col + g)),
        pl.BlockSpec((1, s, LANES), lambda bi, g: (bi, 0, q_col + groups + g)),
        pl.BlockSpec((LANES, s), lambda bi, g: (v_row + g, bi)),
        pl.BlockSpec((1, 2, s), bias_map),
    ] + [pl.BlockSpec(e.shape, lambda bi, g: (0, 0)) for e in extra]
    rows = v_rows + BF16_ROWS
    return pl.pallas_call(
        functools.partial(kernel, tq=tq, tk=tk),
        grid=(b, groups),
        in_specs=in_specs,
        out_specs=pl.BlockSpec((1, s, LANES), lambda bi, g: (bi, 0, g)),
        out_shape=jax.ShapeDtypeStruct((b, s, ATTN_WIDTH), BF16),
        scratch_shapes=[pltpu.VMEM((s, 2 * LANES), BF16), pltpu.VMEM((2, rows, s), BF16),
                        pltpu.VMEM((2, s, 2 * LANES), BF16), pltpu.VMEM((SCORE_SLOTS, 2, tk, tq), F32),
                        pltpu.VMEM((s // tq, 2, rows, tq), F32)],
        compiler_params=_params("parallel", "parallel"),
        name=name,
    )(qk, qk, vt, bias, *extra)


def _mlstm_kernel(cur_ref, prev_ref, z_ref, cw_ref, cb_ref, wqt_ref, wk_ref, wvt_ref, wif_ref, bif_ref,
                  nrm_ref, skip_ref, o_ref, xbuf_ref, ct_ref, n_ref, m_ref):
    ci = pl.program_id(1)
    nb, span, _ = cur_ref.shape
    chunks = span // MLSTM_CHUNK

    @pl.when(ci == 0)
    def _():
        ct_ref[...] = jnp.zeros_like(ct_ref)
        n_ref[...] = jnp.zeros_like(n_ref)
        m_ref[...] = jnp.zeros_like(m_ref)

    for r in range(nb):
        xbuf_ref[r, 0:SUBLANES, :] = jnp.where(ci > 0, prev_ref[r], 0.0)
        xbuf_ref[r, SUBLANES:, :] = cur_ref[r]

    work = [(r, k) for k in range(chunks) for r in range(nb)]
    refs = (cw_ref, cb_ref, wqt_ref, wk_ref, wvt_ref, wif_ref, bif_ref)
    ahead = min(MLSTM_STAGE_AHEAD, len(work))
    staged = {n: _mlstm_gates(xbuf_ref.at[work[n][0]], work[n][1], *refs) for n in range(ahead)}
    for n, (r, k) in enumerate(work):
        if n + ahead < len(work):
            staged[n + ahead] = _mlstm_gates(xbuf_ref.at[work[n + ahead][0]], work[n + ahead][1], *refs)
        rows = slice(k * MLSTM_CHUNK, (k + 1) * MLSTM_CHUNK)
        _mlstm_recur(staged.pop(n), z_ref.at[r, rows], nrm_ref, skip_ref, o_ref.at[r, rows],
                     ct_ref.at[r], n_ref.at[r], m_ref.at[r])


def _mlstm_gates(xbuf_ref, k, cw_ref, cb_ref, wqt_ref, wk_ref, wvt_ref, wif_ref, bif_ref):
    L = MLSTM_CHUNK
    dh = MLSTM_HEAD_DIM
    nh = MLSTM_HEADS

    base = SUBLANES + k * L
    cur = xbuf_ref[base:base + L, :]
    conv = cb_ref[...]
    for j in range(MLSTM_CONV):
        off = base - (MLSTM_CONV - 1) + j
        conv = conv + cw_ref[j:j + 1, :] * xbuf_ref[off:off + L, :]
    xc = _silu(conv)

    qts, kss, vts = [], [], []
    ift = bif_ref[...]
    for h in range(nh):
        sl = slice(h * dh, (h + 1) * dh)
        xh = xc[:, sl].astype(BF16)
        vh = cur[:, sl].astype(BF16)
        qt = lax.dot_general(wqt_ref[h], xh, NT_DIMS, preferred_element_type=F32).astype(BF16)
        kk = jnp.dot(xh, wk_ref[h], preferred_element_type=F32)
        vt = lax.dot_general(wvt_ref[h], vh, NT_DIMS, preferred_element_type=F32).astype(BF16)
        ift = ift + jnp.dot(wif_ref[:, sl], qt, preferred_element_type=F32)
        ift = ift + lax.dot_general(wif_ref[:, MLSTM_WIDTH + h * dh:MLSTM_WIDTH + (h + 1) * dh],
                                    kk.astype(BF16), NT_DIMS, preferred_element_type=F32)
        ift = ift + jnp.dot(wif_ref[:, 2 * MLSTM_WIDTH + h * dh:2 * MLSTM_WIDTH + (h + 1) * dh], vt,
                            preferred_element_type=F32)
        qts.append(qt)
        kss.append((kk * (dh ** -0.5)).astype(BF16))
        vts.append(vt)

    key = lax.broadcasted_iota(jnp.int32, (L, L), 0)
    qry = lax.broadcasted_iota(jnp.int32, (L, L), 1)
    lf = _log_sigmoid(ift)
    tri = jnp.where(key <= qry, 1.0, 0.0).astype(BF16)
    b_all = jnp.zeros_like(lf)
    rem = lf
    for _ in range(3):
        part = rem.astype(BF16)
        b_all = b_all + jnp.dot(part, tri, preferred_element_type=F32)
        rem = rem - part.astype(F32)
    return dict(xc=xc, qts=qts, kss=kss, vts=vts, li_rows=ift[0:nh], b_rows=b_all[nh:2 * nh])


def _mlstm_recur(st, z_ref, nrm_ref, skip_ref, o_ref, ct_ref, n_ref, m_ref):
    L = MLSTM_CHUNK
    dh = MLSTM_HEAD_DIM
    key = lax.broadcasted_iota(jnp.int32, (L, L), 0)
    qry = lax.broadcasted_iota(jnp.int32, (L, L), 1)
    causal = key <= qry
    xc = st["xc"]

    heads = range(MLSTM_HEADS)
    qts, kss, vts = st["qts"], st["kss"], st["vts"]
    ct_state = [ct_ref[h] for h in heads]
    n_state = [n_ref[h] for h in heads]
    m_prev = [m_ref[h:h + 1, 0:1] for h in heads]
    qk = [jnp.dot(kss[h], qts[h], preferred_element_type=F32) for h in heads]
    cq = [jnp.dot(ct_state[h].astype(BF16), qts[h], preferred_element_type=F32) for h in heads]
    nq = [jnp.dot(n_state[h].astype(BF16), qts[h], preferred_element_type=F32)[0:1] for h in heads]

    li = [st["li_rows"][h:h + 1] for h in heads]
    b = [st["b_rows"][h:h + 1] for h in heads]
    a = [li[h] - b[h] for h in heads]
    a_keys = [jnp.broadcast_to(a[h], (L, L)).T for h in heads]
    d = [jnp.where(causal, b[h] + a_keys[h], -jnp.inf) for h in heads]
    inter = [b[h] + m_prev[h] for h in heads]
    m_t = [jnp.maximum(inter[h], jnp.max(d[h], axis=0, keepdims=True)) for h in heads]
    w_inter = [jnp.exp(inter[h] - m_t[h]) for h in heads]
    s = [qk[h] * jnp.exp(d[h] - m_t[h]) for h in heads]
    sv = [jnp.dot(vts[h], s[h].astype(BF16), preferred_element_type=F32) for h in heads]

    b_last = [b[h][:, L - 1:L] for h in heads]
    g = [b_last[h] + a[h] for h in heads]
    m_new = [jnp.maximum(b_last[h] + m_prev[h], jnp.max(g[h], axis=1, keepdims=True)) for h in heads]
    wk = [jnp.exp(g[h] - m_new[h]) for h in heads]
    dec = [jnp.exp(b_last[h] + m_prev[h] - m_new[h]) for h in heads]
    for h in heads:
        ct_ref[h] = dec[h] * ct_state[h] + jnp.dot((vts[h].astype(F32) * wk[h]).astype(BF16), kss[h],
                                                   preferred_element_type=F32)
        n_ref[h] = dec[h] * n_state[h] + jnp.dot(jnp.broadcast_to(wk[h], (SUBLANES, L)).astype(BF16),
                                                 kss[h], preferred_element_type=F32)
        m_ref[h:h + 1, :] = jnp.broadcast_to(m_new[h], (1, LANES))

    for h in heads:
        sl = slice(h * dh, (h + 1) * dh)
        num = w_inter[h] * cq[h] + sv[h]
        den = w_inter[h] * nq[h] + jnp.sum(s[h], axis=0, keepdims=True)
        hh = num * (1.0 / jnp.maximum(jnp.abs(den), jnp.exp(-m_t[h])))
        mu = jnp.mean(hh, axis=0, keepdims=True)
        cen = hh - mu
        var = jnp.mean(cen * cen, axis=0, keepdims=True)
        hn = (cen * lax.rsqrt(var + LN_EPS)).T * nrm_ref[:, sl]
        o_ref[:, sl] = ((hn + skip_ref[:, sl] * xc[:, sl]) * _silu(z_ref[:, sl])).astype(o_ref.dtype)


def _mlstm(mxz, conv_w, conv_b, wq, wk, wv, w_if_t, b_if, norm_w, skip_w, *, nb=2, chunks=4):
    b, s, _ = mxz.shape
    span = chunks * MLSTM_CHUNK
    w = MLSTM_WIDTH
    rows_per_span = span // SUBLANES
    const2 = lambda bi, c: (0, 0)
    const3 = lambda bi, c: (0, 0, 0)
    return pl.pallas_call(
        _mlstm_kernel,
        grid=(b // nb, s // span),
        in_specs=[
            pl.BlockSpec((nb, span, w), lambda bi, c: (bi, c, 0)),
            pl.BlockSpec((nb, SUBLANES, w), lambda bi, c: (bi, jnp.maximum(c * rows_per_span - 1, 0), 0)),
            pl.BlockSpec((nb, span, w), lambda bi, c: (bi, c, 1)),
            pl.BlockSpec(conv_w.shape, const2),
            pl.BlockSpec(conv_b.shape, const2),
            pl.BlockSpec(wq.shape, const3),
            pl.BlockSpec(wk.shape, const3),
            pl.BlockSpec(wv.shape, const3),
            pl.BlockSpec(w_if_t.shape, const2),
            pl.BlockSpec(b_if.shape, const2),
            pl.BlockSpec(norm_w.shape, const2),
            pl.BlockSpec(skip_w.shape, const2),
        ],
        out_specs=pl.BlockSpec((nb, span, w), lambda bi, c: (bi, c, 0)),
        out_shape=jax.ShapeDtypeStruct((b, s, w), BF16),
        scratch_shapes=[
            pltpu.VMEM((nb, SUBLANES + span, w), F32),
            pltpu.VMEM((nb, MLSTM_HEADS, MLSTM_HEAD_DIM, MLSTM_HEAD_DIM), F32),
            pltpu.VMEM((nb, MLSTM_HEADS, SUBLANES, MLSTM_HEAD_DIM), F32),
            pltpu.VMEM((nb, SUBLANES, LANES), F32),
        ],
        compiler_params=_params("parallel", "arbitrary"),
        name="mlstm",
    )(mxz, mxz, mxz, conv_w, conv_b, wq, wk, wv, w_if_t, b_if, norm_w, skip_w)


def _merge_kernel(x_ref, nw_ref, wg_ref, gb_ref, od_ref, of_ref, om_ref, wbd_ref, wbf_ref, wbm_ref,
                  wo_ref, o_ref):
    x = x_ref[...]
    d = x.shape[1]
    xn = _rms_norm(x, nw_ref[...]).astype(BF16)
    merged = None
    for br, (b_ref, w_ref) in enumerate(((od_ref, wbd_ref), (of_ref, wbf_ref), (om_ref, wbm_ref))):
        cols = slice(br * d, (br + 1) * d)
        gate = jax.nn.sigmoid(jnp.dot(xn, wg_ref[:, cols], preferred_element_type=F32) + gb_ref[:, cols])
        term = gate * jnp.dot(b_ref[...], w_ref[...], preferred_element_type=F32)
        merged = term if merged is None else merged + term
    o_ref[...] = x + jnp.dot(merged.astype(BF16), wo_ref[...], preferred_element_type=F32)


def _merge(x, norm_w, w_g, gate_b, o_diff, o_fox, o_mlstm, wbd, wbf, wbm, w_out, *, tm=512):
    m, d = x.shape
    const = lambda i: (0, 0)
    row = lambda i: (i, 0)
    return pl.pallas_call(
        _merge_kernel,
        grid=(m // tm,),
        in_specs=[
            pl.BlockSpec((tm, d), row),
            pl.BlockSpec((1, d), const),
            pl.BlockSpec(w_g.shape, const),
            pl.BlockSpec(gate_b.shape, const),
            pl.BlockSpec((tm, ATTN_WIDTH), row),
            pl.BlockSpec((tm, ATTN_WIDTH), row),
            pl.BlockSpec((tm, MLSTM_WIDTH), row),
            pl.BlockSpec(wbd.shape, const),
            pl.BlockSpec(wbf.shape, const),
            pl.BlockSpec(wbm.shape, const),
            pl.BlockSpec(w_out.shape, const),
        ],
        out_specs=pl.BlockSpec((tm, d), row),
        out_shape=jax.ShapeDtypeStruct((m, d), F32),
        compiler_params=_params("parallel"),
        name="merge",
    )(x, norm_w, w_g, gate_b, o_diff, o_fox, o_mlstm, wbd, wbf, wbm, w_out)


def _alibi_bias(seq):
    slopes = np.array([2.0 ** (-8.0 * (h + 1) / DIFF_HEADS) for h in range(DIFF_HEADS)], dtype=np.float32)
    pos = np.arange(seq, dtype=np.float32)
    per_head = slopes[:, None] * pos[None, :]
    return jnp.asarray(np.repeat(per_head[:, None, :], 2, axis=1))


def kernel(x, ffn1_norm, ffn1_w_gate, ffn1_w_up, ffn1_w_down, mix_norm, w_in, gate_bias, diff_lq1, diff_lk1, diff_lq2, diff_lk2, diff_subln, fox_b_f, mlstm_conv_w, mlstm_conv_b, mlstm_wq, mlstm_wk, mlstm_wv, mlstm_w_if, mlstm_b_if, mlstm_norm, mlstm_skip, w_branch_diff, w_branch_fox, w_branch_mlstm, w_out, ffn2_norm, ffn2_w_gate, ffn2_w_up, ffn2_w_down, final_norm):
    batch, seq, d = x.shape
    depth = w_in.shape[0]
    m = batch * seq
    bf = lambda a: a.astype(BF16)
    row = lambda a: a.reshape(1, -1)
    aw = ATTN_WIDTH
    ff0 = 6 * aw
    mx0 = ff0 + FOX_HEADS
    g0 = mx0 + 2 * MLSTM_WIDTH
    blocks = aw // LANES
    alibi = _alibi_bias(seq)

    x = x.reshape(m, d)
    for l in range(depth):
        x = _ffn(x, row(ffn1_norm[l]), ffn1_w_gate, ffn1_w_up, ffn1_w_down, l)

        w_in_l = w_in[l]
        w_qk = bf(jnp.concatenate([w_in_l[:, 0:2 * aw], w_in_l[:, 3 * aw:5 * aw]], axis=1))
        w_vt = bf(jnp.concatenate([w_in_l[:, 2 * aw:3 * aw], w_in_l[:, 5 * aw:6 * aw]], axis=1).T)
        qk, vt, mxz, cf = _proj(x, row(mix_norm[l]), w_qk, w_vt, bf(w_in_l[:, mx0:g0]),
                                bf(w_in_l[:, ff0:mx0].T), fox_b_f[l].reshape(-1, 1), seq=seq)
        qk = qk.reshape(batch, seq, 4 * aw)
        lam_init = 0.8 - 0.6 * math.exp(-0.3 * l)
        o_diff = _attention(
            functools.partial(_diff_attn_kernel, lam_init=lam_init), qk, vt, alibi,
            [row(diff_lq1[l]), row(diff_lk1[l]), row(diff_lq2[l]), row(diff_lk2[l]), row(diff_subln[l])],
            q_col=0, v_row=0, v_rows=LANES, bias_per_batch=False, name="diff_attn")
        o_fox = _attention(_fox_attn_kernel, qk, vt, cf.reshape(FOX_HEADS // 2, 2, m), [],
                           q_col=2 * blocks, v_row=blocks, v_rows=LANES // 2, bias_per_batch=True,
                           name="fox_attn")
        o_mlstm = _mlstm(mxz.reshape(batch, seq, 2 * MLSTM_WIDTH), mlstm_conv_w[l], row(mlstm_conv_b[l]),
                         bf(jnp.swapaxes(mlstm_wq[l], 1, 2)), bf(mlstm_wk[l]),
                         bf(jnp.swapaxes(mlstm_wv[l], 1, 2)), bf(mlstm_w_if[l].T),
                         mlstm_b_if[l].reshape(-1, 1), row(mlstm_norm[l]), row(mlstm_skip[l]))
        x = _merge(x, row(mix_norm[l]), bf(w_in_l[:, g0:]), gate_bias[l].reshape(1, -1),
                   o_diff.reshape(m, aw), o_fox.reshape(m, aw),
                   o_mlstm.reshape(m, MLSTM_WIDTH), bf(w_branch_diff[l]), bf(w_branch_fox[l]),
                   bf(w_branch_mlstm[l]), bf(w_out[l]))

        last = l == depth - 1
        x = _ffn(x, row(ffn2_norm[l]), ffn2_w_gate, ffn2_w_up, ffn2_w_down, l,
                 row(final_norm) if last else None)
    return x.reshape(batch, seq, d)
```

```python
import functools
import math

import numpy as np
import jax
import jax.numpy as jnp
from jax import lax
from jax.experimental import pallas as pl
from jax.experimental.pallas import tpu as pltpu

RMS_EPS = 1e-6
LN_EPS = 1e-5
DIFF_HEADS = 4
ATTN_HEAD_DIM = 64
FOX_HEADS = 8
MLSTM_HEADS = 4
MLSTM_HEAD_DIM = 128
MLSTM_WIDTH = MLSTM_HEADS * MLSTM_HEAD_DIM
MLSTM_CONV = 4
MLSTM_CHUNK = 128
MLSTM_STAGE_AHEAD = 2
ATTN_WIDTH = 512
LANES = 128
SUBLANES = 8
BF16_ROWS = 16
SCORE_SLOTS = 4
BIAS_TERMS = 3
VMEM_LIMIT_BYTES = 56 * 1024 * 1024
LOG2E = math.log2(math.e)
Q_SCALE = ATTN_HEAD_DIM ** -0.5 * LOG2E

F32 = jnp.float32
BF16 = jnp.bfloat16
NT_DIMS = (((1,), (1,)), ((), ()))


def _params(*semantics):
    return pltpu.CompilerParams(dimension_semantics=semantics, vmem_limit_bytes=VMEM_LIMIT_BYTES)


def _rms_norm(x, w):
    return x * lax.rsqrt(jnp.mean(x * x, axis=-1, keepdims=True) + RMS_EPS) * w


def _silu(x):
    return x * jax.nn.sigmoid(x)


def _log_sigmoid(x):
    return jnp.minimum(x, 0.0) - jnp.log1p(jnp.exp(-jnp.abs(x)))


def _lane_cumsum(x):
    n = x.shape[-1]
    lane = lax.broadcasted_iota(jnp.int32, x.shape, 1)
    shift = 1
    while shift < n:
        x = x + jnp.where(lane >= shift, pltpu.roll(x, shift, axis=1), 0.0)
        shift *= 2
    return x


def _ffn_kernel(*refs, final):
    if final:
        x_ref, nw_ref, wg_ref, wu_ref, wd_ref, fn_ref, o_ref, xn_ref = refs
    else:
        x_ref, nw_ref, wg_ref, wu_ref, wd_ref, o_ref, xn_ref = refs
    f = pl.program_id(1)

    @pl.when(f == 0)
    def _():
        xn_ref[...] = _rms_norm(x_ref[...], nw_ref[...]).astype(BF16)
        o_ref[...] = jnp.zeros_like(o_ref)

    xn = xn_ref[...]
    g = jnp.dot(xn, wg_ref[...].astype(BF16), preferred_element_type=F32)
    u = jnp.dot(xn, wu_ref[...].astype(BF16), preferred_element_type=F32)
    h = (_silu(g) * u).astype(BF16)
    o_ref[...] += jnp.dot(h, wd_ref[...].astype(BF16), preferred_element_type=F32)

    @pl.when(f == pl.num_programs(1) - 1)
    def _():
        y = x_ref[...] + 0.5 * o_ref[...]
        if final:
            y = _rms_norm(y, fn_ref[...])
        o_ref[...] = y


def _ffn(x, norm_w, w_gate, w_up, w_down, layer, final_norm=None, *, tm=2048, tf=256):
    m, d = x.shape
    dff = w_gate.shape[2]
    assert m % tm == 0 and dff % tf == 0, (m, dff)
    final = final_norm is not None
    in_specs = [
        pl.BlockSpec((tm, d), lambda i, f: (i, 0)),
        pl.BlockSpec((1, d), lambda i, f: (0, 0)),
        pl.BlockSpec((None, d, tf), lambda i, f: (layer, 0, f)),
        pl.BlockSpec((None, d, tf), lambda i, f: (layer, 0, f)),
        pl.BlockSpec((None, tf, d), lambda i, f: (layer, f, 0)),
    ]
    args = [x, norm_w, w_gate, w_up, w_down]
    if final:
        in_specs.append(pl.BlockSpec((1, d), lambda i, f: (0, 0)))
        args.append(final_norm)
    return pl.pallas_call(
        functools.partial(_ffn_kernel, final=final),
        grid=(m // tm, dff // tf),
        in_specs=in_specs,
        out_specs=pl.BlockSpec((tm, d), lambda i, f: (i, 0)),
        out_shape=jax.ShapeDtypeStruct((m, d), F32),
        scratch_shapes=[pltpu.VMEM((tm, d), BF16)],
        compiler_params=_params("parallel", "arbitrary"),
        name="ffn_final" if final else "ffn",
    )(*args)


def _proj_kernel(x_ref, nw_ref, wqk_ref, wvt_ref, wm_ref, wff_ref, bf_ref,
                 qk_ref, vt_ref, mxz_ref, cf_ref, carry_ref, *, tiles_per_seq):
    i = pl.program_id(0)

    @pl.when(i == 0)
    def _():
        carry_ref[...] = jnp.zeros_like(carry_ref)

    xn = _rms_norm(x_ref[...], nw_ref[...]).astype(BF16)

    ff = lax.dot_general(wff_ref[...], xn, NT_DIMS, preferred_element_type=F32)
    csum = _lane_cumsum(_log_sigmoid(ff + bf_ref[...]))
    csum = csum + jnp.where(i % tiles_per_seq == 0, 0.0, carry_ref[:, 0:1])
    carry_ref[...] = jnp.broadcast_to(csum[:, -1:], carry_ref.shape)
    cf_ref[...] = -csum

    for grp in range(4):
        cols = slice(grp * ATTN_WIDTH, (grp + 1) * ATTN_WIDTH)
        r = jnp.dot(xn, wqk_ref[:, cols], preferred_element_type=F32)
        if grp % 2 == 0:
            r = r * Q_SCALE
        qk_ref[:, cols] = r.astype(BF16)
    for grp in range(2):
        rows = slice(grp * ATTN_WIDTH, (grp + 1) * ATTN_WIDTH)
        vt_ref[rows, :] = lax.dot_general(wvt_ref[rows, :], xn, NT_DIMS,
                                          preferred_element_type=F32).astype(BF16)
    for grp in range(2):
        cols = slice(grp * MLSTM_WIDTH, (grp + 1) * MLSTM_WIDTH)
        mxz_ref[:, cols] = jnp.dot(xn, wm_ref[:, cols], preferred_element_type=F32)


def _proj(x, norm_w, w_qk, w_vt, w_m, w_ff_t, b_f, *, seq, tm=512):
    m, d = x.shape
    nh = w_ff_t.shape[0]
    assert seq % tm == 0 and m % seq == 0, (m, seq)
    const = lambda i: (0, 0)
    return pl.pallas_call(
        functools.partial(_proj_kernel, tiles_per_seq=seq // tm),
        grid=(m // tm,),
        in_specs=[
            pl.BlockSpec((tm, d), lambda i: (i, 0)),
            pl.BlockSpec((1, d), const),
            pl.BlockSpec(w_qk.shape, const),
            pl.BlockSpec(w_vt.shape, const),
            pl.BlockSpec(w_m.shape, const),
            pl.BlockSpec(w_ff_t.shape, const),
            pl.BlockSpec((nh, 1), const),
        ],
        out_specs=[
            pl.BlockSpec((tm, w_qk.shape[1]), lambda i: (i, 0)),
            pl.BlockSpec((w_vt.shape[0], tm), lambda i: (0, i)),
            pl.BlockSpec((tm, w_m.shape[1]), lambda i: (i, 0)),
            pl.BlockSpec((nh, tm), lambda i: (0, i)),
        ],
        out_shape=[
            jax.ShapeDtypeStruct((m, w_qk.shape[1]), BF16),
            jax.ShapeDtypeStruct((w_vt.shape[0], m), BF16),
            jax.ShapeDtypeStruct((m, w_m.shape[1]), F32),
            jax.ShapeDtypeStruct((nh, m), F32),
        ],
        scratch_shapes=[pltpu.VMEM((nh, LANES), F32)],
        compiler_params=_params("arbitrary"),
        name="proj",
    )(x, norm_w, w_qk, w_vt, w_m, w_ff_t, b_f)


def _flash_pair(q_ref, k_ref, vt_ref, bias_ref, kaug_ref, vaug_ref, qm_ref, s_ref, acc_ref, finish, *, tq, tk, v_rows):
    seq = k_ref.shape[1]
    dr = v_rows[0].stop - v_rows[0].start
    half = LANES // 2

    kaug_ref[:, 0:LANES] = k_ref[0]
    terms = []
    for c in range(2):
        rem = bias_ref[0, c:c + 1, :] * LOG2E
        for _ in range(BIAS_TERMS):
            part = rem.astype(BF16).astype(F32)
            terms.append(part)
            rem = rem - part
    term_rows = jnp.concatenate(terms + [jnp.zeros((LANES - 2 * BIAS_TERMS, seq), F32)], axis=0)
    for blk in range(seq // LANES):
        sl = slice(blk * LANES, (blk + 1) * LANES)
        kaug_ref[sl, LANES:] = term_rows[:, sl].T.astype(BF16)
    for c in range(2):
        vaug_ref[c, 0:dr, :] = vt_ref[v_rows[c], :]
        vaug_ref[c, dr:, :] = jnp.ones((BF16_ROWS, seq), BF16)

    qlane = lax.broadcasted_iota(jnp.int32, (seq, LANES), 1)
    q = q_ref[0]
    for c in range(2):
        qm_ref[c, :, 0:LANES] = jnp.where((qlane >= c * half) & (qlane < (c + 1) * half), q, jnp.zeros_like(q))
        picks_bias = (qlane >= BIAS_TERMS * c) & (qlane < BIAS_TERMS * (c + 1))
        qm_ref[c, :, LANES:] = jnp.where(picks_bias, 1.0, 0.0).astype(BF16)

    work = []
    for i in range(seq // tq):
        n_full = (i * tq) // tk
        lead = i * tq - n_full * tk
        work += [(i, j * tk, tk, None, j == 0, False) for j in range(n_full)]
        work.append((i, n_full * tk, lead + tq, lead, n_full == 0, True))

    def scores(n):
        i, k0, size, _, _, _ = work[n]
        for c in range(2):
            s_ref[n % s_ref.shape[0], c, 0:size, :] = lax.dot_general(
                kaug_ref[k0:k0 + size, :], qm_ref[c, i * tq:(i + 1) * tq, :], NT_DIMS,
                preferred_element_type=F32)

    scores(0)
    scores(1)
    m = [None, None]
    for n, (i, k0, size, lead, first, last) in enumerate(work):
        if n + 2 < len(work):
            scores(n + 2)
        for c in range(2):
            s = s_ref[n % s_ref.shape[0], c, 0:size, :]
            if lead is not None:
                key = lax.broadcasted_iota(jnp.int32, (tq, tq), 0)
                qry = lax.broadcasted_iota(jnp.int32, (tq, tq), 1)
                diag = jnp.where(key <= qry, s[lead:, :], -jnp.inf)
                s = diag if lead == 0 else jnp.concatenate([s[0:lead, :], diag], axis=0)
            m_blk = jnp.max(s, axis=0, keepdims=True)
            m_new = m_blk if first else jnp.maximum(m[c], m_blk)
            p = jnp.exp2(s - m_new).astype(BF16)
            pv = jnp.dot(vaug_ref[c, :, k0:k0 + size], p, preferred_element_type=F32)
            if first:
                acc_ref[i, c] = pv
            else:
                acc_ref[i, c] = jnp.exp2(m[c] - m_new) * acc_ref[i, c] + pv
            m[c] = m_new
        if last:
            finish(i, *[acc_ref[i, c, 0:dr, :] * (1.0 / acc_ref[i, c, dr:dr + 1, :]) for c in range(2)])


def _diff_attn_kernel(q_ref, k_ref, vt_ref, bias_ref, lq1_ref, lk1_ref, lq2_ref, lk2_ref, sub_ref,
                      o_ref, *scratch, tq, tk, lam_init):
    lam = (jnp.exp(jnp.sum(lq1_ref[...] * lk1_ref[...], axis=1, keepdims=True))
           - jnp.exp(jnp.sum(lq2_ref[...] * lk2_ref[...], axis=1, keepdims=True)) + lam_init)

    def finish(i, a1, a2):
        d = (a1 - lam * a2).T
        o_ref[0, i * tq:(i + 1) * tq, :] = (_rms_norm(d, sub_ref[...]) * (1.0 - lam_init)).astype(o_ref.dtype)

    full = slice(0, LANES)
    _flash_pair(q_ref, k_ref, vt_ref, bias_ref, *scratch, finish, tq=tq, tk=tk, v_rows=(full, full))


def _fox_attn_kernel(q_ref, k_ref, vt_ref, bias_ref, o_ref, *scratch, tq, tk):
    def finish(i, a_even, a_odd):
        o_ref[0, i * tq:(i + 1) * tq, :] = jnp.concatenate([a_even, a_odd], axis=0).T.astype(o_ref.dtype)

    half = LANES // 2
    _flash_pair(q_ref, k_ref, vt_ref, bias_ref, *scratch, finish, tq=tq, tk=tk,
                v_rows=(slice(0, half), slice(half, LANES)))


def _attention(kernel, qk, vt, bias, extra, *, q_col, v_row, v_rows, bias_per_batch, tq=256, tk=1024, name):
    b, s, _ = qk.shape
    assert s % tk == 0 and tk % tq == 0 and tq % LANES == 0, (s, tq, tk)
    groups = ATTN_WIDTH // LANES
    if bias_per_batch:
        bias_map = lambda bi, g: (g, 0, bi)
    else:
        bias_map = lambda bi, g: (g, 0, 0)
    in_specs = [
        pl.BlockSpec((1, s, LANES), lambda bi, g: (bi, 0, q_col + g)),
        pl.BlockSpec((1, s, LANES), lambda bi, g: (bi, 0, q_col + groups + g)),
        pl.BlockSpec((LANES, s), lambda bi, g: (v_row + g, bi)),
        pl.BlockSpec((1, 2, s), bias_map),
    ] + [pl.BlockSpec(e.shape, lambda bi, g: (0, 0)) for e in extra]
    rows = v_rows + BF16_ROWS
    return pl.pallas_call(
        functools.partial(kernel, tq=tq, tk=tk),
        grid=(b, groups),
        in_specs=in_specs,
        out_specs=pl.BlockSpec((1, s, LANES), lambda bi, g: (bi, 0, g)),
        out_shape=jax.ShapeDtypeStruct((b, s, ATTN_WIDTH), BF16),
        scratch_shapes=[pltpu.VMEM((s, 2 * LANES), BF16), pltpu.VMEM((2, rows, s), BF16),
                        pltpu.VMEM((2, s, 2 * LANES), BF16), pltpu.VMEM((SCORE_SLOTS, 2, tk, tq), F32),
                        pltpu.VMEM((s // tq, 2, rows, tq), F32)],
        compiler_params=_params("parallel", "parallel"),
        name=name,
    )(qk, qk, vt, bias, *extra)


def _mlstm_kernel(cur_ref, prev_ref, z_ref, cw_ref, cb_ref, wqt_ref, wk_ref, wvt_ref, wif_ref, bif_ref,
                  nrm_ref, skip_ref, o_ref, xbuf_ref, ct_ref, n_ref, m_ref):
    ci = pl.program_id(1)
    nb, span, _ = cur_ref.shape
    chunks = span // MLSTM_CHUNK

    @pl.when(ci == 0)
    def _():
        ct_ref[...] = jnp.zeros_like(ct_ref)
        n_ref[...] = jnp.zeros_like(n_ref)
        m_ref[...] = jnp.zeros_like(m_ref)

    for r in range(nb):
        xbuf_ref[r, 0:SUBLANES, :] = jnp.where(ci > 0, prev_ref[r], 0.0)
        xbuf_ref[r, SUBLANES:, :] = cur_ref[r]

    work = [(r, k) for k in range(chunks) for r in range(nb)]
    refs = (cw_ref, cb_ref, wqt_ref, wk_ref, wvt_ref, wif_ref, bif_ref)
    ahead = min(MLSTM_STAGE_AHEAD, len(work))
    staged = {n: _mlstm_gates(xbuf_ref.at[work[n][0]], work[n][1], *refs) for n in range(ahead)}
    for n, (r, k) in enumerate(work):
        if n + ahead < len(work):
            staged[n + ahead] = _mlstm_gates(xbuf_ref.at[work[n + ahead][0]], work[n + ahead][1], *refs)
        rows = slice(k * MLSTM_CHUNK, (k + 1) * MLSTM_CHUNK)
        _mlstm_recur(staged.pop(n), z_ref.at[r, rows], nrm_ref, skip_ref, o_ref.at[r, rows],
                     ct_ref.at[r], n_ref.at[r], m_ref.at[r])


def _mlstm_gates(xbuf_ref, k, cw_ref, cb_ref, wqt_ref, wk_ref, wvt_ref, wif_ref, bif_ref):
    L = MLSTM_CHUNK
    dh = MLSTM_HEAD_DIM
    nh = MLSTM_HEADS

    base = SUBLANES + k * L
    cur = xbuf_ref[base:base + L, :]
    conv = cb_ref[...]
    for j in range(MLSTM_CONV):
        off = base - (MLSTM_CONV - 1) + j
        conv = conv + cw_ref[j:j + 1, :] * xbuf_ref[off:off + L, :]
    xc = _silu(conv)

    qts, kss, vts = [], [], []
    ift = bif_ref[...]
    for h in range(nh):
        sl = slice(h * dh, (h + 1) * dh)
        xh = xc[:, sl].astype(BF16)
        vh = cur[:, sl].astype(BF16)
        qt = lax.dot_general(wqt_ref[h], xh, NT_DIMS, preferred_element_type=F32).astype(BF16)
        kk = jnp.dot(xh, wk_ref[h], preferred_element_type=F32)
        vt = lax.dot_general(wvt_ref[h], vh, NT_DIMS, preferred_element_type=F32).astype(BF16)
        ift = ift + jnp.dot(wif_ref[:, sl], qt, preferred_element_type=F32)
        ift = ift + lax.dot_general(wif_ref[:, MLSTM_WIDTH + h * dh:MLSTM_WIDTH + (h + 1) * dh],
                                    kk.astype(BF16), NT_DIMS, preferred_element_type=F32)
        ift = ift + jnp.dot(wif_ref[:, 2 * MLSTM_WIDTH + h * dh:2 * MLSTM_WIDTH + (h + 1) * dh], vt,
                            preferred_element_type=F32)
        qts.append(qt)
        kss.append((kk * (dh ** -0.5)).astype(BF16))
        vts.append(vt)

    key = lax.broadcasted_iota(jnp.int32, (L, L), 0)
    qry = lax.broadcasted_iota(jnp.int32, (L, L), 1)
    lf = _log_sigmoid(ift)
    tri = jnp.where(key <= qry, 1.0, 0.0).astype(BF16)
    b_all = jnp.zeros_like(lf)
    rem = lf
    for _ in range(3):
        part = rem.astype(BF16)
        b_all = b_all + jnp.dot(part, tri, preferred_element_type=F32)
        rem = rem - part.astype(F32)
    return dict(xc=xc, qts=qts, kss=kss, vts=vts, li_rows=ift[0:nh], b_rows=b_all[nh:2 * nh])


def _mlstm_recur(st, z_ref, nrm_ref, skip_ref, o_ref, ct_ref, n_ref, m_ref):
    L = MLSTM_CHUNK
    dh = MLSTM_HEAD_DIM
    key = lax.broadcasted_iota(jnp.int32, (L, L), 0)
    qry = lax.broadcasted_iota(jnp.int32, (L, L), 1)
    causal = key <= qry
    xc = st["xc"]

    heads = range(MLSTM_HEADS)
    qts, kss, vts = st["qts"], st["kss"], st["vts"]
    ct_state = [ct_ref[h] for h in heads]
    n_state = [n_ref[h] for h in heads]
    m_prev = [m_ref[h:h + 1, 0:1] for h in heads]
    qk = [jnp.dot(kss[h], qts[h], preferred_element_type=F32) for h in heads]
    cq = [jnp.dot(ct_state[h].astype(BF16), qts[h], preferred_element_type=F32) for h in heads]
    nq = [jnp.dot(n_state[h].astype(BF16), qts[h], preferred_element_type=F32)[0:1] for h in heads]

    li = [st["li_rows"][h:h + 1] for h in heads]
    b = [st["b_rows"][h:h + 1] for h in heads]
    a = [li[h] - b[h] for h in heads]
    a_keys = [jnp.broadcast_to(a[h], (L, L)).T for h in heads]
    d = [jnp.where(causal, b[h] + a_keys[h], -jnp.inf) for h in heads]
    inter = [b[h] + m_prev[h] for h in heads]
    m_t = [jnp.maximum(inter[h], jnp.max(d[h], axis=0, keepdims=True)) for h in heads]
    w_inter = [jnp.exp(inter[h] - m_t[h]) for h in heads]
    s = [qk[h] * jnp.exp(d[h] - m_t[h]) for h in heads]
    sv = [jnp.dot(vts[h], s[h].astype(BF16), preferred_element_type=F32) for h in heads]

    b_last = [b[h][:, L - 1:L] for h in heads]
    g = [b_last[h] + a[h] for h in heads]
    m_new = [jnp.maximum(b_last[h] + m_prev[h], jnp.max(g[h], axis=1, keepdims=True)) for h in heads]
    wk = [jnp.exp(g[h] - m_new[h]) for h in heads]
    dec = [jnp.exp(b_last[h] + m_prev[h] - m_new[h]) for h in heads]
    for h in heads:
        ct_ref[h] = dec[h] * ct_state[h] + jnp.dot((vts[h].astype(F32) * wk[h]).astype(BF16), kss[h],
                                                   preferred_element_type=F32)
        n_ref[h] = dec[h] * n_state[h] + jnp.dot(jnp.broadcast_to(wk[h], (SUBLANES, L)).astype(BF16),
                                                 kss[h], preferred_element_type=F32)
        m_ref[h:h + 1, :] = jnp.broadcast_to(m_new[h], (1, LANES))

    for h in heads:
        sl = slice(h * dh, (h + 1) * dh)
        num = w_inter[h] * cq[h] + sv[h]
        den = w_inter[h] * nq[h] + jnp.sum(s[h], axis=0, keepdims=True)
        hh = num * (1.0 / jnp.maximum(jnp.abs(den), jnp.exp(-m_t[h])))
        mu = jnp.mean(hh, axis=0, keepdims=True)
        cen = hh - mu
        var = jnp.mean(cen * cen, axis=0, keepdims=True)
        hn = (cen * lax.rsqrt(var + LN_EPS)).T * nrm_ref[:, sl]
        o_ref[:, sl] = ((hn + skip_ref[:, sl] * xc[:, sl]) * _silu(z_ref[:, sl])).astype(o_ref.dtype)


def _mlstm(mxz, conv_w, conv_b, wq, wk, wv, w_if_t, b_if, norm_w, skip_w, *, nb=2, chunks=4):
    b, s, _ = mxz.shape
    span = chunks * MLSTM_CHUNK
    assert b % nb == 0 and s % span == 0, (b, s)
    w = MLSTM_WIDTH
    rows_per_span = span // SUBLANES
    const2 = lambda bi, c: (0, 0)
    const3 = lambda bi, c: (0, 0, 0)
    return pl.pallas_call(
        _mlstm_kernel,
        grid=(b // nb, s // span),
        in_specs=[
            pl.BlockSpec((nb, span, w), lambda bi, c: (bi, c, 0)),
            pl.BlockSpec((nb, SUBLANES, w), lambda bi, c: (bi, jnp.maximum(c * rows_per_span - 1, 0), 0)),
            pl.BlockSpec((nb, span, w), lambda bi, c: (bi, c, 1)),
            pl.BlockSpec(conv_w.shape, const2),
            pl.BlockSpec(conv_b.shape, const2),
            pl.BlockSpec(wq.shape, const3),
            pl.BlockSpec(wk.shape, const3),
            pl.BlockSpec(wv.shape, const3),
            pl.BlockSpec(w_if_t.shape, const2),
            pl.BlockSpec(b_if.shape, const2),
            pl.BlockSpec(norm_w.shape, const2),
            pl.BlockSpec(skip_w.shape, const2),
        ],
        out_specs=pl.BlockSpec((nb, span, w), lambda bi, c: (bi, c, 0)),
        out_shape=jax.ShapeDtypeStruct((b, s, w), BF16),
        scratch_shapes=[
            pltpu.VMEM((nb, SUBLANES + span, w), F32),
            pltpu.VMEM((nb, MLSTM_HEADS, MLSTM_HEAD_DIM, MLSTM_HEAD_DIM), F32),
            pltpu.VMEM((nb, MLSTM_HEADS, SUBLANES, MLSTM_HEAD_DIM), F32),
            pltpu.VMEM((nb, SUBLANES, LANES), F32),
        ],
        compiler_params=_params("parallel", "arbitrary"),
        name="mlstm",
    )(mxz, mxz, mxz, conv_w, conv_b, wq, wk, wv, w_if_t, b_if, norm_w, skip_w)


def _merge_kernel(x_ref, nw_ref, wg_ref, gb_ref, od_ref, of_ref, om_ref, wbd_ref, wbf_ref, wbm_ref,
                  wo_ref, o_ref):
    x = x_ref[...]
    d = x.shape[1]
    xn = _rms_norm(x, nw_ref[...]).astype(BF16)
    merged = None
    for br, (b_ref, w_ref) in enumerate(((od_ref, wbd_ref), (of_ref, wbf_ref), (om_ref, wbm_ref))):
        cols = slice(br * d, (br + 1) * d)
        gate = jax.nn.sigmoid(jnp.dot(xn, wg_ref[:, cols], preferred_element_type=F32) + gb_ref[:, cols])
        term = gate * jnp.dot(b_ref[...], w_ref[...], preferred_element_type=F32)
        merged = term if merged is None else merged + term
    o_ref[...] = x + jnp.dot(merged.astype(BF16), wo_ref[...], preferred_element_type=F32)


def _merge(x, norm_w, w_g, gate_b, o_diff, o_fox, o_mlstm, wbd, wbf, wbm, w_out, *, tm=512):
    m, d = x.shape
    assert m % tm == 0, m
    const = lambda i: (0, 0)
    row = lambda i: (i, 0)
    return pl.pallas_call(
        _merge_kernel,
        grid=(m // tm,),
        in_specs=[
            pl.BlockSpec((tm, d), row),
            pl.BlockSpec((1, d), const),
            pl.BlockSpec(w_g.shape, const),
            pl.BlockSpec(gate_b.shape, const),
            pl.BlockSpec((tm, ATTN_WIDTH), row),
            pl.BlockSpec((tm, ATTN_WIDTH), row),
            pl.BlockSpec((tm, MLSTM_WIDTH), row),
            pl.BlockSpec(wbd.shape, const),
            pl.BlockSpec(wbf.shape, const),
            pl.BlockSpec(wbm.shape, const),
            pl.BlockSpec(w_out.shape, const),
        ],
        out_specs=pl.BlockSpec((tm, d), row),
        out_shape=jax.ShapeDtypeStruct((m, d), F32),
        compiler_params=_params("parallel"),
        name="merge",
    )(x, norm_w, w_g, gate_b, o_diff, o_fox, o_mlstm, wbd, wbf, wbm, w_out)


def _alibi_bias(seq):
    slopes = np.array([2.0 ** (-8.0 * (h + 1) / DIFF_HEADS) for h in range(DIFF_HEADS)], dtype=np.float32)
    pos = np.arange(seq, dtype=np.float32)
    per_head = slopes[:, None] * pos[None, :]
    return jnp.asarray(np.repeat(per_head[:, None, :], 2, axis=1))


def kernel(x, ffn1_norm, ffn1_w_gate, ffn1_w_up, ffn1_w_down, mix_norm, w_in, gate_bias, diff_lq1, diff_lk1, diff_lq2, diff_lk2, diff_subln, fox_b_f, mlstm_conv_w, mlstm_conv_b, mlstm_wq, mlstm_wk, mlstm_wv, mlstm_w_if, mlstm_b_if, mlstm_norm, mlstm_skip, w_branch_diff, w_branch_fox, w_branch_mlstm, w_out, ffn2_norm, ffn2_w_gate, ffn2_w_up, ffn2_w_down, final_norm):
    batch, seq, d = x.shape
    depth = w_in.shape[0]
    m = batch * seq
    bf = lambda a: a.astype(BF16)
    row = lambda a: a.reshape(1, -1)
    aw = ATTN_WIDTH
    ff0 = 6 * aw
    mx0 = ff0 + FOX_HEADS
    g0 = mx0 + 2 * MLSTM_WIDTH
    blocks = aw // LANES
    alibi = _alibi_bias(seq)

    x = x.reshape(m, d)
    for l in range(depth):
        x = _ffn(x, row(ffn1_norm[l]), ffn1_w_gate, ffn1_w_up, ffn1_w_down, l)

        w_in_l = w_in[l]
        w_qk = bf(jnp.concatenate([w_in_l[:, 0:2 * aw], w_in_l[:, 3 * aw:5 * aw]], axis=1))
        w_vt = bf(jnp.concatenate([w_in_l[:, 2 * aw:3 * aw], w_in_l[:, 5 * aw:6 * aw]], axis=1).T)
        qk, vt, mxz, cf = _proj(x, row(mix_norm[l]), w_qk, w_vt, bf(w_in_l[:, mx0:g0]),
                                bf(w_in_l[:, ff0:mx0].T), fox_b_f[l].reshape(-1, 1), seq=seq)
        qk = qk.reshape(batch, seq, 4 * aw)
        lam_init = 0.8 - 0.6 * math.exp(-0.3 * l)
        o_diff = _attention(
            functools.partial(_diff_attn_kernel, lam_init=lam_init), qk, vt, alibi,
            [row(diff_lq1[l]), row(diff_lk1[l]), row(diff_lq2[l]), row(diff_lk2[l]), row(diff_subln[l])],
            q_col=0, v_row=0, v_rows=LANES, bias_per_batch=False, name="diff_attn")
        o_fox = _attention(_fox_attn_kernel, qk, vt, cf.reshape(FOX_HEADS // 2, 2, m), [],
                           q_col=2 * blocks, v_row=blocks, v_rows=LANES // 2, bias_per_batch=True,
                           name="fox_attn")
        o_mlstm = _mlstm(mxz.reshape(batch, seq, 2 * MLSTM_WIDTH), mlstm_conv_w[l], row(mlstm_conv_b[l]),
                         bf(jnp.swapaxes(mlstm_wq[l], 1, 2)), bf(mlstm_wk[l]),
                         bf(jnp.swapaxes(mlstm_wv[l], 1, 2)), bf(mlstm_w_if[l].T),
                         mlstm_b_if[l].reshape(-1, 1), row(mlstm_norm[l]), row(mlstm_skip[l]))
        x = _merge(x, row(mix_norm[l]), bf(w_in_l[:, g0:]), gate_bias[l].reshape(1, -1),
                   o_diff.reshape(m, aw), o_fox.reshape(m, aw),
                   o_mlstm.reshape(m, MLSTM_WIDTH), bf(w_branch_diff[l]), bf(w_branch_fox[l]),
                   bf(w_branch_mlstm[l]), bf(w_out[l]))

        last = l == depth - 1
        x = _ffn(x, row(ffn2_norm[l]), ffn2_w_gate, ffn2_w_up, ffn2_w_down, l,
                 row(final_norm) if last else None)
    return x.reshape(batch, seq, d)
```

```python
import functools
import math

import numpy as np
import jax
import jax.numpy as jnp
from jax import lax
from jax.experimental import pallas as pl
from jax.experimental.pallas import tpu as pltpu

RMS_EPS = 1e-6
LN_EPS = 1e-5
DIFF_HEADS = 4
ATTN_HEAD_DIM = 64
FOX_HEADS = 8
MLSTM_HEADS = 4
MLSTM_HEAD_DIM = 128
MLSTM_WIDTH = MLSTM_HEADS * MLSTM_HEAD_DIM
MLSTM_CONV = 4
MLSTM_CHUNK = 128
MLSTM_STAGE_AHEAD = 2
ATTN_WIDTH = 512
LANES = 128
SUBLANES = 8
BF16_ROWS = 16
SCORE_SLOTS = 4
BIAS_TERMS = 3
VMEM_LIMIT_BYTES = 56 * 1024 * 1024
LOG2E = math.log2(math.e)
Q_SCALE = ATTN_HEAD_DIM ** -0.5 * LOG2E

F32 = jnp.float32
BF16 = jnp.bfloat16
NT_DIMS = (((1,), (1,)), ((), ()))


def _params(*semantics):
    return pltpu.CompilerParams(dimension_semantics=semantics, vmem_limit_bytes=VMEM_LIMIT_BYTES)


def _rms_norm(x, w):
    return x * lax.rsqrt(jnp.mean(x * x, axis=-1, keepdims=True) + RMS_EPS) * w


def _silu(x):
    return x * jax.nn.sigmoid(x)


def _log_sigmoid(x):
    return jnp.minimum(x, 0.0) - jnp.log1p(jnp.exp(-jnp.abs(x)))


def _lane_cumsum(x):
    n = x.shape[-1]
    lane = lax.broadcasted_iota(jnp.int32, x.shape, 1)
    shift = 1
    while shift < n:
        x = x + jnp.where(lane >= shift, pltpu.roll(x, shift, axis=1), 0.0)
        shift *= 2
    return x


def _ffn_kernel(*refs, final):
    if final:
        x_ref, nw_ref, wg_ref, wu_ref, wd_ref, fn_ref, o_ref, xn_ref = refs
    else:
        x_ref, nw_ref, wg_ref, wu_ref, wd_ref, o_ref, xn_ref = refs
    f = pl.program_id(1)

    @pl.when(f == 0)
    def _():
        xn_ref[...] = _rms_norm(x_ref[...], nw_ref[...]).astype(BF16)
        o_ref[...] = jnp.zeros_like(o_ref)

    xn = xn_ref[...]
    g = jnp.dot(xn, wg_ref[...].astype(BF16), preferred_element_type=F32)
    u = jnp.dot(xn, wu_ref[...].astype(BF16), preferred_element_type=F32)
    h = (_silu(g) * u).astype(BF16)
    o_ref[...] += jnp.dot(h, wd_ref[...].astype(BF16), preferred_element_type=F32)

    @pl.when(f == pl.num_programs(1) - 1)
    def _():
        y = x_ref[...] + 0.5 * o_ref[...]
        if final:
            y = _rms_norm(y, fn_ref[...])
        o_ref[...] = y


def _ffn(x, norm_w, w_gate, w_up, w_down, layer, final_norm=None, *, tm=2048, tf=256):
    m, d = x.shape
    dff = w_gate.shape[2]
    assert m % tm == 0 and dff % tf == 0, (m, dff)
    final = final_norm is not None
    in_specs = [
        pl.BlockSpec((tm, d), lambda i, f: (i, 0)),
        pl.BlockSpec((1, d), lambda i, f: (0, 0)),
        pl.BlockSpec((None, d, tf), lambda i, f: (layer, 0, f)),
        pl.BlockSpec((None, d, tf), lambda i, f: (layer, 0, f)),
        pl.BlockSpec((None, tf, d), lambda i, f: (layer, f, 0)),
    ]
    args = [x, norm_w, w_gate, w_up, w_down]
    if final:
        in_specs.append(pl.BlockSpec((1, d), lambda i, f: (0, 0)))
        args.append(final_norm)
    return pl.pallas_call(
        functools.partial(_ffn_kernel, final=final),
        grid=(m // tm, dff // tf),
        in_specs=in_specs,
        out_specs=pl.BlockSpec((tm, d), lambda i, f: (i, 0)),
        out_shape=jax.ShapeDtypeStruct((m, d), F32),
        scratch_shapes=[pltpu.VMEM((tm, d), BF16)],
        compiler_params=_params("parallel", "arbitrary"),
        name="ffn_final" if final else "ffn",
    )(*args)


def _proj_kernel(x_ref, nw_ref, wdq_ref, wdk_ref, wfq_ref, wfk_ref, wvt_ref, wm_ref, wff_ref, bf_ref,
                 qk_ref, vt_ref, mxz_ref, cf_ref, carry_ref, *, tiles_per_seq):
    i = pl.program_id(0)

    @pl.when(i == 0)
    def _():
        carry_ref[...] = jnp.zeros_like(carry_ref)

    xn = _rms_norm(x_ref[...], nw_ref[...]).astype(BF16)

    ff = lax.dot_general(wff_ref[...], xn, NT_DIMS, preferred_element_type=F32)
    csum = _lane_cumsum(_log_sigmoid(ff + bf_ref[...]))
    csum = csum + jnp.where(i % tiles_per_seq == 0, 0.0, carry_ref[:, 0:1])
    carry_ref[...] = jnp.broadcast_to(csum[:, -1:], carry_ref.shape)
    cf_ref[...] = -csum

    for grp, w_ref in enumerate((wdq_ref, wdk_ref, wfq_ref, wfk_ref)):
        cols = slice(grp * ATTN_WIDTH, (grp + 1) * ATTN_WIDTH)
        r = jnp.dot(xn, w_ref[...], preferred_element_type=F32)
        if grp % 2 == 0:
            r = r * Q_SCALE
        qk_ref[:, cols] = r.astype(BF16)
    for grp in range(2):
        rows = slice(grp * ATTN_WIDTH, (grp + 1) * ATTN_WIDTH)
        vt_ref[rows, :] = lax.dot_general(wvt_ref[rows, :], xn, NT_DIMS,
                                          preferred_element_type=F32).astype(BF16)
    for grp in range(2):
        cols = slice(grp * MLSTM_WIDTH, (grp + 1) * MLSTM_WIDTH)
        mxz_ref[:, cols] = jnp.dot(xn, wm_ref[:, cols], preferred_element_type=F32)


def _proj(x, norm_w, w_in_b, layer, w_vt, w_m, w_ff_t, b_f, *, seq, tm=512):
    m, d = x.shape
    nh = w_ff_t.shape[0]
    assert seq % tm == 0 and m % seq == 0, (m, seq)
    const = lambda i: (0, 0)
    qk_groups = (0, 1, 3, 4)
    qk_specs = [pl.BlockSpec((None, d, ATTN_WIDTH), functools.partial(lambda i, g: (layer, 0, g), g=g))
                for g in qk_groups]
    return pl.pallas_call(
        functools.partial(_proj_kernel, tiles_per_seq=seq // tm),
        grid=(m // tm,),
        in_specs=[
            pl.BlockSpec((tm, d), lambda i: (i, 0)),
            pl.BlockSpec((1, d), const),
            *qk_specs,
            pl.BlockSpec(w_vt.shape, const),
            pl.BlockSpec(w_m.shape, const),
            pl.BlockSpec(w_ff_t.shape, const),
            pl.BlockSpec((nh, 1), const),
        ],
        out_specs=[
            pl.BlockSpec((tm, len(qk_groups) * ATTN_WIDTH), lambda i: (i, 0)),
            pl.BlockSpec((w_vt.shape[0], tm), lambda i: (0, i)),
            pl.BlockSpec((tm, w_m.shape[1]), lambda i: (i, 0)),
            pl.BlockSpec((nh, tm), lambda i: (0, i)),
        ],
        out_shape=[
            jax.ShapeDtypeStruct((m, len(qk_groups) * ATTN_WIDTH), BF16),
            jax.ShapeDtypeStruct((w_vt.shape[0], m), BF16),
            jax.ShapeDtypeStruct((m, w_m.shape[1]), F32),
            jax.ShapeDtypeStruct((nh, m), F32),
        ],
        scratch_shapes=[pltpu.VMEM((nh, LANES), F32)],
        compiler_params=_params("arbitrary"),
        name="proj",
    )(x, norm_w, w_in_b, w_in_b, w_in_b, w_in_b, w_vt, w_m, w_ff_t, b_f)


def _flash_pair(q_ref, k_ref, vt_ref, bias_ref, kaug_ref, vaug_ref, qm_ref, s_ref, acc_ref, finish, *, tq, tk, v_rows):
    seq = k_ref.shape[1]
    dr = v_rows[0].stop - v_rows[0].start
    half = LANES // 2

    kaug_ref[:, 0:LANES] = k_ref[0]
    terms = []
    for c in range(2):
        rem = bias_ref[0, c:c + 1, :] * LOG2E
        for _ in range(BIAS_TERMS):
            part = rem.astype(BF16).astype(F32)
            terms.append(part)
            rem = rem - part
    term_rows = jnp.concatenate(terms + [jnp.zeros((LANES - 2 * BIAS_TERMS, seq), F32)], axis=0)
    for blk in range(seq // LANES):
        sl = slice(blk * LANES, (blk + 1) * LANES)
        kaug_ref[sl, LANES:] = term_rows[:, sl].T.astype(BF16)
    for c in range(2):
        vaug_ref[c, 0:dr, :] = vt_ref[v_rows[c], :]
        vaug_ref[c, dr:, :] = jnp.ones((BF16_ROWS, seq), BF16)

    qlane = lax.broadcasted_iota(jnp.int32, (seq, LANES), 1)
    q = q_ref[0]
    for c in range(2):
        qm_ref[c, :, 0:LANES] = jnp.where((qlane >= c * half) & (qlane < (c + 1) * half), q, jnp.zeros_like(q))
        picks_bias = (qlane >= BIAS_TERMS * c) & (qlane < BIAS_TERMS * (c + 1))
        qm_ref[c, :, LANES:] = jnp.where(picks_bias, 1.0, 0.0).astype(BF16)

    work = []
    for i in range(seq // tq):
        n_full = (i * tq) // tk
        lead = i * tq - n_full * tk
        work += [(i, j * tk, tk, None, j == 0, False) for j in range(n_full)]
        work.append((i, n_full * tk, lead + tq, lead, n_full == 0, True))

    def scores(n):
        i, k0, size, _, _, _ = work[n]
        for c in range(2):
            s_ref[n % s_ref.shape[0], c, 0:size, :] = lax.dot_general(
                kaug_ref[k0:k0 + size, :], qm_ref[c, i * tq:(i + 1) * tq, :], NT_DIMS,
                preferred_element_type=F32)

    scores(0)
    scores(1)
    m = [None, None]
    for n, (i, k0, size, lead, first, last) in enumerate(work):
        if n + 2 < len(work):
            scores(n + 2)
        for c in range(2):
            s = s_ref[n % s_ref.shape[0], c, 0:size, :]
            if lead is not None:
                key = lax.broadcasted_iota(jnp.int32, (tq, tq), 0)
                qry = lax.broadcasted_iota(jnp.int32, (tq, tq), 1)
                diag = jnp.where(key <= qry, s[lead:, :], -jnp.inf)
                s = diag if lead == 0 else jnp.concatenate([s[0:lead, :], diag], axis=0)
            m_blk = jnp.max(s, axis=0, keepdims=True)
            m_new = m_blk if first else jnp.maximum(m[c], m_blk)
            p = jnp.exp2(s - m_new).astype(BF16)
            pv = jnp.dot(vaug_ref[c, :, k0:k0 + size], p, preferred_element_type=F32)
            if first:
                acc_ref[i, c] = pv
            else:
                acc_ref[i, c] = jnp.exp2(m[c] - m_new) * acc_ref[i, c] + pv
            m[c] = m_new
        if last:
            finish(i, *[acc_ref[i, c, 0:dr, :] * (1.0 / acc_ref[i, c, dr:dr + 1, :]) for c in range(2)])


def _diff_attn_kernel(q_ref, k_ref, vt_ref, bias_ref, lq1_ref, lk1_ref, lq2_ref, lk2_ref, sub_ref,
                      o_ref, *scratch, tq, tk, lam_init):
    lam = (jnp.exp(jnp.sum(lq1_ref[...] * lk1_ref[...], axis=1, keepdims=True))
           - jnp.exp(jnp.sum(lq2_ref[...] * lk2_ref[...], axis=1, keepdims=True)) + lam_init)

    def finish(i, a1, a2):
        d = (a1 - lam * a2).T
        o_ref[0, i * tq:(i + 1) * tq, :] = (_rms_norm(d, sub_ref[...]) * (1.0 - lam_init)).astype(o_ref.dtype)

    full = slice(0, LANES)
    _flash_pair(q_ref, k_ref, vt_ref, bias_ref, *scratch, finish, tq=tq, tk=tk, v_rows=(full, full))


def _fox_attn_kernel(q_ref, k_ref, vt_ref, bias_ref, o_ref, *scratch, tq, tk):
    def finish(i, a_even, a_odd):
        o_ref[0, i * tq:(i + 1) * tq, :] = jnp.concatenate([a_even, a_odd], axis=0).T.astype(o_ref.dtype)

    half = LANES // 2
    _flash_pair(q_ref, k_ref, vt_ref, bias_ref, *scratch, finish, tq=tq, tk=tk,
                v_rows=(slice(0, half), slice(half, LANES)))


def _attention(kernel, qk, vt, bias, extra, *, q_col, v_row, v_rows, bias_per_batch, tq=256, tk=1024, name):
    b, s, _ = qk.shape
    assert s % tk == 0 and tk % tq == 0 and tq % LANES == 0, (s, tq, tk)
    groups = ATTN_WIDTH // LANES
    if bias_per_batch:
        bias_map = lambda bi, g: (g, 0, bi)
    else:
        bias_map = lambda bi, g: (g, 0, 0)
    in_specs = [
        pl.BlockSpec((1, s, LANES), lambda bi, g: (bi, 0, q_col + g)),
        pl.BlockSpec((1, s, LANES), lambda bi, g: (bi, 0, q_col + groups + g)),
        pl.BlockSpec((LANES, s), lambda bi, g: (v_row + g, bi)),
        pl.BlockSpec((1, 2, s), bias_map),
    ] + [pl.BlockSpec(e.shape, lambda bi, g: (0, 0)) for e in extra]
    rows = v_rows + BF16_ROWS
    return pl.pallas_call(
        functools.partial(kernel, tq=tq, tk=tk),
        grid=(b, groups),
        in_specs=in_specs,
        out_specs=pl.BlockSpec((1, s, LANES), lambda bi, g: (bi, 0, g)),
        out_shape=jax.ShapeDtypeStruct((b, s, ATTN_WIDTH), BF16),
        scratch_shapes=[pltpu.VMEM((s, 2 * LANES), BF16), pltpu.VMEM((2, rows, s), BF16),
                        pltpu.VMEM((2, s, 2 * LANES), BF16), pltpu.VMEM((SCORE_SLOTS, 2, tk, tq), F32),
                        pltpu.VMEM((s // tq, 2, rows, tq), F32)],
        compiler_params=_params("parallel", "parallel"),
        name=name,
    )(qk, qk, vt, bias, *extra)


def _mlstm_kernel(cur_ref, prev_ref, z_ref, cw_ref, cb_ref, wqt_ref, wk_ref, wvt_ref, wif_ref, bif_ref,
                  nrm_ref, skip_ref, o_ref, xbuf_ref, ct_ref, n_ref, m_ref):
    ci = pl.program_id(1)
    nb, span, _ = cur_ref.shape
    chunks = span // MLSTM_CHUNK

    @pl.when(ci == 0)
    def _():
        ct_ref[...] = jnp.zeros_like(ct_ref)
        n_ref[...] = jnp.zeros_like(n_ref)
        m_ref[...] = jnp.zeros_like(m_ref)

    for r in range(nb):
        xbuf_ref[r, 0:SUBLANES, :] = jnp.where(ci > 0, prev_ref[r], 0.0)
        xbuf_ref[r, SUBLANES:, :] = cur_ref[r]

    work = [(r, k) for k in range(chunks) for r in range(nb)]
    refs = (cw_ref, cb_ref, wqt_ref, wk_ref, wvt_ref, wif_ref, bif_ref)
    ahead = min(MLSTM_STAGE_AHEAD, len(work))
    staged = {n: _mlstm_gates(xbuf_ref.at[work[n][0]], work[n][1], *refs) for n in range(ahead)}
    for n, (r, k) in enumerate(work):
        if n + ahead < len(work):
            staged[n + ahead] = _mlstm_gates(xbuf_ref.at[work[n + ahead][0]], work[n + ahead][1], *refs)
        rows = slice(k * MLSTM_CHUNK, (k + 1) * MLSTM_CHUNK)
        _mlstm_recur(staged.pop(n), z_ref.at[r, rows], nrm_ref, skip_ref, o_ref.at[r, rows],
                     ct_ref.at[r], n_ref.at[r], m_ref.at[r])


def _mlstm_gates(xbuf_ref, k, cw_ref, cb_ref, wqt_ref, wk_ref, wvt_ref, wif_ref, bif_ref):
    L = MLSTM_CHUNK
    dh = MLSTM_HEAD_DIM
    nh = MLSTM_HEADS

    base = SUBLANES + k * L
    cur = xbuf_ref[base:base + L, :]
    conv = cb_ref[...]
    for j in range(MLSTM_CONV):
        off = base - (MLSTM_CONV - 1) + j
        conv = conv + cw_ref[j:j + 1, :] * xbuf_ref[off:off + L, :]
    xc = _silu(conv)

    qts, kss, vts = [], [], []
    ift = bif_ref[...]
    for h in range(nh):
        sl = slice(h * dh, (h + 1) * dh)
        xh = xc[:, sl].astype(BF16)
        vh = cur[:, sl].astype(BF16)
        qt = lax.dot_general(wqt_ref[h], xh, NT_DIMS, preferred_element_type=F32).astype(BF16)
        kk = jnp.dot(xh, wk_ref[h], preferred_element_type=F32)
        vt = lax.dot_general(wvt_ref[h], vh, NT_DIMS, preferred_element_type=F32).astype(BF16)
        ift = ift + jnp.dot(wif_ref[:, sl], qt, preferred_element_type=F32)
        ift = ift + lax.dot_general(wif_ref[:, MLSTM_WIDTH + h * dh:MLSTM_WIDTH + (h + 1) * dh],
                                    kk.astype(BF16), NT_DIMS, preferred_element_type=F32)
        ift = ift + jnp.dot(wif_ref[:, 2 * MLSTM_WIDTH + h * dh:2 * MLSTM_WIDTH + (h + 1) * dh], vt,
                            preferred_element_type=F32)
        qts.append(qt)
        kss.append((kk * (dh ** -0.5)).astype(BF16))
        vts.append(vt)

    key = lax.broadcasted_iota(jnp.int32, (L, L), 0)
    qry = lax.broadcasted_iota(jnp.int32, (L, L), 1)
    lf = _log_sigmoid(ift)
    tri = jnp.where(key <= qry, 1.0, 0.0).astype(BF16)
    b_all = jnp.zeros_like(lf)
    rem = lf
    for _ in range(3):
        part = rem.astype(BF16)
        b_all = b_all + jnp.dot(part, tri, preferred_element_type=F32)
        rem = rem - part.astype(F32)
    return dict(xc=xc, qts=qts, kss=kss, vts=vts, li_rows=ift[0:nh], b_rows=b_all[nh:2 * nh])


def _mlstm_recur(st, z_ref, nrm_ref, skip_ref, o_ref, ct_ref, n_ref, m_ref):
    L = MLSTM_CHUNK
    dh = MLSTM_HEAD_DIM
    key = lax.broadcasted_iota(jnp.int32, (L, L), 0)
    qry = lax.broadcasted_iota(jnp.int32, (L, L), 1)
    causal = key <= qry
    xc = st["xc"]

    heads = range(MLSTM_HEADS)
    qts, kss, vts = st["qts"], st["kss"], st["vts"]
    ct_state = [ct_ref[h] for h in heads]
    n_state = [n_ref[h] for h in heads]
    m_prev = [m_ref[h:h + 1, 0:1] for h in heads]
    qk = [jnp.dot(kss[h], qts[h], preferred_element_type=F32) for h in heads]
    cq = [jnp.dot(ct_state[h].astype(BF16), qts[h], preferred_element_type=F32) for h in heads]
    nq = [jnp.dot(n_state[h].astype(BF16), qts[h], preferred_element_type=F32)[0:1] for h in heads]

    li = [st["li_rows"][h:h + 1] for h in heads]
    b = [st["b_rows"][h:h + 1] for h in heads]
    a = [li[h] - b[h] for h in heads]
    a_keys = [jnp.broadcast_to(a[h], (L, L)).T for h in heads]
    d = [jnp.where(causal, b[h] + a_keys[h], -jnp.inf) for h in heads]
    inter = [b[h] + m_prev[h] for h in heads]
    m_t = [jnp.maximum(inter[h], jnp.max(d[h], axis=0, keepdims=True)) for h in heads]
    w_inter = [jnp.exp(inter[h] - m_t[h]) for h in heads]
    s = [qk[h] * jnp.exp(d[h] - m_t[h]) for h in heads]
    sv = [jnp.dot(vts[h], s[h].astype(BF16), preferred_element_type=F32) for h in heads]

    b_last = [b[h][:, L - 1:L] for h in heads]
    g = [b_last[h] + a[h] for h in heads]
    m_new = [jnp.maximum(b_last[h] + m_prev[h], jnp.max(g[h], axis=1, keepdims=True)) for h in heads]
    wk = [jnp.exp(g[h] - m_new[h]) for h in heads]
    dec = [jnp.exp(b_last[h] + m_prev[h] - m_new[h]) for h in heads]
    for h in heads:
        ct_ref[h] = dec[h] * ct_state[h] + jnp.dot((vts[h].astype(F32) * wk[h]).astype(BF16), kss[h],
                                                   preferred_element_type=F32)
        n_ref[h] = dec[h] * n_state[h] + jnp.dot(jnp.broadcast_to(wk[h], (SUBLANES, L)).astype(BF16),
                                                 kss[h], preferred_element_type=F32)
        m_ref[h:h + 1, :] = jnp.broadcast_to(m_new[h], (1, LANES))

    for h in heads:
        sl = slice(h * dh, (h + 1) * dh)
        num = w_inter[h] * cq[h] + sv[h]
        den = w_inter[h] * nq[h] + jnp.sum(s[h], axis=0, keepdims=True)
        hh = num * (1.0 / jnp.maximum(jnp.abs(den), jnp.exp(-m_t[h])))
        mu = jnp.mean(hh, axis=0, keepdims=True)
        cen = hh - mu
        var = jnp.mean(cen * cen, axis=0, keepdims=True)
        hn = (cen * lax.rsqrt(var + LN_EPS)).T * nrm_ref[:, sl]
        o_ref[:, sl] = ((hn + skip_ref[:, sl] * xc[:, sl]) * _silu(z_ref[:, sl])).astype(o_ref.dtype)


def _mlstm(mxz, conv_w, conv_b, wq, wk, wv, w_if_t, b_if, norm_w, skip_w, *, nb=2, chunks=4):
    b, s, _ = mxz.shape
    span = chunks * MLSTM_CHUNK
    assert b % nb == 0 and s % span == 0, (b, s)
    w = MLSTM_WIDTH
    rows_per_span = span // SUBLANES
    const2 = lambda bi, c: (0, 0)
    const3 = lambda bi, c: (0, 0, 0)
    return pl.pallas_call(
        _mlstm_kernel,
        grid=(b // nb, s // span),
        in_specs=[
            pl.BlockSpec((nb, span, w), lambda bi, c: (bi, c, 0)),
            pl.BlockSpec((nb, SUBLANES, w), lambda bi, c: (bi, jnp.maximum(c * rows_per_span - 1, 0), 0)),
            pl.BlockSpec((nb, span, w), lambda bi, c: (bi, c, 1)),
            pl.BlockSpec(conv_w.shape, const2),
            pl.BlockSpec(conv_b.shape, const2),
            pl.BlockSpec(wq.shape, const3),
            pl.BlockSpec(wk.shape, const3),
            pl.BlockSpec(wv.shape, const3),
            pl.BlockSpec(w_if_t.shape, const2),
            pl.BlockSpec(b_if.shape, const2),
            pl.BlockSpec(norm_w.shape, const2),
            pl.BlockSpec(skip_w.shape, const2),
        ],
        out_specs=pl.BlockSpec((nb, span, w), lambda bi, c: (bi, c, 0)),
        out_shape=jax.ShapeDtypeStruct((b, s, w), BF16),
        scratch_shapes=[
            pltpu.VMEM((nb, SUBLANES + span, w), F32),
            pltpu.VMEM((nb, MLSTM_HEADS, MLSTM_HEAD_DIM, MLSTM_HEAD_DIM), F32),
            pltpu.VMEM((nb, MLSTM_HEADS, SUBLANES, MLSTM_HEAD_DIM), F32),
            pltpu.VMEM((nb, SUBLANES, LANES), F32),
        ],
        compiler_params=_params("parallel", "arbitrary"),
        name="mlstm",
    )(mxz, mxz, mxz, conv_w, conv_b, wq, wk, wv, w_if_t, b_if, norm_w, skip_w)


def _merge_kernel(x_ref, nw_ref, wg_ref, gb_ref, od_ref, of_ref, om_ref, wbd_ref, wbf_ref, wbm_ref,
                  wo_ref, o_ref):
    x = x_ref[...]
    d = x.shape[1]
    xn = _rms_norm(x, nw_ref[...]).astype(BF16)
    merged = None
    for br, (b_ref, w_ref) in enumerate(((od_ref, wbd_ref), (of_ref, wbf_ref), (om_ref, wbm_ref))):
        cols = slice(br * d, (br + 1) * d)
        gate = jax.nn.sigmoid(jnp.dot(xn, wg_ref[:, cols], preferred_element_type=F32) + gb_ref[:, cols])
        term = gate * jnp.dot(b_ref[...], w_ref[...], preferred_element_type=F32)
        merged = term if merged is None else merged + term
    o_ref[...] = x + jnp.dot(merged.astype(BF16), wo_ref[...], preferred_element_type=F32)


def _merge(x, norm_w, w_g, gate_b, o_diff, o_fox, o_mlstm, wbd, wbf, wbm, w_out, *, tm=512):
    m, d = x.shape
    assert m % tm == 0, m
    const = lambda i: (0, 0)
    row = lambda i: (i, 0)
    return pl.pallas_call(
        _merge_kernel,
        grid=(m // tm,),
        in_specs=[
            pl.BlockSpec((tm, d), row),
            pl.BlockSpec((1, d), const),
            pl.BlockSpec(w_g.shape, const),
            pl.BlockSpec(gate_b.shape, const),
            pl.BlockSpec((tm, ATTN_WIDTH), row),
            pl.BlockSpec((tm, ATTN_WIDTH), row),
            pl.BlockSpec((tm, MLSTM_WIDTH), row),
            pl.BlockSpec(wbd.shape, const),
            pl.BlockSpec(wbf.shape, const),
            pl.BlockSpec(wbm.shape, const),
            pl.BlockSpec(w_out.shape, const),
        ],
        out_specs=pl.BlockSpec((tm, d), row),
        out_shape=jax.ShapeDtypeStruct((m, d), F32),
        compiler_params=_params("parallel"),
        name="merge",
    )(x, norm_w, w_g, gate_b, o_diff, o_fox, o_mlstm, wbd, wbf, wbm, w_out)


def _alibi_bias(seq):
    slopes = np.array([2.0 ** (-8.0 * (h + 1) / DIFF_HEADS) for h in range(DIFF_HEADS)], dtype=np.float32)
    pos = np.arange(seq, dtype=np.float32)
    per_head = slopes[:, None] * pos[None, :]
    return jnp.asarray(np.repeat(per_head[:, None, :], 2, axis=1))


def kernel(x, ffn1_norm, ffn1_w_gate, ffn1_w_up, ffn1_w_down, mix_norm, w_in, gate_bias, diff_lq1, diff_lk1, diff_lq2, diff_lk2, diff_subln, fox_b_f, mlstm_conv_w, mlstm_conv_b, mlstm_wq, mlstm_wk, mlstm_wv, mlstm_w_if, mlstm_b_if, mlstm_norm, mlstm_skip, w_branch_diff, w_branch_fox, w_branch_mlstm, w_out, ffn2_norm, ffn2_w_gate, ffn2_w_up, ffn2_w_down, final_norm):
    batch, seq, d = x.shape
    depth = w_in.shape[0]
    m = batch * seq
    bf = lambda a: a.astype(BF16)
    row = lambda a: a.reshape(1, -1)
    aw = ATTN_WIDTH
    ff0 = 6 * aw
    mx0 = ff0 + FOX_HEADS
    g0 = mx0 + 2 * MLSTM_WIDTH
    blocks = aw // LANES
    alibi = _alibi_bias(seq)
    w_in_b = bf(w_in)

    x = x.reshape(m, d)
    for l in range(depth):
        x = _ffn(x, row(ffn1_norm[l]), ffn1_w_gate, ffn1_w_up, ffn1_w_down, l)

        w_in_l = w_in_b[l]
        w_vt = jnp.concatenate([w_in_l[:, 2 * aw:3 * aw], w_in_l[:, 5 * aw:6 * aw]], axis=1).T
        qk, vt, mxz, cf = _proj(x, row(mix_norm[l]), w_in_b, l, w_vt, w_in_l[:, mx0:g0],
                                w_in_l[:, ff0:mx0].T, fox_b_f[l].reshape(-1, 1), seq=seq)
        qk = qk.reshape(batch, seq, 4 * aw)
        lam_init = 0.8 - 0.6 * math.exp(-0.3 * l)
        o_diff = _attention(
            functools.partial(_diff_attn_kernel, lam_init=lam_init), qk, vt, alibi,
            [row(diff_lq1[l]), row(diff_lk1[l]), row(diff_lq2[l]), row(diff_lk2[l]), row(diff_subln[l])],
            q_col=0, v_row=0, v_rows=LANES, bias_per_batch=False, name="diff_attn")
        o_fox = _attention(_fox_attn_kernel, qk, vt, cf.reshape(FOX_HEADS // 2, 2, m), [],
                           q_col=2 * blocks, v_row=blocks, v_rows=LANES // 2, bias_per_batch=True,
                           name="fox_attn")
        o_mlstm = _mlstm(mxz.reshape(batch, seq, 2 * MLSTM_WIDTH), mlstm_conv_w[l], row(mlstm_conv_b[l]),
                         bf(jnp.swapaxes(mlstm_wq[l], 1, 2)), bf(mlstm_wk[l]),
                         bf(jnp.swapaxes(mlstm_wv[l], 1, 2)), bf(mlstm_w_if[l].T),
                         mlstm_b_if[l].reshape(-1, 1), row(mlstm_norm[l]), row(mlstm_skip[l]))
        x = _merge(x, row(mix_norm[l]), w_in_l[:, g0:], gate_bias[l].reshape(1, -1),
                   o_diff.reshape(m, aw), o_fox.reshape(m, aw),
                   o_mlstm.reshape(m, MLSTM_WIDTH), bf(w_branch_diff[l]), bf(w_branch_fox[l]),
                   bf(w_branch_mlstm[l]), bf(w_out[l]))

        last = l == depth - 1
        x = _ffn(x, row(ffn2_norm[l]), ffn2_w_gate, ffn2_w_up, ffn2_w_down, l,
                 row(final_norm) if last else None)
    return x.reshape(batch, seq, d)
```

```python
import functools
import math

import numpy as np
import jax
import jax.numpy as jnp
from jax import lax
from jax.experimental import pallas as pl
from jax.experimental.pallas import tpu as pltpu

RMS_EPS = 1e-6
LN_EPS = 1e-5
DIFF_HEADS = 4
ATTN_HEAD_DIM = 64
FOX_HEADS = 8
MLSTM_HEADS = 4
MLSTM_HEAD_DIM = 128
MLSTM_WIDTH = MLSTM_HEADS * MLSTM_HEAD_DIM
MLSTM_CONV = 4
MLSTM_CHUNK = 128
MLSTM_STAGE_AHEAD = 2
ATTN_WIDTH = 512
LANES = 128
SUBLANES = 8
BF16_ROWS = 16
SCORE_SLOTS = 4
MX_BLOCK = 6 * ATTN_WIDTH // (2 * MLSTM_WIDTH)
GATE_BLOCK = MX_BLOCK + 1
BIAS_TERMS = 3
VMEM_LIMIT_BYTES = 56 * 1024 * 1024
LOG2E = math.log2(math.e)
Q_SCALE = ATTN_HEAD_DIM ** -0.5 * LOG2E

F32 = jnp.float32
BF16 = jnp.bfloat16
NT_DIMS = (((1,), (1,)), ((), ()))


def _params(*semantics):
    return pltpu.CompilerParams(dimension_semantics=semantics, vmem_limit_bytes=VMEM_LIMIT_BYTES)


def _rms_norm(x, w):
    return x * lax.rsqrt(jnp.mean(x * x, axis=-1, keepdims=True) + RMS_EPS) * w


def _silu(x):
    return x * jax.nn.sigmoid(x)


def _log_sigmoid(x):
    return jnp.minimum(x, 0.0) - jnp.log1p(jnp.exp(-jnp.abs(x)))


def _lane_cumsum(x):
    n = x.shape[-1]
    lane = lax.broadcasted_iota(jnp.int32, x.shape, 1)
    shift = 1
    while shift < n:
        x = x + jnp.where(lane >= shift, pltpu.roll(x, shift, axis=1), 0.0)
        shift *= 2
    return x


def _ffn_kernel(*refs, final):
    if final:
        x_ref, nw_ref, wg_ref, wu_ref, wd_ref, fn_ref, o_ref, xn_ref = refs
    else:
        x_ref, nw_ref, wg_ref, wu_ref, wd_ref, o_ref, xn_ref = refs
    f = pl.program_id(1)

    @pl.when(f == 0)
    def _():
        xn_ref[...] = _rms_norm(x_ref[...], nw_ref[...]).astype(BF16)
        o_ref[...] = jnp.zeros_like(o_ref)

    xn = xn_ref[...]
    g = jnp.dot(xn, wg_ref[...].astype(BF16), preferred_element_type=F32)
    u = jnp.dot(xn, wu_ref[...].astype(BF16), preferred_element_type=F32)
    h = (_silu(g) * u).astype(BF16)
    o_ref[...] += jnp.dot(h, wd_ref[...].astype(BF16), preferred_element_type=F32)

    @pl.when(f == pl.num_programs(1) - 1)
    def _():
        y = x_ref[...] + 0.5 * o_ref[...]
        if final:
            y = _rms_norm(y, fn_ref[...])
        o_ref[...] = y


def _ffn(x, norm_w, w_gate, w_up, w_down, layer, final_norm=None, *, tm=2048, tf=256):
    m, d = x.shape
    dff = w_gate.shape[2]
    assert m % tm == 0 and dff % tf == 0, (m, dff)
    final = final_norm is not None
    in_specs = [
        pl.BlockSpec((tm, d), lambda i, f: (i, 0)),
        pl.BlockSpec((1, d), lambda i, f: (0, 0)),
        pl.BlockSpec((None, d, tf), lambda i, f: (layer, 0, f)),
        pl.BlockSpec((None, d, tf), lambda i, f: (layer, 0, f)),
        pl.BlockSpec((None, tf, d), lambda i, f: (layer, f, 0)),
    ]
    args = [x, norm_w, w_gate, w_up, w_down]
    if final:
        in_specs.append(pl.BlockSpec((1, d), lambda i, f: (0, 0)))
        args.append(final_norm)
    return pl.pallas_call(
        functools.partial(_ffn_kernel, final=final),
        grid=(m // tm, dff // tf),
        in_specs=in_specs,
        out_specs=pl.BlockSpec((tm, d), lambda i, f: (i, 0)),
        out_shape=jax.ShapeDtypeStruct((m, d), F32),
        scratch_shapes=[pltpu.VMEM((tm, d), BF16)],
        compiler_params=_params("parallel", "arbitrary"),
        name="ffn_final" if final else "ffn",
    )(*args)


def _proj_kernel(x_ref, nw_ref, wdq_ref, wdk_ref, wfq_ref, wfk_ref, wvt_ref, wm_ref, wff_ref, bf_ref,
                 qk_ref, vt_ref, mxz_ref, cf_ref, carry_ref, *, tiles_per_seq):
    i = pl.program_id(0)

    @pl.when(i == 0)
    def _():
        carry_ref[...] = jnp.zeros_like(carry_ref)

    xn = _rms_norm(x_ref[...], nw_ref[...]).astype(BF16)

    ff = lax.dot_general(wff_ref[...], xn, NT_DIMS, preferred_element_type=F32)
    csum = _lane_cumsum(_log_sigmoid(ff + bf_ref[...]))
    csum = csum + jnp.where(i % tiles_per_seq == 0, 0.0, carry_ref[:, 0:1])
    carry_ref[...] = jnp.broadcast_to(csum[:, -1:], carry_ref.shape)
    cf_ref[...] = -csum

    for grp, w_ref in enumerate((wdq_ref, wdk_ref, wfq_ref, wfk_ref)):
        cols = slice(grp * ATTN_WIDTH, (grp + 1) * ATTN_WIDTH)
        r = jnp.dot(xn, w_ref[...], preferred_element_type=F32)
        if grp % 2 == 0:
            r = r * Q_SCALE
        qk_ref[:, cols] = r.astype(BF16)
    for grp in range(2):
        rows = slice(grp * ATTN_WIDTH, (grp + 1) * ATTN_WIDTH)
        vt_ref[rows, :] = lax.dot_general(wvt_ref[rows, :], xn, NT_DIMS,
                                          preferred_element_type=F32).astype(BF16)
    for grp in range(2):
        cols = slice(grp * MLSTM_WIDTH, (grp + 1) * MLSTM_WIDTH)
        mxz_ref[:, cols] = jnp.dot(xn, wm_ref[:, cols], preferred_element_type=F32)


def _proj(x, norm_w, w_in_b, layer, w_vt, w_ff_t, b_f, *, seq, tm=512):
    m, d = x.shape
    nh = w_ff_t.shape[0]
    assert seq % tm == 0 and m % seq == 0, (m, seq)
    const = lambda i: (0, 0)
    qk_groups = (0, 1, 3, 4)
    qk_specs = [pl.BlockSpec((None, d, ATTN_WIDTH), functools.partial(lambda i, g: (layer, 0, g), g=g))
                for g in qk_groups]
    return pl.pallas_call(
        functools.partial(_proj_kernel, tiles_per_seq=seq // tm),
        grid=(m // tm,),
        in_specs=[
            pl.BlockSpec((tm, d), lambda i: (i, 0)),
            pl.BlockSpec((1, d), const),
            *qk_specs,
            pl.BlockSpec(w_vt.shape, const),
            pl.BlockSpec((None, d, 2 * MLSTM_WIDTH), lambda i: (layer, 0, MX_BLOCK)),
            pl.BlockSpec(w_ff_t.shape, const),
            pl.BlockSpec((nh, 1), const),
        ],
        out_specs=[
            pl.BlockSpec((tm, len(qk_groups) * ATTN_WIDTH), lambda i: (i, 0)),
            pl.BlockSpec((w_vt.shape[0], tm), lambda i: (0, i)),
            pl.BlockSpec((tm, 2 * MLSTM_WIDTH), lambda i: (i, 0)),
            pl.BlockSpec((nh, tm), lambda i: (0, i)),
        ],
        out_shape=[
            jax.ShapeDtypeStruct((m, len(qk_groups) * ATTN_WIDTH), BF16),
            jax.ShapeDtypeStruct((w_vt.shape[0], m), BF16),
            jax.ShapeDtypeStruct((m, 2 * MLSTM_WIDTH), F32),
            jax.ShapeDtypeStruct((nh, m), F32),
        ],
        scratch_shapes=[pltpu.VMEM((nh, LANES), F32)],
        compiler_params=_params("arbitrary"),
        name="proj",
    )(x, norm_w, w_in_b, w_in_b, w_in_b, w_in_b, w_vt, w_in_b, w_ff_t, b_f)


def _flash_pair(q_ref, k_ref, vt_ref, bias_ref, kaug_ref, vaug_ref, qm_ref, s_ref, acc_ref, finish, *, tq, tk, v_rows):
    seq = k_ref.shape[1]
    dr = v_rows[0].stop - v_rows[0].start
    half = LANES // 2

    kaug_ref[:, 0:LANES] = k_ref[0]
    terms = []
    for c in range(2):
        rem = bias_ref[0, c:c + 1, :] * LOG2E
        for _ in range(BIAS_TERMS):
            part = rem.astype(BF16).astype(F32)
            terms.append(part)
            rem = rem - part
    term_rows = jnp.concatenate(terms + [jnp.zeros((LANES - 2 * BIAS_TERMS, seq), F32)], axis=0)
    for blk in range(seq // LANES):
        sl = slice(blk * LANES, (blk + 1) * LANES)
        kaug_ref[sl, LANES:] = term_rows[:, sl].T.astype(BF16)
    for c in range(2):
        vaug_ref[c, 0:dr, :] = vt_ref[v_rows[c], :]
        vaug_ref[c, dr:, :] = jnp.ones((BF16_ROWS, seq), BF16)

    qlane = lax.broadcasted_iota(jnp.int32, (seq, LANES), 1)
    q = q_ref[0]
    for c in range(2):
        qm_ref[c, :, 0:LANES] = jnp.where((qlane >= c * half) & (qlane < (c + 1) * half), q, jnp.zeros_like(q))
        picks_bias = (qlane >= BIAS_TERMS * c) & (qlane < BIAS_TERMS * (c + 1))
        qm_ref[c, :, LANES:] = jnp.where(picks_bias, 1.0, 0.0).astype(BF16)

    work = []
    for i in range(seq // tq):
        n_full = (i * tq) // tk
        lead = i * tq - n_full * tk
        work += [(i, j * tk, tk, None, j == 0, False) for j in range(n_full)]
        work.append((i, n_full * tk, lead + tq, lead, n_full == 0, True))

    def scores(n):
        i, k0, size, _, _, _ = work[n]
        for c in range(2):
            s_ref[n % s_ref.shape[0], c, 0:size, :] = lax.dot_general(
                kaug_ref[k0:k0 + size, :], qm_ref[c, i * tq:(i + 1) * tq, :], NT_DIMS,
                preferred_element_type=F32)

    scores(0)
    scores(1)
    m = [None, None]
    for n, (i, k0, size, lead, first, last) in enumerate(work):
        if n + 2 < len(work):
            scores(n + 2)
        for c in range(2):
            s = s_ref[n % s_ref.shape[0], c, 0:size, :]
            if lead is not None:
                key = lax.broadcasted_iota(jnp.int32, (tq, tq), 0)
                qry = lax.broadcasted_iota(jnp.int32, (tq, tq), 1)
                diag = jnp.where(key <= qry, s[lead:, :], -jnp.inf)
                s = diag if lead == 0 else jnp.concatenate([s[0:lead, :], diag], axis=0)
            m_blk = jnp.max(s, axis=0, keepdims=True)
            m_new = m_blk if first else jnp.maximum(m[c], m_blk)
            p = jnp.exp2(s - m_new).astype(BF16)
            pv = jnp.dot(vaug_ref[c, :, k0:k0 + size], p, preferred_element_type=F32)
            if first:
                acc_ref[i, c] = pv
            else:
                acc_ref[i, c] = jnp.exp2(m[c] - m_new) * acc_ref[i, c] + pv
            m[c] = m_new
        if last:
            finish(i, *[acc_ref[i, c, 0:dr, :] * (1.0 / acc_ref[i, c, dr:dr + 1, :]) for c in range(2)])


def _diff_attn_kernel(q_ref, k_ref, vt_ref, bias_ref, lq1_ref, lk1_ref, lq2_ref, lk2_ref, sub_ref,
                      o_ref, *scratch, tq, tk, lam_init):
    lam = (jnp.exp(jnp.sum(lq1_ref[...] * lk1_ref[...], axis=1, keepdims=True))
           - jnp.exp(jnp.sum(lq2_ref[...] * lk2_ref[...], axis=1, keepdims=True)) + lam_init)

    def finish(i, a1, a2):
        d = (a1 - lam * a2).T
        o_ref[0, i * tq:(i + 1) * tq, :] = (_rms_norm(d, sub_ref[...]) * (1.0 - lam_init)).astype(o_ref.dtype)

    full = slice(0, LANES)
    _flash_pair(q_ref, k_ref, vt_ref, bias_ref, *scratch, finish, tq=tq, tk=tk, v_rows=(full, full))


def _fox_attn_kernel(q_ref, k_ref, vt_ref, bias_ref, o_ref, *scratch, tq, tk):
    def finish(i, a_even, a_odd):
        o_ref[0, i * tq:(i + 1) * tq, :] = jnp.concatenate([a_even, a_odd], axis=0).T.astype(o_ref.dtype)

    half = LANES // 2
    _flash_pair(q_ref, k_ref, vt_ref, bias_ref, *scratch, finish, tq=tq, tk=tk,
                v_rows=(slice(0, half), slice(half, LANES)))


def _attention(kernel, qk, vt, bias, extra, *, q_col, v_row, v_rows, bias_per_batch, tq=256, tk=1024, name):
    b, s, _ = qk.shape
    assert s % tk == 0 and tk % tq == 0 and tq % LANES == 0, (s, tq, tk)
    groups = ATTN_WIDTH // LANES
    if bias_per_batch:
        bias_map = lambda bi, g: (g, 0, bi)
    else:
        bias_map = lambda bi, g: (g, 0, 0)
    in_specs = [
        pl.BlockSpec((1, s, LANES), lambda bi, g: (bi, 0, q_col + g)),
        pl.BlockSpec((1, s, LANES), lambda bi, g: (bi, 0, q_col + groups + g)),
        pl.BlockSpec((LANES, s), lambda bi, g: (v_row + g, bi)),
        pl.BlockSpec((1, 2, s), bias_map),
    ] + [pl.BlockSpec(e.shape, lambda bi, g: (0, 0)) for e in extra]
    rows = v_rows + BF16_ROWS
    return pl.pallas_call(
        functools.partial(kernel, tq=tq, tk=tk),
        grid=(b, groups),
        in_specs=in_specs,
        out_specs=pl.BlockSpec((1, s, LANES), lambda bi, g: (bi, 0, g)),
        out_shape=jax.ShapeDtypeStruct((b, s, ATTN_WIDTH), BF16),
        scratch_shapes=[pltpu.VMEM((s, 2 * LANES), BF16), pltpu.VMEM((2, rows, s), BF16),
                        pltpu.VMEM((2, s, 2 * LANES), BF16), pltpu.VMEM((SCORE_SLOTS, 2, tk, tq), F32),
                        pltpu.VMEM((s // tq, 2, rows, tq), F32)],
        compiler_params=_params("parallel", "parallel"),
        name=name,
    )(qk, qk, vt, bias, *extra)


def _mlstm_kernel(cur_ref, prev_ref, z_ref, cw_ref, cb_ref, wqt_ref, wk_ref, wvt_ref, wif_ref, bif_ref,
                  nrm_ref, skip_ref, o_ref, xbuf_ref, ct_ref, n_ref, m_ref):
    ci = pl.program_id(1)
    nb, span, _ = cur_ref.shape
    chunks = span // MLSTM_CHUNK

    @pl.when(ci == 0)
    def _():
        ct_ref[...] = jnp.zeros_like(ct_ref)
        n_ref[...] = jnp.zeros_like(n_ref)
        m_ref[...] = jnp.zeros_like(m_ref)

    for r in range(nb):
        xbuf_ref[r, 0:SUBLANES, :] = jnp.where(ci > 0, prev_ref[r], 0.0)
        xbuf_ref[r, SUBLANES:, :] = cur_ref[r]

    work = [(r, k) for k in range(chunks) for r in range(nb)]
    refs = (cw_ref, cb_ref, wqt_ref, wk_ref, wvt_ref, wif_ref, bif_ref)
    ahead = min(MLSTM_STAGE_AHEAD, len(work))
    staged = {n: _mlstm_gates(xbuf_ref.at[work[n][0]], work[n][1], *refs) for n in range(ahead)}
    for n, (r, k) in enumerate(work):
        if n + ahead < len(work):
            staged[n + ahead] = _mlstm_gates(xbuf_ref.at[work[n + ahead][0]], work[n + ahead][1], *refs)
        rows = slice(k * MLSTM_CHUNK, (k + 1) * MLSTM_CHUNK)
        _mlstm_recur(staged.pop(n), z_ref.at[r, rows], nrm_ref, skip_ref, o_ref.at[r, rows],
                     ct_ref.at[r], n_ref.at[r], m_ref.at[r])


def _mlstm_gates(xbuf_ref, k, cw_ref, cb_ref, wqt_ref, wk_ref, wvt_ref, wif_ref, bif_ref):
    L = MLSTM_CHUNK
    dh = MLSTM_HEAD_DIM
    nh = MLSTM_HEADS

    base = SUBLANES + k * L
    cur = xbuf_ref[base:base + L, :]
    conv = cb_ref[...]
    for j in range(MLSTM_CONV):
        off = base - (MLSTM_CONV - 1) + j
        conv = conv + cw_ref[j:j + 1, :] * xbuf_ref[off:off + L, :]
    xc = _silu(conv)

    qts, kss, vts = [], [], []
    ift = bif_ref[...]
    for h in range(nh):
        sl = slice(h * dh, (h + 1) * dh)
        xh = xc[:, sl].astype(BF16)
        vh = cur[:, sl].astype(BF16)
        qt = lax.dot_general(wqt_ref[h], xh, NT_DIMS, preferred_element_type=F32).astype(BF16)
        kk = jnp.dot(xh, wk_ref[h], preferred_element_type=F32)
        vt = lax.dot_general(wvt_ref[h], vh, NT_DIMS, preferred_element_type=F32).astype(BF16)
        ift = ift + jnp.dot(wif_ref[:, sl], qt, preferred_element_type=F32)
        ift = ift + lax.dot_general(wif_ref[:, MLSTM_WIDTH + h * dh:MLSTM_WIDTH + (h + 1) * dh],
                                    kk.astype(BF16), NT_DIMS, preferred_element_type=F32)
        ift = ift + jnp.dot(wif_ref[:, 2 * MLSTM_WIDTH + h * dh:2 * MLSTM_WIDTH + (h + 1) * dh], vt,
                            preferred_element_type=F32)
        qts.append(qt)
        kss.append((kk * (dh ** -0.5)).astype(BF16))
        vts.append(vt)

    key = lax.broadcasted_iota(jnp.int32, (L, L), 0)
    qry = lax.broadcasted_iota(jnp.int32, (L, L), 1)
    lf = _log_sigmoid(ift)
    tri = jnp.where(key <= qry, 1.0, 0.0).astype(BF16)
    b_all = jnp.zeros_like(lf)
    rem = lf
    for _ in range(3):
        part = rem.astype(BF16)
        b_all = b_all + jnp.dot(part, tri, preferred_element_type=F32)
        rem = rem - part.astype(F32)
    return dict(xc=xc, qts=qts, kss=kss, vts=vts, li_rows=ift[0:nh], b_rows=b_all[nh:2 * nh])


def _mlstm_recur(st, z_ref, nrm_ref, skip_ref, o_ref, ct_ref, n_ref, m_ref):
    L = MLSTM_CHUNK
    dh = MLSTM_HEAD_DIM
    key = lax.broadcasted_iota(jnp.int32, (L, L), 0)
    qry = lax.broadcasted_iota(jnp.int32, (L, L), 1)
    causal = key <= qry
    xc = st["xc"]

    heads = range(MLSTM_HEADS)
    qts, kss, vts = st["qts"], st["kss"], st["vts"]
    ct_state = [ct_ref[h] for h in heads]
    n_state = [n_ref[h] for h in heads]
    m_prev = [m_ref[h:h + 1, 0:1] for h in heads]
    qk = [jnp.dot(kss[h], qts[h], preferred_element_type=F32) for h in heads]
    cq = [jnp.dot(ct_state[h].astype(BF16), qts[h], preferred_element_type=F32) for h in heads]
    nq = [jnp.dot(n_state[h].astype(BF16), qts[h], preferred_element_type=F32)[0:1] for h in heads]

    li = [st["li_rows"][h:h + 1] for h in heads]
    b = [st["b_rows"][h:h + 1] for h in heads]
    a = [li[h] - b[h] for h in heads]
    a_keys = [jnp.broadcast_to(a[h], (L, L)).T for h in heads]
    d = [jnp.where(causal, b[h] + a_keys[h], -jnp.inf) for h in heads]
    inter = [b[h] + m_prev[h] for h in heads]
    m_t = [jnp.maximum(inter[h], jnp.max(d[h], axis=0, keepdims=True)) for h in heads]
    w_inter = [jnp.exp(inter[h] - m_t[h]) for h in heads]
    s = [qk[h] * jnp.exp(d[h] - m_t[h]) for h in heads]
    sv = [jnp.dot(vts[h], s[h].astype(BF16), preferred_element_type=F32) for h in heads]

    b_last = [b[h][:, L - 1:L] for h in heads]
    g = [b_last[h] + a[h] for h in heads]
    m_new = [jnp.maximum(b_last[h] + m_prev[h], jnp.max(g[h], axis=1, keepdims=True)) for h in heads]
    wk = [jnp.exp(g[h] - m_new[h]) for h in heads]
    dec = [jnp.exp(b_last[h] + m_prev[h] - m_new[h]) for h in heads]
    for h in heads:
        ct_ref[h] = dec[h] * ct_state[h] + jnp.dot((vts[h].astype(F32) * wk[h]).astype(BF16), kss[h],
                                                   preferred_element_type=F32)
        n_ref[h] = dec[h] * n_state[h] + jnp.dot(jnp.broadcast_to(wk[h], (SUBLANES, L)).astype(BF16),
                                                 kss[h], preferred_element_type=F32)
        m_ref[h:h + 1, :] = jnp.broadcast_to(m_new[h], (1, LANES))

    for h in heads:
        sl = slice(h * dh, (h + 1) * dh)
        num = w_inter[h] * cq[h] + sv[h]
        den = w_inter[h] * nq[h] + jnp.sum(s[h], axis=0, keepdims=True)
        hh = num * (1.0 / jnp.maximum(jnp.abs(den), jnp.exp(-m_t[h])))
        mu = jnp.mean(hh, axis=0, keepdims=True)
        cen = hh - mu
        var = jnp.mean(cen * cen, axis=0, keepdims=True)
        hn = (cen * lax.rsqrt(var + LN_EPS)).T * nrm_ref[:, sl]
        o_ref[:, sl] = ((hn + skip_ref[:, sl] * xc[:, sl]) * _silu(z_ref[:, sl])).astype(o_ref.dtype)


def _mlstm(mxz, conv_w, conv_b, wq, wk, wv, w_if_t, b_if, norm_w, skip_w, *, nb=2, chunks=4):
    b, s, _ = mxz.shape
    span = chunks * MLSTM_CHUNK
    assert b % nb == 0 and s % span == 0, (b, s)
    w = MLSTM_WIDTH
    rows_per_span = span // SUBLANES
    const2 = lambda bi, c: (0, 0)
    const3 = lambda bi, c: (0, 0, 0)
    return pl.pallas_call(
        _mlstm_kernel,
        grid=(b // nb, s // span),
        in_specs=[
            pl.BlockSpec((nb, span, w), lambda bi, c: (bi, c, 0)),
            pl.BlockSpec((nb, SUBLANES, w), lambda bi, c: (bi, jnp.maximum(c * rows_per_span - 1, 0), 0)),
            pl.BlockSpec((nb, span, w), lambda bi, c: (bi, c, 1)),
            pl.BlockSpec(conv_w.shape, const2),
            pl.BlockSpec(conv_b.shape, const2),
            pl.BlockSpec(wq.shape, const3),
            pl.BlockSpec(wk.shape, const3),
            pl.BlockSpec(wv.shape, const3),
            pl.BlockSpec(w_if_t.shape, const2),
            pl.BlockSpec(b_if.shape, const2),
            pl.BlockSpec(norm_w.shape, const2),
            pl.BlockSpec(skip_w.shape, const2),
        ],
        out_specs=pl.BlockSpec((nb, span, w), lambda bi, c: (bi, c, 0)),
        out_shape=jax.ShapeDtypeStruct((b, s, w), BF16),
        scratch_shapes=[
            pltpu.VMEM((nb, SUBLANES + span, w), F32),
            pltpu.VMEM((nb, MLSTM_HEADS, MLSTM_HEAD_DIM, MLSTM_HEAD_DIM), F32),
            pltpu.VMEM((nb, MLSTM_HEADS, SUBLANES, MLSTM_HEAD_DIM), F32),
            pltpu.VMEM((nb, SUBLANES, LANES), F32),
        ],
        compiler_params=_params("parallel", "arbitrary"),
        name="mlstm",
    )(mxz, mxz, mxz, conv_w, conv_b, wq, wk, wv, w_if_t, b_if, norm_w, skip_w)


def _merge_kernel(x_ref, nw_ref, wgd_ref, wgf_ref, wgm_ref, gb_ref, od_ref, of_ref, om_ref, wbd_ref, wbf_ref,
                  wbm_ref, wo_ref, o_ref):
    x = x_ref[...]
    d = x.shape[1]
    xn = _rms_norm(x, nw_ref[...]).astype(BF16)
    merged = None
    branches = ((od_ref, wbd_ref, wgd_ref), (of_ref, wbf_ref, wgf_ref), (om_ref, wbm_ref, wgm_ref))
    for br, (b_ref, w_ref, wg_ref) in enumerate(branches):
        cols = slice(br * d, (br + 1) * d)
        gate = jax.nn.sigmoid(jnp.dot(xn, wg_ref[...], preferred_element_type=F32) + gb_ref[:, cols])
        term = gate * jnp.dot(b_ref[...], w_ref[...], preferred_element_type=F32)
        merged = term if merged is None else merged + term
    o_ref[...] = x + jnp.dot(merged.astype(BF16), wo_ref[...], preferred_element_type=F32)


def _merge(x, norm_w, w_in_b, layer, gate_b, o_diff, o_fox, o_mlstm, wbd, wbf, wbm, w_out, *, tm=512):
    m, d = x.shape
    assert m % tm == 0, m
    const = lambda i: (0, 0)
    row = lambda i: (i, 0)
    return pl.pallas_call(
        _merge_kernel,
        grid=(m // tm,),
        in_specs=[
            pl.BlockSpec((tm, d), row),
            pl.BlockSpec((1, d), const),
            *[pl.BlockSpec((None, d, d), functools.partial(lambda i, g: (layer, 0, g), g=GATE_BLOCK + br))
              for br in range(3)],
            pl.BlockSpec(gate_b.shape, const),
            pl.BlockSpec((tm, ATTN_WIDTH), row),
            pl.BlockSpec((tm, ATTN_WIDTH), row),
            pl.BlockSpec((tm, MLSTM_WIDTH), row),
            pl.BlockSpec(wbd.shape, const),
            pl.BlockSpec(wbf.shape, const),
            pl.BlockSpec(wbm.shape, const),
            pl.BlockSpec(w_out.shape, const),
        ],
        out_specs=pl.BlockSpec((tm, d), row),
        out_shape=jax.ShapeDtypeStruct((m, d), F32),
        compiler_params=_params("parallel"),
        name="merge",
    )(x, norm_w, w_in_b, w_in_b, w_in_b, gate_b, o_diff, o_fox, o_mlstm, wbd, wbf, wbm, w_out)


def _alibi_bias(seq):
    slopes = np.array([2.0 ** (-8.0 * (h + 1) / DIFF_HEADS) for h in range(DIFF_HEADS)], dtype=np.float32)
    pos = np.arange(seq, dtype=np.float32)
    per_head = slopes[:, None] * pos[None, :]
    return jnp.asarray(np.repeat(per_head[:, None, :], 2, axis=1))


def kernel(x, ffn1_norm, ffn1_w_gate, ffn1_w_up, ffn1_w_down, mix_norm, w_in, gate_bias, diff_lq1, diff_lk1, diff_lq2, diff_lk2, diff_subln, fox_b_f, mlstm_conv_w, mlstm_conv_b, mlstm_wq, mlstm_wk, mlstm_wv, mlstm_w_if, mlstm_b_if, mlstm_norm, mlstm_skip, w_branch_diff, w_branch_fox, w_branch_mlstm, w_out, ffn2_norm, ffn2_w_gate, ffn2_w_up, ffn2_w_down, final_norm):
    batch, seq, d = x.shape
    depth = w_in.shape[0]
    m = batch * seq
    bf = lambda a: a.astype(BF16)
    row = lambda a: a.reshape(1, -1)
    aw = ATTN_WIDTH
    ff0 = 6 * aw
    mx0 = ff0 + FOX_HEADS
    blocks = aw // LANES
    alibi = _alibi_bias(seq)
    w_in_b = bf(jnp.concatenate([w_in[:, :, :ff0], w_in[:, :, mx0:], w_in[:, :, ff0:mx0]], axis=2))
    ff_col = w_in.shape[2] - FOX_HEADS

    x = x.reshape(m, d)
    for l in range(depth):
        x = _ffn(x, row(ffn1_norm[l]), ffn1_w_gate, ffn1_w_up, ffn1_w_down, l)

        w_in_l = w_in_b[l]
        w_vt = jnp.concatenate([w_in_l[:, 2 * aw:3 * aw], w_in_l[:, 5 * aw:6 * aw]], axis=1).T
        qk, vt, mxz, cf = _proj(x, row(mix_norm[l]), w_in_b, l, w_vt, w_in_l[:, ff_col:].T,
                                fox_b_f[l].reshape(-1, 1), seq=seq)
        qk = qk.reshape(batch, seq, 4 * aw)
        lam_init = 0.8 - 0.6 * math.exp(-0.3 * l)
        o_diff = _attention(
            functools.partial(_diff_attn_kernel, lam_init=lam_init), qk, vt, alibi,
            [row(diff_lq1[l]), row(diff_lk1[l]), row(diff_lq2[l]), row(diff_lk2[l]), row(diff_subln[l])],
            q_col=0, v_row=0, v_rows=LANES, bias_per_batch=False, name="diff_attn")
        o_fox = _attention(_fox_attn_kernel, qk, vt, cf.reshape(FOX_HEADS // 2, 2, m), [],
                           q_col=2 * blocks, v_row=blocks, v_rows=LANES // 2, bias_per_batch=True,
                           name="fox_attn")
        o_mlstm = _mlstm(mxz.reshape(batch, seq, 2 * MLSTM_WIDTH), mlstm_conv_w[l], row(mlstm_conv_b[l]),
                         bf(jnp.swapaxes(mlstm_wq[l], 1, 2)), bf(mlstm_wk[l]),
                         bf(jnp.swapaxes(mlstm_wv[l], 1, 2)), bf(mlstm_w_if[l].T),
                         mlstm_b_if[l].reshape(-1, 1), row(mlstm_norm[l]), row(mlstm_skip[l]))
        x = _merge(x, row(mix_norm[l]), w_in_b, l, gate_bias[l].reshape(1, -1),
                   o_diff.reshape(m, aw), o_fox.reshape(m, aw),
                   o_mlstm.reshape(m, MLSTM_WIDTH), bf(w_branch_diff[l]), bf(w_branch_fox[l]),
                   bf(w_branch_mlstm[l]), bf(w_out[l]))

        last = l == depth - 1
        x = _ffn(x, row(ffn2_norm[l]), ffn2_w_gate, ffn2_w_up, ffn2_w_down, l,
                 row(final_norm) if last else None)
    return x.reshape(batch, seq, d)
```

```python
import functools
import math

import numpy as np
import jax
import jax.numpy as jnp
from jax import lax
from jax.experimental import pallas as pl
from jax.experimental.pallas import tpu as pltpu

RMS_EPS = 1e-6
LN_EPS = 1e-5
DIFF_HEADS = 4
ATTN_HEAD_DIM = 64
FOX_HEADS = 8
MLSTM_HEADS = 4
MLSTM_HEAD_DIM = 128
MLSTM_WIDTH = MLSTM_HEADS * MLSTM_HEAD_DIM
MLSTM_CONV = 4
MLSTM_CHUNK = 128
MLSTM_STAGE_AHEAD = 2
ATTN_WIDTH = 512
LANES = 128
SUBLANES = 8
BF16_ROWS = 16
SCORE_SLOTS = 4
BIAS_TERMS = 3
VMEM_LIMIT_BYTES = 56 * 1024 * 1024
LOG2E = math.log2(math.e)
Q_SCALE = ATTN_HEAD_DIM ** -0.5 * LOG2E

F32 = jnp.float32
BF16 = jnp.bfloat16
NT_DIMS = (((1,), (1,)), ((), ()))


def _params(*semantics):
    return pltpu.CompilerParams(dimension_semantics=semantics, vmem_limit_bytes=VMEM_LIMIT_BYTES)


def _rms_norm(x, w):
    return x * lax.rsqrt(jnp.mean(x * x, axis=-1, keepdims=True) + RMS_EPS) * w


def _silu(x):
    return x * jax.nn.sigmoid(x)


def _log_sigmoid(x):
    return jnp.minimum(x, 0.0) - jnp.log1p(jnp.exp(-jnp.abs(x)))


def _lane_cumsum(x):
    n = x.shape[-1]
    lane = lax.broadcasted_iota(jnp.int32, x.shape, 1)
    shift = 1
    while shift < n:
        x = x + jnp.where(lane >= shift, pltpu.roll(x, shift, axis=1), 0.0)
        shift *= 2
    return x


def _ffn_kernel(*refs, final):
    if final:
        x_ref, nw_ref, wg_ref, wu_ref, wd_ref, fn_ref, o_ref, xn_ref = refs
    else:
        x_ref, nw_ref, wg_ref, wu_ref, wd_ref, o_ref, xn_ref = refs
    f = pl.program_id(1)

    @pl.when(f == 0)
    def _():
        xn_ref[...] = _rms_norm(x_ref[...], nw_ref[...]).astype(BF16)
        o_ref[...] = jnp.zeros_like(o_ref)

    xn = xn_ref[...]
    g = jnp.dot(xn, wg_ref[...].astype(BF16), preferred_element_type=F32)
    u = jnp.dot(xn, wu_ref[...].astype(BF16), preferred_element_type=F32)
    h = (_silu(g) * u).astype(BF16)
    o_ref[...] += jnp.dot(h, wd_ref[...].astype(BF16), preferred_element_type=F32)

    @pl.when(f == pl.num_programs(1) - 1)
    def _():
        y = x_ref[...] + 0.5 * o_ref[...]
        if final:
            y = _rms_norm(y, fn_ref[...])
        o_ref[...] = y


def _ffn(x, norm_w, w_gate, w_up, w_down, layer, final_norm=None, *, tm=2048, tf=256):
    m, d = x.shape
    dff = w_gate.shape[2]
    assert m % tm == 0 and dff % tf == 0, (m, dff)
    final = final_norm is not None
    in_specs = [
        pl.BlockSpec((tm, d), lambda i, f: (i, 0)),
        pl.BlockSpec((1, d), lambda i, f: (0, 0)),
        pl.BlockSpec((None, d, tf), lambda i, f: (layer, 0, f)),
        pl.BlockSpec((None, d, tf), lambda i, f: (layer, 0, f)),
        pl.BlockSpec((None, tf, d), lambda i, f: (layer, f, 0)),
    ]
    args = [x, norm_w, w_gate, w_up, w_down]
    if final:
        in_specs.append(pl.BlockSpec((1, d), lambda i, f: (0, 0)))
        args.append(final_norm)
    return pl.pallas_call(
        functools.partial(_ffn_kernel, final=final),
        grid=(m // tm, dff // tf),
        in_specs=in_specs,
        out_specs=pl.BlockSpec((tm, d), lambda i, f: (i, 0)),
        out_shape=jax.ShapeDtypeStruct((m, d), F32),
        scratch_shapes=[pltpu.VMEM((tm, d), BF16)],
        compiler_params=_params("parallel", "arbitrary"),
        name="ffn_final" if final else "ffn",
    )(*args)


def _proj_kernel(x_ref, nw_ref, wdq_ref, wdk_ref, wfq_ref, wfk_ref, wvt_ref, wm_ref, wff_ref, bf_ref,
                 qk_ref, vt_ref, mxz_ref, cf_ref, carry_ref, *, tiles_per_seq):
    i = pl.program_id(0)

    @pl.when(i == 0)
    def _():
        carry_ref[...] = jnp.zeros_like(carry_ref)

    xn = _rms_norm(x_ref[...], nw_ref[...]).astype(BF16)

    ff = lax.dot_general(wff_ref[...], xn, NT_DIMS, preferred_element_type=F32)
    csum = _lane_cumsum(_log_sigmoid(ff + bf_ref[...]))
    csum = csum + jnp.where(i % tiles_per_seq == 0, 0.0, carry_ref[:, 0:1])
    carry_ref[...] = jnp.broadcast_to(csum[:, -1:], carry_ref.shape)
    cf_ref[...] = -csum

    for grp, w_ref in enumerate((wdq_ref, wdk_ref, wfq_ref, wfk_ref)):
        cols = slice(grp * ATTN_WIDTH, (grp + 1) * ATTN_WIDTH)
        r = jnp.dot(xn, w_ref[...], preferred_element_type=F32)
        if grp % 2 == 0:
            r = r * Q_SCALE
        qk_ref[:, cols] = r.astype(BF16)
    for grp in range(2):
        rows = slice(grp * ATTN_WIDTH, (grp + 1) * ATTN_WIDTH)
        vt_ref[rows, :] = lax.dot_general(wvt_ref[rows, :], xn, NT_DIMS,
                                          preferred_element_type=F32).astype(BF16)
    for grp in range(2):
        cols = slice(grp * MLSTM_WIDTH, (grp + 1) * MLSTM_WIDTH)
        mxz_ref[:, cols] = jnp.dot(xn, wm_ref[:, cols], preferred_element_type=F32)


def _proj(x, norm_w, w_in_b, layer, w_vt, w_m, w_ff_t, b_f, *, seq, tm=1024):
    m, d = x.shape
    nh = w_ff_t.shape[0]
    assert seq % tm == 0 and m % seq == 0, (m, seq)
    const = lambda i: (0, 0)
    qk_groups = (0, 1, 3, 4)
    qk_specs = [pl.BlockSpec((None, d, ATTN_WIDTH), functools.partial(lambda i, g: (layer, 0, g), g=g))
                for g in qk_groups]
    return pl.pallas_call(
        functools.partial(_proj_kernel, tiles_per_seq=seq // tm),
        grid=(m // tm,),
        in_specs=[
            pl.BlockSpec((tm, d), lambda i: (i, 0)),
            pl.BlockSpec((1, d), const),
            *qk_specs,
            pl.BlockSpec(w_vt.shape, const),
            pl.BlockSpec(w_m.shape, const),
            pl.BlockSpec(w_ff_t.shape, const),
            pl.BlockSpec((nh, 1), const),
        ],
        out_specs=[
            pl.BlockSpec((tm, len(qk_groups) * ATTN_WIDTH), lambda i: (i, 0)),
            pl.BlockSpec((w_vt.shape[0], tm), lambda i: (0, i)),
            pl.BlockSpec((tm, w_m.shape[1]), lambda i: (i, 0)),
            pl.BlockSpec((nh, tm), lambda i: (0, i)),
        ],
        out_shape=[
            jax.ShapeDtypeStruct((m, len(qk_groups) * ATTN_WIDTH), BF16),
            jax.ShapeDtypeStruct((w_vt.shape[0], m), BF16),
            jax.ShapeDtypeStruct((m, w_m.shape[1]), F32),
            jax.ShapeDtypeStruct((nh, m), F32),
        ],
        scratch_shapes=[pltpu.VMEM((nh, LANES), F32)],
        compiler_params=_params("arbitrary"),
        name="proj",
    )(x, norm_w, w_in_b, w_in_b, w_in_b, w_in_b, w_vt, w_m, w_ff_t, b_f)


def _flash_pair(q_ref, k_ref, vt_ref, bias_ref, kaug_ref, vaug_ref, qm_ref, s_ref, acc_ref, finish, *, tq, tk, v_rows):
    seq = k_ref.shape[1]
    dr = v_rows[0].stop - v_rows[0].start
    half = LANES // 2

    kaug_ref[:, 0:LANES] = k_ref[0]
    terms = []
    for c in range(2):
        rem = bias_ref[0, c:c + 1, :] * LOG2E
        for _ in range(BIAS_TERMS):
            part = rem.astype(BF16).astype(F32)
            terms.append(part)
            rem = rem - part
    term_rows = jnp.concatenate(terms + [jnp.zeros((LANES - 2 * BIAS_TERMS, seq), F32)], axis=0)
    for blk in range(seq // LANES):
        sl = slice(blk * LANES, (blk + 1) * LANES)
        kaug_ref[sl, LANES:] = term_rows[:, sl].T.astype(BF16)
    for c in range(2):
        vaug_ref[c, 0:dr, :] = vt_ref[v_rows[c], :]
        vaug_ref[c, dr:, :] = jnp.ones((BF16_ROWS, seq), BF16)

    qlane = lax.broadcasted_iota(jnp.int32, (seq, LANES), 1)
    q = q_ref[0]
    for c in range(2):
        qm_ref[c, :, 0:LANES] = jnp.where((qlane >= c * half) & (qlane < (c + 1) * half), q, jnp.zeros_like(q))
        picks_bias = (qlane >= BIAS_TERMS * c) & (qlane < BIAS_TERMS * (c + 1))
        qm_ref[c, :, LANES:] = jnp.where(picks_bias, 1.0, 0.0).astype(BF16)

    work = []
    for i in range(seq // tq):
        n_full = (i * tq) // tk
        lead = i * tq - n_full * tk
        work += [(i, j * tk, tk, None, j == 0, False) for j in range(n_full)]
        work.append((i, n_full * tk, lead + tq, lead, n_full == 0, True))

    def scores(n):
        i, k0, size, _, _, _ = work[n]
        for c in range(2):
            s_ref[n % s_ref.shape[0], c, 0:size, :] = lax.dot_general(
                kaug_ref[k0:k0 + size, :], qm_ref[c, i * tq:(i + 1) * tq, :], NT_DIMS,
                preferred_element_type=F32)

    scores(0)
    scores(1)
    m = [None, None]
    for n, (i, k0, size, lead, first, last) in enumerate(work):
        if n + 2 < len(work):
            scores(n + 2)
        for c in range(2):
            s = s_ref[n % s_ref.shape[0], c, 0:size, :]
            if lead is not None:
                key = lax.broadcasted_iota(jnp.int32, (tq, tq), 0)
                qry = lax.broadcasted_iota(jnp.int32, (tq, tq), 1)
                diag = jnp.where(key <= qry, s[lead:, :], -jnp.inf)
                s = diag if lead == 0 else jnp.concatenate([s[0:lead, :], diag], axis=0)
            m_blk = jnp.max(s, axis=0, keepdims=True)
            m_new = m_blk if first else jnp.maximum(m[c], m_blk)
            p = jnp.exp2(s - m_new).astype(BF16)
            pv = jnp.dot(vaug_ref[c, :, k0:k0 + size], p, preferred_element_type=F32)
            if first:
                acc_ref[i, c] = pv
            else:
                acc_ref[i, c] = jnp.exp2(m[c] - m_new) * acc_ref[i, c] + pv
            m[c] = m_new
        if last:
            finish(i, *[acc_ref[i, c, 0:dr, :] * (1.0 / acc_ref[i, c, dr:dr + 1, :]) for c in range(2)])


def _diff_attn_kernel(q_ref, k_ref, vt_ref, bias_ref, lq1_ref, lk1_ref, lq2_ref, lk2_ref, sub_ref,
                      o_ref, *scratch, tq, tk, lam_init):
    lam = (jnp.exp(jnp.sum(lq1_ref[...] * lk1_ref[...], axis=1, keepdims=True))
           - jnp.exp(jnp.sum(lq2_ref[...] * lk2_ref[...], axis=1, keepdims=True)) + lam_init)

    def finish(i, a1, a2):
        d = (a1 - lam * a2).T
        o_ref[0, i * tq:(i + 1) * tq, :] = (_rms_norm(d, sub_ref[...]) * (1.0 - lam_init)).astype(o_ref.dtype)

    full = slice(0, LANES)
    _flash_pair(q_ref, k_ref, vt_ref, bias_ref, *scratch, finish, tq=tq, tk=tk, v_rows=(full, full))


def _fox_attn_kernel(q_ref, k_ref, vt_ref, bias_ref, o_ref, *scratch, tq, tk):
    def finish(i, a_even, a_odd):
        o_ref[0, i * tq:(i + 1) * tq, :] = jnp.concatenate([a_even, a_odd], axis=0).T.astype(o_ref.dtype)

    half = LANES // 2
    _flash_pair(q_ref, k_ref, vt_ref, bias_ref, *scratch, finish, tq=tq, tk=tk,
                v_rows=(slice(0, half), slice(half, LANES)))


def _attention(kernel, qk, vt, bias, extra, *, q_col, v_row, v_rows, bias_per_batch, tq=256, tk=1024, name):
    b, s, _ = qk.shape
    assert s % tk == 0 and tk % tq == 0 and tq % LANES == 0, (s, tq, tk)
    groups = ATTN_WIDTH // LANES
    if bias_per_batch:
        bias_map = lambda bi, g: (g, 0, bi)
    else:
        bias_map = lambda bi, g: (g, 0, 0)
    in_specs = [
        pl.BlockSpec((1, s, LANES), lambda bi, g: (bi, 0, q_col + g)),
        pl.BlockSpec((1, s, LANES), lambda bi, g: (bi, 0, q_col + groups + g)),
        pl.BlockSpec((LANES, s), lambda bi, g: (v_row + g, bi)),
        pl.BlockSpec((1, 2, s), bias_map),
    ] + [pl.BlockSpec(e.shape, lambda bi, g: (0, 0)) for e in extra]
    rows = v_rows + BF16_ROWS
    return pl.pallas_call(
        functools.partial(kernel, tq=tq, tk=tk),
        grid=(b, groups),
        in_specs=in_specs,
        out_specs=pl.BlockSpec((1, s, LANES), lambda bi, g: (bi, 0, g)),
        out_shape=jax.ShapeDtypeStruct((b, s, ATTN_WIDTH), BF16),
        scratch_shapes=[pltpu.VMEM((s, 2 * LANES), BF16), pltpu.VMEM((2, rows, s), BF16),
                        pltpu.VMEM((2, s, 2 * LANES), BF16), pltpu.VMEM((SCORE_SLOTS, 2, tk, tq), F32),
                        pltpu.VMEM((s // tq, 2, rows, tq), F32)],
        compiler_params=_params("parallel", "parallel"),
        name=name,
    )(qk, qk, vt, bias, *extra)


def _mlstm_kernel(cur_ref, prev_ref, z_ref, cw_ref, cb_ref, wqt_ref, wk_ref, wvt_ref, wif_ref, bif_ref,
                  nrm_ref, skip_ref, o_ref, xbuf_ref, ct_ref, n_ref, m_ref):
    ci = pl.program_id(1)
    nb, span, _ = cur_ref.shape
    chunks = span // MLSTM_CHUNK

    @pl.when(ci == 0)
    def _():
        ct_ref[...] = jnp.zeros_like(ct_ref)
        n_ref[...] = jnp.zeros_like(n_ref)
        m_ref[...] = jnp.zeros_like(m_ref)

    for r in range(nb):
        xbuf_ref[r, 0:SUBLANES, :] = jnp.where(ci > 0, prev_ref[r], 0.0)
        xbuf_ref[r, SUBLANES:, :] = cur_ref[r]

    work = [(r, k) for k in range(chunks) for r in range(nb)]
    refs = (cw_ref, cb_ref, wqt_ref, wk_ref, wvt_ref, wif_ref, bif_ref)
    ahead = min(MLSTM_STAGE_AHEAD, len(work))
    staged = {n: _mlstm_gates(xbuf_ref.at[work[n][0]], work[n][1], *refs) for n in range(ahead)}
    for n, (r, k) in enumerate(work):
        if n + ahead < len(work):
            staged[n + ahead] = _mlstm_gates(xbuf_ref.at[work[n + ahead][0]], work[n + ahead][1], *refs)
        rows = slice(k * MLSTM_CHUNK, (k + 1) * MLSTM_CHUNK)
        _mlstm_recur(staged.pop(n), z_ref.at[r, rows], nrm_ref, skip_ref, o_ref.at[r, rows],
                     ct_ref.at[r], n_ref.at[r], m_ref.at[r])


def _mlstm_gates(xbuf_ref, k, cw_ref, cb_ref, wqt_ref, wk_ref, wvt_ref, wif_ref, bif_ref):
    L = MLSTM_CHUNK
    dh = MLSTM_HEAD_DIM
    nh = MLSTM_HEADS

    base = SUBLANES + k * L
    cur = xbuf_ref[base:base + L, :]
    conv = cb_ref[...]
    for j in range(MLSTM_CONV):
        off = base - (MLSTM_CONV - 1) + j
        conv = conv + cw_ref[j:j + 1, :] * xbuf_ref[off:off + L, :]
    xc = _silu(conv)

    qts, kss, vts = [], [], []
    ift = bif_ref[...]
    for h in range(nh):
        sl = slice(h * dh, (h + 1) * dh)
        xh = xc[:, sl].astype(BF16)
        vh = cur[:, sl].astype(BF16)
        qt = lax.dot_general(wqt_ref[h], xh, NT_DIMS, preferred_element_type=F32).astype(BF16)
        kk = jnp.dot(xh, wk_ref[h], preferred_element_type=F32)
        vt = lax.dot_general(wvt_ref[h], vh, NT_DIMS, preferred_element_type=F32).astype(BF16)
        ift = ift + jnp.dot(wif_ref[:, sl], qt, preferred_element_type=F32)
        ift = ift + lax.dot_general(wif_ref[:, MLSTM_WIDTH + h * dh:MLSTM_WIDTH + (h + 1) * dh],
                                    kk.astype(BF16), NT_DIMS, preferred_element_type=F32)
        ift = ift + jnp.dot(wif_ref[:, 2 * MLSTM_WIDTH + h * dh:2 * MLSTM_WIDTH + (h + 1) * dh], vt,
                            preferred_element_type=F32)
        qts.append(qt)
        kss.append((kk * (dh ** -0.5)).astype(BF16))
        vts.append(vt)

    key = lax.broadcasted_iota(jnp.int32, (L, L), 0)
    qry = lax.broadcasted_iota(jnp.int32, (L, L), 1)
    lf = _log_sigmoid(ift)
    tri = jnp.where(key <= qry, 1.0, 0.0).astype(BF16)
    b_all = jnp.zeros_like(lf)
    rem = lf
    for _ in range(3):
        part = rem.astype(BF16)
        b_all = b_all + jnp.dot(part, tri, preferred_element_type=F32)
        rem = rem - part.astype(F32)
    return dict(xc=xc, qts=qts, kss=kss, vts=vts, li_rows=ift[0:nh], b_rows=b_all[nh:2 * nh])


def _mlstm_recur(st, z_ref, nrm_ref, skip_ref, o_ref, ct_ref, n_ref, m_ref):
    L = MLSTM_CHUNK
    dh = MLSTM_HEAD_DIM
    key = lax.broadcasted_iota(jnp.int32, (L, L), 0)
    qry = lax.broadcasted_iota(jnp.int32, (L, L), 1)
    causal = key <= qry
    xc = st["xc"]

    heads = range(MLSTM_HEADS)
    qts, kss, vts = st["qts"], st["kss"], st["vts"]
    ct_state = [ct_ref[h] for h in heads]
    n_state = [n_ref[h] for h in heads]
    m_prev = [m_ref[h:h + 1, 0:1] for h in heads]
    qk = [jnp.dot(kss[h], qts[h], preferred_element_type=F32) for h in heads]
    cq = [jnp.dot(ct_state[h].astype(BF16), qts[h], preferred_element_type=F32) for h in heads]
    nq = [jnp.dot(n_state[h].astype(BF16), qts[h], preferred_element_type=F32)[0:1] for h in heads]

    li = [st["li_rows"][h:h + 1] for h in heads]
    b = [st["b_rows"][h:h + 1] for h in heads]
    a = [li[h] - b[h] for h in heads]
    a_keys = [jnp.broadcast_to(a[h], (L, L)).T for h in heads]
    d = [jnp.where(causal, b[h] + a_keys[h], -jnp.inf) for h in heads]
    inter = [b[h] + m_prev[h] for h in heads]
    m_t = [jnp.maximum(inter[h], jnp.max(d[h], axis=0, keepdims=True)) for h in heads]
    w_inter = [jnp.exp(inter[h] - m_t[h]) for h in heads]
    s = [qk[h] * jnp.exp(d[h] - m_t[h]) for h in heads]
    sv = [jnp.dot(vts[h], s[h].astype(BF16), preferred_element_type=F32) for h in heads]

    b_last = [b[h][:, L - 1:L] for h in heads]
    g = [b_last[h] + a[h] for h in heads]
    m_new = [jnp.maximum(b_last[h] + m_prev[h], jnp.max(g[h], axis=1, keepdims=True)) for h in heads]
    wk = [jnp.exp(g[h] - m_new[h]) for h in heads]
    dec = [jnp.exp(b_last[h] + m_prev[h] - m_new[h]) for h in heads]
    for h in heads:
        ct_ref[h] = dec[h] * ct_state[h] + jnp.dot((vts[h].astype(F32) * wk[h]).astype(BF16), kss[h],
                                                   preferred_element_type=F32)
        n_ref[h] = dec[h] * n_state[h] + jnp.dot(jnp.broadcast_to(wk[h], (SUBLANES, L)).astype(BF16),
                                                 kss[h], preferred_element_type=F32)
        m_ref[h:h + 1, :] = jnp.broadcast_to(m_new[h], (1, LANES))

    for h in heads:
        sl = slice(h * dh, (h + 1) * dh)
        num = w_inter[h] * cq[h] + sv[h]
        den = w_inter[h] * nq[h] + jnp.sum(s[h], axis=0, keepdims=True)
        hh = num * (1.0 / jnp.maximum(jnp.abs(den), jnp.exp(-m_t[h])))
        mu = jnp.mean(hh, axis=0, keepdims=True)
        cen = hh - mu
        var = jnp.mean(cen * cen, axis=0, keepdims=True)
        hn = (cen * lax.rsqrt(var + LN_EPS)).T * nrm_ref[:, sl]
        o_ref[:, sl] = ((hn + skip_ref[:, sl] * xc[:, sl]) * _silu(z_ref[:, sl])).astype(o_ref.dtype)


def _mlstm(mxz, conv_w, conv_b, wq, wk, wv, w_if_t, b_if, norm_w, skip_w, *, nb=2, chunks=4):
    b, s, _ = mxz.shape
    span = chunks * MLSTM_CHUNK
    assert b % nb == 0 and s % span == 0, (b, s)
    w = MLSTM_WIDTH
    rows_per_span = span // SUBLANES
    const2 = lambda bi, c: (0, 0)
    const3 = lambda bi, c: (0, 0, 0)
    return pl.pallas_call(
        _mlstm_kernel,
        grid=(b // nb, s // span),
        in_specs=[
            pl.BlockSpec((nb, span, w), lambda bi, c: (bi, c, 0)),
            pl.BlockSpec((nb, SUBLANES, w), lambda bi, c: (bi, jnp.maximum(c * rows_per_span - 1, 0), 0)),
            pl.BlockSpec((nb, span, w), lambda bi, c: (bi, c, 1)),
            pl.BlockSpec(conv_w.shape, const2),
            pl.BlockSpec(conv_b.shape, const2),
            pl.BlockSpec(wq.shape, const3),
            pl.BlockSpec(wk.shape, const3),
            pl.BlockSpec(wv.shape, const3),
            pl.BlockSpec(w_if_t.shape, const2),
            pl.BlockSpec(b_if.shape, const2),
            pl.BlockSpec(norm_w.shape, const2),
            pl.BlockSpec(skip_w.shape, const2),
        ],
        out_specs=pl.BlockSpec((nb, span, w), lambda bi, c: (bi, c, 0)),
        out_shape=jax.ShapeDtypeStruct((b, s, w), BF16),
        scratch_shapes=[
            pltpu.VMEM((nb, SUBLANES + span, w), F32),
            pltpu.VMEM((nb, MLSTM_HEADS, MLSTM_HEAD_DIM, MLSTM_HEAD_DIM), F32),
            pltpu.VMEM((nb, MLSTM_HEADS, SUBLANES, MLSTM_HEAD_DIM), F32),
            pltpu.VMEM((nb, SUBLANES, LANES), F32),
        ],
        compiler_params=_params("parallel", "arbitrary"),
        name="mlstm",
    )(mxz, mxz, mxz, conv_w, conv_b, wq, wk, wv, w_if_t, b_if, norm_w, skip_w)


def _merge_kernel(x_ref, nw_ref, wg_ref, gb_ref, od_ref, of_ref, om_ref, wbd_ref, wbf_ref, wbm_ref,
                  wo_ref, o_ref):
    x = x_ref[...]
    d = x.shape[1]
    xn = _rms_norm(x, nw_ref[...]).astype(BF16)
    merged = None
    for br, (b_ref, w_ref) in enumerate(((od_ref, wbd_ref), (of_ref, wbf_ref), (om_ref, wbm_ref))):
        cols = slice(br * d, (br + 1) * d)
        gate = jax.nn.sigmoid(jnp.dot(xn, wg_ref[:, cols], preferred_element_type=F32) + gb_ref[:, cols])
        term = gate * jnp.dot(b_ref[...], w_ref[...], preferred_element_type=F32)
        merged = term if merged is None else merged + term
    o_ref[...] = x + jnp.dot(merged.astype(BF16), wo_ref[...], preferred_element_type=F32)


def _merge(x, norm_w, w_g, gate_b, o_diff, o_fox, o_mlstm, wbd, wbf, wbm, w_out, *, tm=1024):
    m, d = x.shape
    assert m % tm == 0, m
    const = lambda i: (0, 0)
    row = lambda i: (i, 0)
    return pl.pallas_call(
        _merge_kernel,
        grid=(m // tm,),
        in_specs=[
            pl.BlockSpec((tm, d), row),
            pl.BlockSpec((1, d), const),
            pl.BlockSpec(w_g.shape, const),
            pl.BlockSpec(gate_b.shape, const),
            pl.BlockSpec((tm, ATTN_WIDTH), row),
            pl.BlockSpec((tm, ATTN_WIDTH), row),
            pl.BlockSpec((tm, MLSTM_WIDTH), row),
            pl.BlockSpec(wbd.shape, const),
            pl.BlockSpec(wbf.shape, const),
            pl.BlockSpec(wbm.shape, const),
            pl.BlockSpec(w_out.shape, const),
        ],
        out_specs=pl.BlockSpec((tm, d), row),
        out_shape=jax.ShapeDtypeStruct((m, d), F32),
        compiler_params=_params("parallel"),
        name="merge",
    )(x, norm_w, w_g, gate_b, o_diff, o_fox, o_mlstm, wbd, wbf, wbm, w_out)


def _alibi_bias(seq):
    slopes = np.array([2.0 ** (-8.0 * (h + 1) / DIFF_HEADS) for h in range(DIFF_HEADS)], dtype=np.float32)
    pos = np.arange(seq, dtype=np.float32)
    per_head = slopes[:, None] * pos[None, :]
    return jnp.asarray(np.repeat(per_head[:, None, :], 2, axis=1))


def kernel(x, ffn1_norm, ffn1_w_gate, ffn1_w_up, ffn1_w_down, mix_norm, w_in, gate_bias, diff_lq1, diff_lk1, diff_lq2, diff_lk2, diff_subln, fox_b_f, mlstm_conv_w, mlstm_conv_b, mlstm_wq, mlstm_wk, mlstm_wv, mlstm_w_if, mlstm_b_if, mlstm_norm, mlstm_skip, w_branch_diff, w_branch_fox, w_branch_mlstm, w_out, ffn2_norm, ffn2_w_gate, ffn2_w_up, ffn2_w_down, final_norm):
    batch, seq, d = x.shape
    depth = w_in.shape[0]
    m = batch * seq
    bf = lambda a: a.astype(BF16)
    row = lambda a: a.reshape(1, -1)
    aw = ATTN_WIDTH
    ff0 = 6 * aw
    mx0 = ff0 + FOX_HEADS
    g0 = mx0 + 2 * MLSTM_WIDTH
    blocks = aw // LANES
    alibi = _alibi_bias(seq)
    w_in_b = bf(w_in)

    x = x.reshape(m, d)
    for l in range(depth):
        x = _ffn(x, row(ffn1_norm[l]), ffn1_w_gate, ffn1_w_up, ffn1_w_down, l)

        w_in_l = w_in_b[l]
        w_vt = jnp.concatenate([w_in_l[:, 2 * aw:3 * aw], w_in_l[:, 5 * aw:6 * aw]], axis=1).T
        qk, vt, mxz, cf = _proj(x, row(mix_norm[l]), w_in_b, l, w_vt, w_in_l[:, mx0:g0],
                                w_in_l[:, ff0:mx0].T, fox_b_f[l].reshape(-1, 1), seq=seq)
        qk = qk.reshape(batch, seq, 4 * aw)
        lam_init = 0.8 - 0.6 * math.exp(-0.3 * l)
        o_diff = _attention(
            functools.partial(_diff_attn_kernel, lam_init=lam_init), qk, vt, alibi,
            [row(diff_lq1[l]), row(diff_lk1[l]), row(diff_lq2[l]), row(diff_lk2[l]), row(diff_subln[l])],
            q_col=0, v_row=0, v_rows=LANES, bias_per_batch=False, name="diff_attn")
        o_fox = _attention(_fox_attn_kernel, qk, vt, cf.reshape(FOX_HEADS // 2, 2, m), [],
                           q_col=2 * blocks, v_row=blocks, v_rows=LANES // 2, bias_per_batch=True,
                           name="fox_attn")
        o_mlstm = _mlstm(mxz.reshape(batch, seq, 2 * MLSTM_WIDTH), mlstm_conv_w[l], row(mlstm_conv_b[l]),
                         bf(jnp.swapaxes(mlstm_wq[l], 1, 2)), bf(mlstm_wk[l]),
                         bf(jnp.swapaxes(mlstm_wv[l], 1, 2)), bf(mlstm_w_if[l].T),
                         mlstm_b_if[l].reshape(-1, 1), row(mlstm_norm[l]), row(mlstm_skip[l]))
        x = _merge(x, row(mix_norm[l]), w_in_l[:, g0:], gate_bias[l].reshape(1, -1),
                   o_diff.reshape(m, aw), o_fox.reshape(m, aw),
                   o_mlstm.reshape(m, MLSTM_WIDTH), bf(w_branch_diff[l]), bf(w_branch_fox[l]),
                   bf(w_branch_mlstm[l]), bf(w_out[l]))

        last = l == depth - 1
        x = _ffn(x, row(ffn2_norm[l]), ffn2_w_gate, ffn2_w_up, ffn2_w_down, l,
                 row(final_norm) if last else None)
    return x.reshape(batch, seq, d)
```

```python
import functools
import math

import numpy as np
import jax
import jax.numpy as jnp
from jax import lax
from jax.experimental import pallas as pl
from jax.experimental.pallas import tpu as pltpu

RMS_EPS = 1e-6
LN_EPS = 1e-5
DIFF_HEADS = 4
ATTN_HEAD_DIM = 64
FOX_HEADS = 8
MLSTM_HEADS = 4
MLSTM_HEAD_DIM = 128
MLSTM_WIDTH = MLSTM_HEADS * MLSTM_HEAD_DIM
MLSTM_CONV = 4
MLSTM_CHUNK = 128
MLSTM_STAGE_AHEAD = 2
ATTN_WIDTH = 512
LANES = 128
SUBLANES = 8
BF16_ROWS = 16
SCORE_SLOTS = 4
BIAS_TERMS = 3
VMEM_LIMIT_BYTES = 56 * 1024 * 1024
LOG2E = math.log2(math.e)
Q_SCALE = ATTN_HEAD_DIM ** -0.5 * LOG2E

F32 = jnp.float32
BF16 = jnp.bfloat16
NT_DIMS = (((1,), (1,)), ((), ()))


def _params(*semantics):
    return pltpu.CompilerParams(dimension_semantics=semantics, vmem_limit_bytes=VMEM_LIMIT_BYTES)


def _rms_norm(x, w):
    return x * lax.rsqrt(jnp.mean(x * x, axis=-1, keepdims=True) + RMS_EPS) * w


def _silu(x):
    half = 0.5 * x
    return half + half * jnp.tanh(half)


def _log_sigmoid(x):
    return jnp.minimum(x, 0.0) - jnp.log1p(jnp.exp(-jnp.abs(x)))


def _lane_cumsum(x):
    n = x.shape[-1]
    lane = lax.broadcasted_iota(jnp.int32, x.shape, 1)
    shift = 1
    while shift < n:
        x = x + jnp.where(lane >= shift, pltpu.roll(x, shift, axis=1), 0.0)
        shift *= 2
    return x


def _ffn_kernel(*refs, final):
    if final:
        x_ref, nw_ref, wg_ref, wu_ref, wd_ref, fn_ref, o_ref, xn_ref = refs
    else:
        x_ref, nw_ref, wg_ref, wu_ref, wd_ref, o_ref, xn_ref = refs
    f = pl.program_id(1)

    @pl.when(f == 0)
    def _():
        xn_ref[...] = _rms_norm(x_ref[...], nw_ref[...]).astype(BF16)
        o_ref[...] = jnp.zeros_like(o_ref)

    xn = xn_ref[...]
    g = jnp.dot(xn, wg_ref[...].astype(BF16), preferred_element_type=F32)
    u = jnp.dot(xn, wu_ref[...].astype(BF16), preferred_element_type=F32)
    h = (_silu(g) * u).astype(BF16)
    o_ref[...] += jnp.dot(h, wd_ref[...].astype(BF16), preferred_element_type=F32)

    @pl.when(f == pl.num_programs(1) - 1)
    def _():
        y = x_ref[...] + 0.5 * o_ref[...]
        if final:
            y = _rms_norm(y, fn_ref[...])
        o_ref[...] = y


def _ffn(x, norm_w, w_gate, w_up, w_down, layer, final_norm=None, *, tm=2048, tf=256):
    m, d = x.shape
    dff = w_gate.shape[2]
    assert m % tm == 0 and dff % tf == 0, (m, dff)
    final = final_norm is not None
    in_specs = [
        pl.BlockSpec((tm, d), lambda i, f: (i, 0)),
        pl.BlockSpec((1, d), lambda i, f: (0, 0)),
        pl.BlockSpec((None, d, tf), lambda i, f: (layer, 0, f)),
        pl.BlockSpec((None, d, tf), lambda i, f: (layer, 0, f)),
        pl.BlockSpec((None, tf, d), lambda i, f: (layer, f, 0)),
    ]
    args = [x, norm_w, w_gate, w_up, w_down]
    if final:
        in_specs.append(pl.BlockSpec((1, d), lambda i, f: (0, 0)))
        args.append(final_norm)
    return pl.pallas_call(
        functools.partial(_ffn_kernel, final=final),
        grid=(m // tm, dff // tf),
        in_specs=in_specs,
        out_specs=pl.BlockSpec((tm, d), lambda i, f: (i, 0)),
        out_shape=jax.ShapeDtypeStruct((m, d), F32),
        scratch_shapes=[pltpu.VMEM((tm, d), BF16)],
        compiler_params=_params("parallel", "arbitrary"),
        name="ffn_final" if final else "ffn",
    )(*args)


def _proj_kernel(x_ref, nw_ref, wdq_ref, wdk_ref, wfq_ref, wfk_ref, wvt_ref, wm_ref, wff_ref, bf_ref,
                 qk_ref, vt_ref, mxz_ref, cf_ref, carry_ref, *, tiles_per_seq):
    i = pl.program_id(0)

    @pl.when(i == 0)
    def _():
        carry_ref[...] = jnp.zeros_like(carry_ref)

    xn = _rms_norm(x_ref[...], nw_ref[...]).astype(BF16)

    ff = lax.dot_general(wff_ref[...], xn, NT_DIMS, preferred_element_type=F32)
    csum = _lane_cumsum(_log_sigmoid(ff + bf_ref[...]))
    csum = csum + jnp.where(i % tiles_per_seq == 0, 0.0, carry_ref[:, 0:1])
    carry_ref[...] = jnp.broadcast_to(csum[:, -1:], carry_ref.shape)
    cf_ref[...] = -csum

    for grp, w_ref in enumerate((wdq_ref, wdk_ref, wfq_ref, wfk_ref)):
        cols = slice(grp * ATTN_WIDTH, (grp + 1) * ATTN_WIDTH)
        r = jnp.dot(xn, w_ref[...], preferred_element_type=F32)
        if grp % 2 == 0:
            r = r * Q_SCALE
        qk_ref[:, cols] = r.astype(BF16)
    for grp in range(2):
        rows = slice(grp * ATTN_WIDTH, (grp + 1) * ATTN_WIDTH)
        vt_ref[rows, :] = lax.dot_general(wvt_ref[rows, :], xn, NT_DIMS,
                                          preferred_element_type=F32).astype(BF16)
    for grp in range(2):
        cols = slice(grp * MLSTM_WIDTH, (grp + 1) * MLSTM_WIDTH)
        mxz_ref[:, cols] = jnp.dot(xn, wm_ref[:, cols], preferred_element_type=F32)


def _proj(x, norm_w, w_in_b, layer, w_vt, w_m, w_ff_t, b_f, *, seq, tm=1024):
    m, d = x.shape
    nh = w_ff_t.shape[0]
    assert seq % tm == 0 and m % seq == 0, (m, seq)
    const = lambda i: (0, 0)
    qk_groups = (0, 1, 3, 4)
    qk_specs = [pl.BlockSpec((None, d, ATTN_WIDTH), functools.partial(lambda i, g: (layer, 0, g), g=g))
                for g in qk_groups]
    return pl.pallas_call(
        functools.partial(_proj_kernel, tiles_per_seq=seq // tm),
        grid=(m // tm,),
        in_specs=[
            pl.BlockSpec((tm, d), lambda i: (i, 0)),
            pl.BlockSpec((1, d), const),
            *qk_specs,
            pl.BlockSpec(w_vt.shape, const),
            pl.BlockSpec(w_m.shape, const),
            pl.BlockSpec(w_ff_t.shape, const),
            pl.BlockSpec((nh, 1), const),
        ],
        out_specs=[
            pl.BlockSpec((tm, len(qk_groups) * ATTN_WIDTH), lambda i: (i, 0)),
            pl.BlockSpec((w_vt.shape[0], tm), lambda i: (0, i)),
            pl.BlockSpec((tm, w_m.shape[1]), lambda i: (i, 0)),
            pl.BlockSpec((nh, tm), lambda i: (0, i)),
        ],
        out_shape=[
            jax.ShapeDtypeStruct((m, len(qk_groups) * ATTN_WIDTH), BF16),
            jax.ShapeDtypeStruct((w_vt.shape[0], m), BF16),
            jax.ShapeDtypeStruct((m, w_m.shape[1]), F32),
            jax.ShapeDtypeStruct((nh, m), F32),
        ],
        scratch_shapes=[pltpu.VMEM((nh, LANES), F32)],
        compiler_params=_params("arbitrary"),
        name="proj",
    )(x, norm_w, w_in_b, w_in_b, w_in_b, w_in_b, w_vt, w_m, w_ff_t, b_f)


def _flash_pair(q_ref, k_ref, vt_ref, bias_ref, kaug_ref, vaug_ref, qm_ref, s_ref, acc_ref, finish, *, tq, tk, v_rows):
    seq = k_ref.shape[1]
    dr = v_rows[0].stop - v_rows[0].start
    half = LANES // 2

    kaug_ref[:, 0:LANES] = k_ref[0]
    terms = []
    for c in range(2):
        rem = bias_ref[0, c:c + 1, :] * LOG2E
        for _ in range(BIAS_TERMS):
            part = rem.astype(BF16).astype(F32)
            terms.append(part)
            rem = rem - part
    term_rows = jnp.concatenate(terms + [jnp.zeros((LANES - 2 * BIAS_TERMS, seq), F32)], axis=0)
    for blk in range(seq // LANES):
        sl = slice(blk * LANES, (blk + 1) * LANES)
        kaug_ref[sl, LANES:] = term_rows[:, sl].T.astype(BF16)
    for c in range(2):
        vaug_ref[c, 0:dr, :] = vt_ref[v_rows[c], :]
        vaug_ref[c, dr:, :] = jnp.ones((BF16_ROWS, seq), BF16)

    qlane = lax.broadcasted_iota(jnp.int32, (seq, LANES), 1)
    q = q_ref[0]
    for c in range(2):
        qm_ref[c, :, 0:LANES] = jnp.where((qlane >= c * half) & (qlane < (c + 1) * half), q, jnp.zeros_like(q))
        picks_bias = (qlane >= BIAS_TERMS * c) & (qlane < BIAS_TERMS * (c + 1))
        qm_ref[c, :, LANES:] = jnp.where(picks_bias, 1.0, 0.0).astype(BF16)

    work = []
    for i in range(seq // tq):
        n_full = (i * tq) // tk
        lead = i * tq - n_full * tk
        work += [(i, j * tk, tk, None, j == 0, False) for j in range(n_full)]
        work.append((i, n_full * tk, lead + tq, lead, n_full == 0, True))

    def scores(n):
        i, k0, size, _, _, _ = work[n]
        for c in range(2):
            s_ref[n % s_ref.shape[0], c, 0:size, :] = lax.dot_general(
                kaug_ref[k0:k0 + size, :], qm_ref[c, i * tq:(i + 1) * tq, :], NT_DIMS,
                preferred_element_type=F32)

    scores(0)
    scores(1)
    m = [None, None]
    for n, (i, k0, size, lead, first, last) in enumerate(work):
        if n + 2 < len(work):
            scores(n + 2)
        for c in range(2):
            s = s_ref[n % s_ref.shape[0], c, 0:size, :]
            if lead is not None:
                key = lax.broadcasted_iota(jnp.int32, (tq, tq), 0)
                qry = lax.broadcasted_iota(jnp.int32, (tq, tq), 1)
                diag = jnp.where(key <= qry, s[lead:, :], -jnp.inf)
                s = diag if lead == 0 else jnp.concatenate([s[0:lead, :], diag], axis=0)
            m_blk = jnp.max(s, axis=0, keepdims=True)
            m_new = m_blk if first else jnp.maximum(m[c], m_blk)
            p = jnp.exp2(s - m_new).astype(BF16)
            pv = jnp.dot(vaug_ref[c, :, k0:k0 + size], p, preferred_element_type=F32)
            if first:
                acc_ref[i, c] = pv
            else:
                acc_ref[i, c] = jnp.exp2(m[c] - m_new) * acc_ref[i, c] + pv
            m[c] = m_new
        if last:
            finish(i, *[acc_ref[i, c, 0:dr, :] * (1.0 / acc_ref[i, c, dr:dr + 1, :]) for c in range(2)])


def _diff_attn_kernel(q_ref, k_ref, vt_ref, bias_ref, lq1_ref, lk1_ref, lq2_ref, lk2_ref, sub_ref,
                      o_ref, *scratch, tq, tk, lam_init):
    lam = (jnp.exp(jnp.sum(lq1_ref[...] * lk1_ref[...], axis=1, keepdims=True))
           - jnp.exp(jnp.sum(lq2_ref[...] * lk2_ref[...], axis=1, keepdims=True)) + lam_init)

    def finish(i, a1, a2):
        d = (a1 - lam * a2).T
        o_ref[0, i * tq:(i + 1) * tq, :] = (_rms_norm(d, sub_ref[...]) * (1.0 - lam_init)).astype(o_ref.dtype)

    full = slice(0, LANES)
    _flash_pair(q_ref, k_ref, vt_ref, bias_ref, *scratch, finish, tq=tq, tk=tk, v_rows=(full, full))


def _fox_attn_kernel(q_ref, k_ref, vt_ref, bias_ref, o_ref, *scratch, tq, tk):
    def finish(i, a_even, a_odd):
        o_ref[0, i * tq:(i + 1) * tq, :] = jnp.concatenate([a_even, a_odd], axis=0).T.astype(o_ref.dtype)

    half = LANES // 2
    _flash_pair(q_ref, k_ref, vt_ref, bias_ref, *scratch, finish, tq=tq, tk=tk,
                v_rows=(slice(0, half), slice(half, LANES)))


def _attention(kernel, qk, vt, bias, extra, *, q_col, v_row, v_rows, bias_per_batch, tq=256, tk=1024, name):
    b, s, _ = qk.shape
    assert s % tk == 0 and tk % tq == 0 and tq % LANES == 0, (s, tq, tk)
    groups = ATTN_WIDTH // LANES
    if bias_per_batch:
        bias_map = lambda bi, g: (g, 0, bi)
    else:
        bias_map = lambda bi, g: (g, 0, 0)
    in_specs = [
        pl.BlockSpec((1, s, LANES), lambda bi, g: (bi, 0, q_col + g)),
        pl.BlockSpec((1, s, LANES), lambda bi, g: (bi, 0, q_col + groups + g)),
        pl.BlockSpec((LANES, s), lambda bi, g: (v_row + g, bi)),
        pl.BlockSpec((1, 2, s), bias_map),
    ] + [pl.BlockSpec(e.shape, lambda bi, g: (0, 0)) for e in extra]
    rows = v_rows + BF16_ROWS
    return pl.pallas_call(
        functools.partial(kernel, tq=tq, tk=tk),
        grid=(b, groups),
        in_specs=in_specs,
        out_specs=pl.BlockSpec((1, s, LANES), lambda bi, g: (bi, 0, g)),
        out_shape=jax.ShapeDtypeStruct((b, s, ATTN_WIDTH), BF16),
        scratch_shapes=[pltpu.VMEM((s, 2 * LANES), BF16), pltpu.VMEM((2, rows, s), BF16),
                        pltpu.VMEM((2, s, 2 * LANES), BF16), pltpu.VMEM((SCORE_SLOTS, 2, tk, tq), F32),
                        pltpu.VMEM((s // tq, 2, rows, tq), F32)],
        compiler_params=_params("parallel", "parallel"),
        name=name,
    )(qk, qk, vt, bias, *extra)


def _mlstm_kernel(cur_ref, prev_ref, z_ref, cw_ref, cb_ref, wqt_ref, wk_ref, wvt_ref, wif_ref, bif_ref,
                  nrm_ref, skip_ref, o_ref, xbuf_ref, ct_ref, n_ref, m_ref):
    ci = pl.program_id(1)
    nb, span, _ = cur_ref.shape
    chunks = span // MLSTM_CHUNK

    @pl.when(ci == 0)
    def _():
        ct_ref[...] = jnp.zeros_like(ct_ref)
        n_ref[...] = jnp.zeros_like(n_ref)
        m_ref[...] = jnp.zeros_like(m_ref)

    for r in range(nb):
        xbuf_ref[r, 0:SUBLANES, :] = jnp.where(ci > 0, prev_ref[r], 0.0)
        xbuf_ref[r, SUBLANES:, :] = cur_ref[r]

    work = [(r, k) for k in range(chunks) for r in range(nb)]
    refs = (cw_ref, cb_ref, wqt_ref, wk_ref, wvt_ref, wif_ref, bif_ref)
    ahead = min(MLSTM_STAGE_AHEAD, len(work))
    staged = {n: _mlstm_gates(xbuf_ref.at[work[n][0]], work[n][1], *refs) for n in range(ahead)}
    for n, (r, k) in enumerate(work):
        if n + ahead < len(work):
            staged[n + ahead] = _mlstm_gates(xbuf_ref.at[work[n + ahead][0]], work[n + ahead][1], *refs)
        rows = slice(k * MLSTM_CHUNK, (k + 1) * MLSTM_CHUNK)
        _mlstm_recur(staged.pop(n), z_ref.at[r, rows], nrm_ref, skip_ref, o_ref.at[r, rows],
                     ct_ref.at[r], n_ref.at[r], m_ref.at[r])


def _mlstm_gates(xbuf_ref, k, cw_ref, cb_ref, wqt_ref, wk_ref, wvt_ref, wif_ref, bif_ref):
    L = MLSTM_CHUNK
    dh = MLSTM_HEAD_DIM
    nh = MLSTM_HEADS

    base = SUBLANES + k * L
    cur = xbuf_ref[base:base + L, :]
    conv = cb_ref[...]
    for j in range(MLSTM_CONV):
        off = base - (MLSTM_CONV - 1) + j
        conv = conv + cw_ref[j:j + 1, :] * xbuf_ref[off:off + L, :]
    xc = _silu(conv)

    qts, kss, vts = [], [], []
    ift = bif_ref[...]
    for h in range(nh):
        sl = slice(h * dh, (h + 1) * dh)
        xh = xc[:, sl].astype(BF16)
        vh = cur[:, sl].astype(BF16)
        qt = lax.dot_general(wqt_ref[h], xh, NT_DIMS, preferred_element_type=F32).astype(BF16)
        kk = jnp.dot(xh, wk_ref[h], preferred_element_type=F32)
        vt = lax.dot_general(wvt_ref[h], vh, NT_DIMS, preferred_element_type=F32).astype(BF16)
        ift = ift + jnp.dot(wif_ref[:, sl], qt, preferred_element_type=F32)
        ift = ift + lax.dot_general(wif_ref[:, MLSTM_WIDTH + h * dh:MLSTM_WIDTH + (h + 1) * dh],
                                    kk.astype(BF16), NT_DIMS, preferred_element_type=F32)
        ift = ift + jnp.dot(wif_ref[:, 2 * MLSTM_WIDTH + h * dh:2 * MLSTM_WIDTH + (h + 1) * dh], vt,
                            preferred_element_type=F32)
        qts.append(qt)
        kss.append((kk * (dh ** -0.5)).astype(BF16))
        vts.append(vt)

    key = lax.broadcasted_iota(jnp.int32, (L, L), 0)
    qry = lax.broadcasted_iota(jnp.int32, (L, L), 1)
    lf = _log_sigmoid(ift)
    tri = jnp.where(key <= qry, 1.0, 0.0).astype(BF16)
    b_all = jnp.zeros_like(lf)
    rem = lf
    for _ in range(3):
        part = rem.astype(BF16)
        b_all = b_all + jnp.dot(part, tri, preferred_element_type=F32)
        rem = rem - part.astype(F32)
    return dict(xc=xc, qts=qts, kss=kss, vts=vts, li_rows=ift[0:nh], b_rows=b_all[nh:2 * nh])


def _mlstm_recur(st, z_ref, nrm_ref, skip_ref, o_ref, ct_ref, n_ref, m_ref):
    L = MLSTM_CHUNK
    dh = MLSTM_HEAD_DIM
    key = lax.broadcasted_iota(jnp.int32, (L, L), 0)
    qry = lax.broadcasted_iota(jnp.int32, (L, L), 1)
    causal = key <= qry
    xc = st["xc"]

    heads = range(MLSTM_HEADS)
    qts, kss, vts = st["qts"], st["kss"], st["vts"]
    ct_state = [ct_ref[h] for h in heads]
    n_state = [n_ref[h] for h in heads]
    m_prev = [m_ref[h:h + 1, 0:1] for h in heads]
    qk = [jnp.dot(kss[h], qts[h], preferred_element_type=F32) for h in heads]
    cq = [jnp.dot(ct_state[h].astype(BF16), qts[h], preferred_element_type=F32) for h in heads]
    nq = [jnp.dot(n_state[h].astype(BF16), qts[h], preferred_element_type=F32)[0:1] for h in heads]

    li = [st["li_rows"][h:h + 1] for h in heads]
    b = [st["b_rows"][h:h + 1] for h in heads]
    a = [li[h] - b[h] for h in heads]
    a_keys = [jnp.broadcast_to(a[h], (L, L)).T for h in heads]
    d = [jnp.where(causal, b[h] + a_keys[h], -jnp.inf) for h in heads]
    inter = [b[h] + m_prev[h] for h in heads]
    m_t = [jnp.maximum(inter[h], jnp.max(d[h], axis=0, keepdims=True)) for h in heads]
    w_inter = [jnp.exp(inter[h] - m_t[h]) for h in heads]
    s = [qk[h] * jnp.exp(d[h] - m_t[h]) for h in heads]
    sv = [jnp.dot(vts[h], s[h].astype(BF16), preferred_element_type=F32) for h in heads]

    b_last = [b[h][:, L - 1:L] for h in heads]
    g = [b_last[h] + a[h] for h in heads]
    m_new = [jnp.maximum(b_last[h] + m_prev[h], jnp.max(g[h], axis=1, keepdims=True)) for h in heads]
    wk = [jnp.exp(g[h] - m_new[h]) for h in heads]
    dec = [jnp.exp(b_last[h] + m_prev[h] - m_new[h]) for h in heads]
    for h in heads:
        ct_ref[h] = dec[h] * ct_state[h] + jnp.dot((vts[h].astype(F32) * wk[h]).astype(BF16), kss[h],
                                                   preferred_element_type=F32)
        n_ref[h] = dec[h] * n_state[h] + jnp.dot(jnp.broadcast_to(wk[h], (SUBLANES, L)).astype(BF16),
                                                 kss[h], preferred_element_type=F32)
        m_ref[h:h + 1, :] = jnp.broadcast_to(m_new[h], (1, LANES))

    for h in heads:
        sl = slice(h * dh, (h + 1) * dh)
        num = w_inter[h] * cq[h] + sv[h]
        den = w_inter[h] * nq[h] + jnp.sum(s[h], axis=0, keepdims=True)
        hh = num * (1.0 / jnp.maximum(jnp.abs(den), jnp.exp(-m_t[h])))
        mu = jnp.mean(hh, axis=0, keepdims=True)
        cen = hh - mu
        var = jnp.mean(cen * cen, axis=0, keepdims=True)
        hn = (cen * lax.rsqrt(var + LN_EPS)).T * nrm_ref[:, sl]
        o_ref[:, sl] = ((hn + skip_ref[:, sl] * xc[:, sl]) * _silu(z_ref[:, sl])).astype(o_ref.dtype)


def _mlstm(mxz, conv_w, conv_b, wq, wk, wv, w_if_t, b_if, norm_w, skip_w, *, nb=2, chunks=4):
    b, s, _ = mxz.shape
    span = chunks * MLSTM_CHUNK
    assert b % nb == 0 and s % span == 0, (b, s)
    w = MLSTM_WIDTH
    rows_per_span = span // SUBLANES
    const2 = lambda bi, c: (0, 0)
    const3 = lambda bi, c: (0, 0, 0)
    return pl.pallas_call(
        _mlstm_kernel,
        grid=(b // nb, s // span),
        in_specs=[
            pl.BlockSpec((nb, span, w), lambda bi, c: (bi, c, 0)),
            pl.BlockSpec((nb, SUBLANES, w), lambda bi, c: (bi, jnp.maximum(c * rows_per_span - 1, 0), 0)),
            pl.BlockSpec((nb, span, w), lambda bi, c: (bi, c, 1)),
            pl.BlockSpec(conv_w.shape, const2),
            pl.BlockSpec(conv_b.shape, const2),
            pl.BlockSpec(wq.shape, const3),
            pl.BlockSpec(wk.shape, const3),
            pl.BlockSpec(wv.shape, const3),
            pl.BlockSpec(w_if_t.shape, const2),
            pl.BlockSpec(b_if.shape, const2),
            pl.BlockSpec(norm_w.shape, const2),
            pl.BlockSpec(skip_w.shape, const2),
        ],
        out_specs=pl.BlockSpec((nb, span, w), lambda bi, c: (bi, c, 0)),
        out_shape=jax.ShapeDtypeStruct((b, s, w), BF16),
        scratch_shapes=[
            pltpu.VMEM((nb, SUBLANES + span, w), F32),
            pltpu.VMEM((nb, MLSTM_HEADS, MLSTM_HEAD_DIM, MLSTM_HEAD_DIM), F32),
            pltpu.VMEM((nb, MLSTM_HEADS, SUBLANES, MLSTM_HEAD_DIM), F32),
            pltpu.VMEM((nb, SUBLANES, LANES), F32),
        ],
        compiler_params=_params("parallel", "arbitrary"),
        name="mlstm",
    )(mxz, mxz, mxz, conv_w, conv_b, wq, wk, wv, w_if_t, b_if, norm_w, skip_w)


def _merge_kernel(x_ref, nw_ref, wg_ref, gb_ref, od_ref, of_ref, om_ref, wbd_ref, wbf_ref, wbm_ref,
                  wo_ref, o_ref):
    x = x_ref[...]
    d = x.shape[1]
    xn = _rms_norm(x, nw_ref[...]).astype(BF16)
    merged = None
    for br, (b_ref, w_ref) in enumerate(((od_ref, wbd_ref), (of_ref, wbf_ref), (om_ref, wbm_ref))):
        cols = slice(br * d, (br + 1) * d)
        gate = jax.nn.sigmoid(jnp.dot(xn, wg_ref[:, cols], preferred_element_type=F32) + gb_ref[:, cols])
        term = gate * jnp.dot(b_ref[...], w_ref[...], preferred_element_type=F32)
        merged = term if merged is None else merged + term
    o_ref[...] = x + jnp.dot(merged.astype(BF16), wo_ref[...], preferred_element_type=F32)


def _merge(x, norm_w, w_g, gate_b, o_diff, o_fox, o_mlstm, wbd, wbf, wbm, w_out, *, tm=1024):
    m, d = x.shape
    assert m % tm == 0, m
    const = lambda i: (0, 0)
    row = lambda i: (i, 0)
    return pl.pallas_call(
        _merge_kernel,
        grid=(m // tm,),
        in_specs=[
            pl.BlockSpec((tm, d), row),
            pl.BlockSpec((1, d), const),
            pl.BlockSpec(w_g.shape, const),
            pl.BlockSpec(gate_b.shape, const),
            pl.BlockSpec((tm, ATTN_WIDTH), row),
            pl.BlockSpec((tm, ATTN_WIDTH), row),
            pl.BlockSpec((tm, MLSTM_WIDTH), row),
            pl.BlockSpec(wbd.shape, const),
            pl.BlockSpec(wbf.shape, const),
            pl.BlockSpec(wbm.shape, const),
            pl.BlockSpec(w_out.shape, const),
        ],
        out_specs=pl.BlockSpec((tm, d), row),
        out_shape=jax.ShapeDtypeStruct((m, d), F32),
        compiler_params=_params("parallel"),
        name="merge",
    )(x, norm_w, w_g, gate_b, o_diff, o_fox, o_mlstm, wbd, wbf, wbm, w_out)


def _alibi_bias(seq):
    slopes = np.array([2.0 ** (-8.0 * (h + 1) / DIFF_HEADS) for h in range(DIFF_HEADS)], dtype=np.float32)
    pos = np.arange(seq, dtype=np.float32)
    per_head = slopes[:, None] * pos[None, :]
    return jnp.asarray(np.repeat(per_head[:, None, :], 2, axis=1))


def kernel(x, ffn1_norm, ffn1_w_gate, ffn1_w_up, ffn1_w_down, mix_norm, w_in, gate_bias, diff_lq1, diff_lk1, diff_lq2, diff_lk2, diff_subln, fox_b_f, mlstm_conv_w, mlstm_conv_b, mlstm_wq, mlstm_wk, mlstm_wv, mlstm_w_if, mlstm_b_if, mlstm_norm, mlstm_skip, w_branch_diff, w_branch_fox, w_branch_mlstm, w_out, ffn2_norm, ffn2_w_gate, ffn2_w_up, ffn2_w_down, final_norm):
    batch, seq, d = x.shape
    depth = w_in.shape[0]
    m = batch * seq
    bf = lambda a: a.astype(BF16)
    row = lambda a: a.reshape(1, -1)
    aw = ATTN_WIDTH
    ff0 = 6 * aw
    mx0 = ff0 + FOX_HEADS
    g0 = mx0 + 2 * MLSTM_WIDTH
    blocks = aw // LANES
    alibi = _alibi_bias(seq)
    w_in_b = bf(w_in)

    x = x.reshape(m, d)
    for l in range(depth):
        x = _ffn(x, row(ffn1_norm[l]), ffn1_w_gate, ffn1_w_up, ffn1_w_down, l)

        w_in_l = w_in_b[l]
        w_vt = jnp.concatenate([w_in_l[:, 2 * aw:3 * aw], w_in_l[:, 5 * aw:6 * aw]], axis=1).T
        qk, vt, mxz, cf = _proj(x, row(mix_norm[l]), w_in_b, l, w_vt, w_in_l[:, mx0:g0],
                                w_in_l[:, ff0:mx0].T, fox_b_f[l].reshape(-1, 1), seq=seq)
        qk = qk.reshape(batch, seq, 4 * aw)
        lam_init = 0.8 - 0.6 * math.exp(-0.3 * l)
        o_diff = _attention(
            functools.partial(_diff_attn_kernel, lam_init=lam_init), qk, vt, alibi,
            [row(diff_lq1[l]), row(diff_lk1[l]), row(diff_lq2[l]), row(diff_lk2[l]), row(diff_subln[l])],
            q_col=0, v_row=0, v_rows=LANES, bias_per_batch=False, name="diff_attn")
        o_fox = _attention(_fox_attn_kernel, qk, vt, cf.reshape(FOX_HEADS // 2, 2, m), [],
                           q_col=2 * blocks, v_row=blocks, v_rows=LANES // 2, bias_per_batch=True,
                           name="fox_attn")
        o_mlstm = _mlstm(mxz.reshape(batch, seq, 2 * MLSTM_WIDTH), mlstm_conv_w[l], row(mlstm_conv_b[l]),
                         bf(jnp.swapaxes(mlstm_wq[l], 1, 2)), bf(mlstm_wk[l]),
                         bf(jnp.swapaxes(mlstm_wv[l], 1, 2)), bf(mlstm_w_if[l].T),
                         mlstm_b_if[l].reshape(-1, 1), row(mlstm_norm[l]), row(mlstm_skip[l]))
        x = _merge(x, row(mix_norm[l]), w_in_l[:, g0:], gate_bias[l].reshape(1, -1),
                   o_diff.reshape(m, aw), o_fox.reshape(m, aw),
                   o_mlstm.reshape(m, MLSTM_WIDTH), bf(w_branch_diff[l]), bf(w_branch_fox[l]),
                   bf(w_branch_mlstm[l]), bf(w_out[l]))

        last = l == depth - 1
        x = _ffn(x, row(ffn2_norm[l]), ffn2_w_gate, ffn2_w_up, ffn2_w_down, l,
                 row(final_norm) if last else None)
    return x.reshape(batch, seq, d)
```

```python
import functools
import math

import numpy as np
import jax
import jax.numpy as jnp
from jax import lax
from jax.experimental import pallas as pl
from jax.experimental.pallas import tpu as pltpu

RMS_EPS = 1e-6
LN_EPS = 1e-5
DIFF_HEADS = 4
ATTN_HEAD_DIM = 64
FOX_HEADS = 8
MLSTM_HEADS = 4
MLSTM_HEAD_DIM = 128
MLSTM_WIDTH = MLSTM_HEADS * MLSTM_HEAD_DIM
MLSTM_CONV = 4
MLSTM_CHUNK = 128
MLSTM_STAGE_AHEAD = 2
ATTN_WIDTH = 512
LANES = 128
SUBLANES = 8
BF16_ROWS = 16
SCORE_SLOTS = 4
BIAS_TERMS = 3
VMEM_LIMIT_BYTES = 56 * 1024 * 1024
LOG2E = math.log2(math.e)
Q_SCALE = ATTN_HEAD_DIM ** -0.5 * LOG2E

F32 = jnp.float32
BF16 = jnp.bfloat16
NT_DIMS = (((1,), (1,)), ((), ()))


def _params(*semantics):
    return pltpu.CompilerParams(dimension_semantics=semantics, vmem_limit_bytes=VMEM_LIMIT_BYTES)


def _rms_norm(x, w):
    return x * lax.rsqrt(jnp.mean(x * x, axis=-1, keepdims=True) + RMS_EPS) * w


def _silu(x):
    half = 0.5 * x
    return half + half * jnp.tanh(half)


def _log_sigmoid(x):
    return jnp.minimum(x, 0.0) - jnp.log1p(jnp.exp(-jnp.abs(x)))


def _lane_cumsum(x):
    n = x.shape[-1]
    lane = lax.broadcasted_iota(jnp.int32, x.shape, 1)
    shift = 1
    while shift < n:
        x = x + jnp.where(lane >= shift, pltpu.roll(x, shift, axis=1), 0.0)
        shift *= 2
    return x


def _ffn_kernel(*refs, final):
    if final:
        x_ref, nw_ref, wg_ref, wu_ref, wd_ref, fn_ref, o_ref, xn_ref = refs
    else:
        x_ref, nw_ref, wg_ref, wu_ref, wd_ref, o_ref, xn_ref = refs
    f = pl.program_id(1)

    @pl.when(f == 0)
    def _():
        xn_ref[...] = _rms_norm(x_ref[...], nw_ref[...]).astype(BF16)
        o_ref[...] = jnp.zeros_like(o_ref)

    xn = xn_ref[...]
    g = jnp.dot(xn, wg_ref[...].astype(BF16), preferred_element_type=F32)
    u = jnp.dot(xn, wu_ref[...].astype(BF16), preferred_element_type=F32)
    h = (_silu(g) * u).astype(BF16)
    o_ref[...] += jnp.dot(h, wd_ref[...].astype(BF16), preferred_element_type=F32)

    @pl.when(f == pl.num_programs(1) - 1)
    def _():
        y = x_ref[...] + 0.5 * o_ref[...]
        if final:
            y = _rms_norm(y, fn_ref[...])
        o_ref[...] = y


def _ffn(x, norm_w, w_gate, w_up, w_down, layer, final_norm=None, *, tm=2048, tf=256):
    m, d = x.shape
    dff = w_gate.shape[2]
    assert m % tm == 0 and dff % tf == 0, (m, dff)
    final = final_norm is not None
    in_specs = [
        pl.BlockSpec((tm, d), lambda i, f: (i, 0)),
        pl.BlockSpec((1, d), lambda i, f: (0, 0)),
        pl.BlockSpec((None, d, tf), lambda i, f: (layer, 0, f)),
        pl.BlockSpec((None, d, tf), lambda i, f: (layer, 0, f)),
        pl.BlockSpec((None, tf, d), lambda i, f: (layer, f, 0)),
    ]
    args = [x, norm_w, w_gate, w_up, w_down]
    if final:
        in_specs.append(pl.BlockSpec((1, d), lambda i, f: (0, 0)))
        args.append(final_norm)
    return pl.pallas_call(
        functools.partial(_ffn_kernel, final=final),
        grid=(m // tm, dff // tf),
        in_specs=in_specs,
        out_specs=pl.BlockSpec((tm, d), lambda i, f: (i, 0)),
        out_shape=jax.ShapeDtypeStruct((m, d), F32),
        scratch_shapes=[pltpu.VMEM((tm, d), BF16)],
        compiler_params=_params("parallel", "arbitrary"),
        name="ffn_final" if final else "ffn",
    )(*args)


def _proj_kernel(x_ref, nw_ref, wdq_ref, wdk_ref, wfq_ref, wfk_ref, wvt_ref, wm_ref, wff_ref, bf_ref,
                 qk_ref, vt_ref, mxz_ref, cf_ref, carry_ref, *, tiles_per_seq):
    i = pl.program_id(0)

    @pl.when(i == 0)
    def _():
        carry_ref[...] = jnp.zeros_like(carry_ref)

    xn = _rms_norm(x_ref[...], nw_ref[...]).astype(BF16)

    ff = lax.dot_general(wff_ref[...], xn, NT_DIMS, preferred_element_type=F32)
    csum = _lane_cumsum(_log_sigmoid(ff + bf_ref[...]))
    csum = csum + jnp.where(i % tiles_per_seq == 0, 0.0, carry_ref[:, 0:1])
    carry_ref[...] = jnp.broadcast_to(csum[:, -1:], carry_ref.shape)
    cf_ref[...] = -csum

    for grp, w_ref in enumerate((wdq_ref, wdk_ref, wfq_ref, wfk_ref)):
        cols = slice(grp * ATTN_WIDTH, (grp + 1) * ATTN_WIDTH)
        r = jnp.dot(xn, w_ref[...], preferred_element_type=F32)
        if grp % 2 == 0:
            r = r * Q_SCALE
        qk_ref[:, cols] = r.astype(BF16)
    for grp in range(2):
        rows = slice(grp * ATTN_WIDTH, (grp + 1) * ATTN_WIDTH)
        vt_ref[rows, :] = lax.dot_general(wvt_ref[rows, :], xn, NT_DIMS,
                                          preferred_element_type=F32).astype(BF16)
    for grp in range(2):
        cols = slice(grp * MLSTM_WIDTH, (grp + 1) * MLSTM_WIDTH)
        mxz_ref[:, cols] = jnp.dot(xn, wm_ref[:, cols], preferred_element_type=F32)


def _proj(x, norm_w, w_in_b, layer, w_vt, w_m, w_ff_t, b_f, *, seq, tm=1024):
    m, d = x.shape
    nh = w_ff_t.shape[0]
    assert seq % tm == 0 and m % seq == 0, (m, seq)
    const = lambda i: (0, 0)
    qk_groups = (0, 1, 3, 4)
    qk_specs = [pl.BlockSpec((None, d, ATTN_WIDTH), functools.partial(lambda i, g: (layer, 0, g), g=g))
                for g in qk_groups]
    return pl.pallas_call(
        functools.partial(_proj_kernel, tiles_per_seq=seq // tm),
        grid=(m // tm,),
        in_specs=[
            pl.BlockSpec((tm, d), lambda i: (i, 0)),
            pl.BlockSpec((1, d), const),
            *qk_specs,
            pl.BlockSpec(w_vt.shape, const),
            pl.BlockSpec(w_m.shape, const),
            pl.BlockSpec(w_ff_t.shape, const),
            pl.BlockSpec((nh, 1), const),
        ],
        out_specs=[
            pl.BlockSpec((tm, len(qk_groups) * ATTN_WIDTH), lambda i: (i, 0)),
            pl.BlockSpec((w_vt.shape[0], tm), lambda i: (0, i)),
            pl.BlockSpec((tm, w_m.shape[1]), lambda i: (i, 0)),
            pl.BlockSpec((nh, tm), lambda i: (0, i)),
        ],
        out_shape=[
            jax.ShapeDtypeStruct((m, len(qk_groups) * ATTN_WIDTH), BF16),
            jax.ShapeDtypeStruct((w_vt.shape[0], m), BF16),
            jax.ShapeDtypeStruct((m, w_m.shape[1]), F32),
            jax.ShapeDtypeStruct((nh, m), F32),
        ],
        scratch_shapes=[pltpu.VMEM((nh, LANES), F32)],
        compiler_params=_params("arbitrary"),
        name="proj",
    )(x, norm_w, w_in_b, w_in_b, w_in_b, w_in_b, w_vt, w_m, w_ff_t, b_f)


def _flash_pair(q_ref, k_ref, vt_ref, bias_ref, kaug_ref, vaug_ref, qm_ref, s_ref, acc_ref, finish, *, tq, tk, v_rows,
                between=None):
    seq = k_ref.shape[1]
    dr = v_rows[0].stop - v_rows[0].start
    half = LANES // 2

    kaug_ref[:, 0:LANES] = k_ref[0]
    terms = []
    for c in range(2):
        rem = bias_ref[0, c:c + 1, :] * LOG2E
        for _ in range(BIAS_TERMS):
            part = rem.astype(BF16).astype(F32)
            terms.append(part)
            rem = rem - part
    term_rows = jnp.concatenate(terms + [jnp.zeros((LANES - 2 * BIAS_TERMS, seq), F32)], axis=0)
    for blk in range(seq // LANES):
        sl = slice(blk * LANES, (blk + 1) * LANES)
        kaug_ref[sl, LANES:] = term_rows[:, sl].T.astype(BF16)
    for c in range(2):
        vaug_ref[c, 0:dr, :] = vt_ref[v_rows[c], :]
        vaug_ref[c, dr:, :] = jnp.ones((BF16_ROWS, seq), BF16)

    qlane = lax.broadcasted_iota(jnp.int32, (seq, LANES), 1)
    q = q_ref[0]
    for c in range(2):
        qm_ref[c, :, 0:LANES] = jnp.where((qlane >= c * half) & (qlane < (c + 1) * half), q, jnp.zeros_like(q))
        picks_bias = (qlane >= BIAS_TERMS * c) & (qlane < BIAS_TERMS * (c + 1))
        qm_ref[c, :, LANES:] = jnp.where(picks_bias, 1.0, 0.0).astype(BF16)

    work = []
    for i in range(seq // tq):
        n_full = (i * tq) // tk
        lead = i * tq - n_full * tk
        work += [(i, j * tk, tk, None, j == 0, False) for j in range(n_full)]
        work.append((i, n_full * tk, lead + tq, lead, n_full == 0, True))

    def scores(n):
        i, k0, size, _, _, _ = work[n]
        for c in range(2):
            s_ref[n % s_ref.shape[0], c, 0:size, :] = lax.dot_general(
                kaug_ref[k0:k0 + size, :], qm_ref[c, i * tq:(i + 1) * tq, :], NT_DIMS,
                preferred_element_type=F32)

    scores(0)
    scores(1)
    m = [None, None]
    for n, (i, k0, size, lead, first, last) in enumerate(work):
        if n + 2 < len(work):
            scores(n + 2)
        for c in range(2):
            s = s_ref[n % s_ref.shape[0], c, 0:size, :]
            if lead is not None:
                key = lax.broadcasted_iota(jnp.int32, (tq, tq), 0)
                qry = lax.broadcasted_iota(jnp.int32, (tq, tq), 1)
                diag = jnp.where(key <= qry, s[lead:, :], -jnp.inf)
                s = diag if lead == 0 else jnp.concatenate([s[0:lead, :], diag], axis=0)
            m_blk = jnp.max(s, axis=0, keepdims=True)
            m_new = m_blk if first else jnp.maximum(m[c], m_blk)
            p = jnp.exp2(s - m_new).astype(BF16)
            pv = jnp.dot(vaug_ref[c, :, k0:k0 + size], p, preferred_element_type=F32)
            if first:
                acc_ref[i, c] = pv
            else:
                acc_ref[i, c] = jnp.exp2(m[c] - m_new) * acc_ref[i, c] + pv
            m[c] = m_new
        if last:
            finish(i, *[acc_ref[i, c, 0:dr, :] * (1.0 / acc_ref[i, c, dr:dr + 1, :]) for c in range(2)])
        if between is not None:
            between(n)


def _diff_attn_kernel(q_ref, k_ref, vt_ref, bias_ref, lq1_ref, lk1_ref, lq2_ref, lk2_ref, sub_ref,
                      o_ref, *scratch, tq, tk, lam_init):
    lam = (jnp.exp(jnp.sum(lq1_ref[...] * lk1_ref[...], axis=1, keepdims=True))
           - jnp.exp(jnp.sum(lq2_ref[...] * lk2_ref[...], axis=1, keepdims=True)) + lam_init)

    def finish(i, a1, a2):
        d = (a1 - lam * a2).T
        o_ref[0, i * tq:(i + 1) * tq, :] = (_rms_norm(d, sub_ref[...]) * (1.0 - lam_init)).astype(o_ref.dtype)

    full = slice(0, LANES)
    _flash_pair(q_ref, k_ref, vt_ref, bias_ref, *scratch, finish, tq=tq, tk=tk, v_rows=(full, full))


def _fox_attn_kernel(q_ref, k_ref, vt_ref, bias_ref, o_ref, *scratch, tq, tk):
    def finish(i, a_even, a_odd):
        o_ref[0, i * tq:(i + 1) * tq, :] = jnp.concatenate([a_even, a_odd], axis=0).T.astype(o_ref.dtype)

    half = LANES // 2
    _flash_pair(q_ref, k_ref, vt_ref, bias_ref, *scratch, finish, tq=tq, tk=tk,
                v_rows=(slice(0, half), slice(half, LANES)))


def _attention(kernel, qk, vt, bias, extra, *, q_col, v_row, v_rows, bias_per_batch, tq=256, tk=1024, name):
    b, s, _ = qk.shape
    assert s % tk == 0 and tk % tq == 0 and tq % LANES == 0, (s, tq, tk)
    groups = ATTN_WIDTH // LANES
    if bias_per_batch:
        bias_map = lambda bi, g: (g, 0, bi)
    else:
        bias_map = lambda bi, g: (g, 0, 0)
    in_specs = [
        pl.BlockSpec((1, s, LANES), lambda bi, g: (bi, 0, q_col + g)),
        pl.BlockSpec((1, s, LANES), lambda bi, g: (bi, 0, q_col + groups + g)),
        pl.BlockSpec((LANES, s), lambda bi, g: (v_row + g, bi)),
        pl.BlockSpec((1, 2, s), bias_map),
    ] + [pl.BlockSpec(e.shape, lambda bi, g: (0, 0)) for e in extra]
    rows = v_rows + BF16_ROWS
    return pl.pallas_call(
        functools.partial(kernel, tq=tq, tk=tk),
        grid=(b, groups),
        in_specs=in_specs,
        out_specs=pl.BlockSpec((1, s, LANES), lambda bi, g: (bi, 0, g)),
        out_shape=jax.ShapeDtypeStruct((b, s, ATTN_WIDTH), BF16),
        scratch_shapes=[pltpu.VMEM((s, 2 * LANES), BF16), pltpu.VMEM((2, rows, s), BF16),
                        pltpu.VMEM((2, s, 2 * LANES), BF16), pltpu.VMEM((SCORE_SLOTS, 2, tk, tq), F32),
                        pltpu.VMEM((s // tq, 2, rows, tq), F32)],
        compiler_params=_params("parallel", "parallel"),
        name=name,
    )(qk, qk, vt, bias, *extra)


def _mlstm_kernel(cur_ref, prev_ref, z_ref, cw_ref, cb_ref, wqt_ref, wk_ref, wvt_ref, wif_ref, bif_ref,
                  nrm_ref, skip_ref, o_ref, xbuf_ref, ct_ref, n_ref, m_ref):
    ci = pl.program_id(1)
    nb, span, _ = cur_ref.shape
    chunks = span // MLSTM_CHUNK

    @pl.when(ci == 0)
    def _():
        ct_ref[...] = jnp.zeros_like(ct_ref)
        n_ref[...] = jnp.zeros_like(n_ref)
        m_ref[...] = jnp.zeros_like(m_ref)

    for r in range(nb):
        xbuf_ref[r, 0:SUBLANES, :] = jnp.where(ci > 0, prev_ref[r], 0.0)
        xbuf_ref[r, SUBLANES:, :] = cur_ref[r]

    work = [(r, k) for k in range(chunks) for r in range(nb)]
    refs = (cw_ref, cb_ref, wqt_ref, wk_ref, wvt_ref, wif_ref, bif_ref)
    ahead = min(MLSTM_STAGE_AHEAD, len(work))
    staged = {n: _mlstm_gates(xbuf_ref.at[work[n][0]], work[n][1], *refs) for n in range(ahead)}
    for n, (r, k) in enumerate(work):
        if n + ahead < len(work):
            staged[n + ahead] = _mlstm_gates(xbuf_ref.at[work[n + ahead][0]], work[n + ahead][1], *refs)
        rows = slice(k * MLSTM_CHUNK, (k + 1) * MLSTM_CHUNK)
        _mlstm_recur(staged.pop(n), z_ref.at[r, rows], nrm_ref, skip_ref, o_ref.at[r, rows],
                     ct_ref.at[r], n_ref.at[r], m_ref.at[r])


def _mlstm_gates(xbuf_ref, k, cw_ref, cb_ref, wqt_ref, wk_ref, wvt_ref, wif_ref, bif_ref):
    L = MLSTM_CHUNK
    dh = MLSTM_HEAD_DIM
    nh = MLSTM_HEADS

    base = SUBLANES + k * L
    cur = xbuf_ref[base:base + L, :]
    conv = cb_ref[...]
    for j in range(MLSTM_CONV):
        off = base - (MLSTM_CONV - 1) + j
        conv = conv + cw_ref[j:j + 1, :] * xbuf_ref[off:off + L, :]
    xc = _silu(conv)

    qts, kss, vts = [], [], []
    ift = bif_ref[...]
    for h in range(nh):
        sl = slice(h * dh, (h + 1) * dh)
        xh = xc[:, sl].astype(BF16)
        vh = cur[:, sl].astype(BF16)
        qt = lax.dot_general(wqt_ref[h], xh, NT_DIMS, preferred_element_type=F32).astype(BF16)
        kk = jnp.dot(xh, wk_ref[h], preferred_element_type=F32)
        vt = lax.dot_general(wvt_ref[h], vh, NT_DIMS, preferred_element_type=F32).astype(BF16)
        ift = ift + jnp.dot(wif_ref[:, sl], qt, preferred_element_type=F32)
        ift = ift + lax.dot_general(wif_ref[:, MLSTM_WIDTH + h * dh:MLSTM_WIDTH + (h + 1) * dh],
                                    kk.astype(BF16), NT_DIMS, preferred_element_type=F32)
        ift = ift + jnp.dot(wif_ref[:, 2 * MLSTM_WIDTH + h * dh:2 * MLSTM_WIDTH + (h + 1) * dh], vt,
                            preferred_element_type=F32)
        qts.append(qt)
        kss.append((kk * (dh ** -0.5)).astype(BF16))
        vts.append(vt)

    key = lax.broadcasted_iota(jnp.int32, (L, L), 0)
    qry = lax.broadcasted_iota(jnp.int32, (L, L), 1)
    lf = _log_sigmoid(ift)
    tri = jnp.where(key <= qry, 1.0, 0.0).astype(BF16)
    b_all = jnp.zeros_like(lf)
    rem = lf
    for _ in range(3):
        part = rem.astype(BF16)
        b_all = b_all + jnp.dot(part, tri, preferred_element_type=F32)
        rem = rem - part.astype(F32)
    return dict(xc=xc, qts=qts, kss=kss, vts=vts, li_rows=ift[0:nh], b_rows=b_all[nh:2 * nh])


def _mlstm_recur(st, z_ref, nrm_ref, skip_ref, o_ref, ct_ref, n_ref, m_ref):
    L = MLSTM_CHUNK
    dh = MLSTM_HEAD_DIM
    key = lax.broadcasted_iota(jnp.int32, (L, L), 0)
    qry = lax.broadcasted_iota(jnp.int32, (L, L), 1)
    causal = key <= qry
    xc = st["xc"]

    heads = range(MLSTM_HEADS)
    qts, kss, vts = st["qts"], st["kss"], st["vts"]
    ct_state = [ct_ref[h] for h in heads]
    n_state = [n_ref[h] for h in heads]
    m_prev = [m_ref[h:h + 1, 0:1] for h in heads]
    qk = [jnp.dot(kss[h], qts[h], preferred_element_type=F32) for h in heads]
    cq = [jnp.dot(ct_state[h].astype(BF16), qts[h], preferred_element_type=F32) for h in heads]
    nq = [jnp.dot(n_state[h].astype(BF16), qts[h], preferred_element_type=F32)[0:1] for h in heads]

    li = [st["li_rows"][h:h + 1] for h in heads]
    b = [st["b_rows"][h:h + 1] for h in heads]
    a = [li[h] - b[h] for h in heads]
    a_keys = [jnp.broadcast_to(a[h], (L, L)).T for h in heads]
    d = [jnp.where(causal, b[h] + a_keys[h], -jnp.inf) for h in heads]
    inter = [b[h] + m_prev[h] for h in heads]
    m_t = [jnp.maximum(inter[h], jnp.max(d[h], axis=0, keepdims=True)) for h in heads]
    w_inter = [jnp.exp(inter[h] - m_t[h]) for h in heads]
    s = [qk[h] * jnp.exp(d[h] - m_t[h]) for h in heads]
    sv = [jnp.dot(vts[h], s[h].astype(BF16), preferred_element_type=F32) for h in heads]

    b_last = [b[h][:, L - 1:L] for h in heads]
    g = [b_last[h] + a[h] for h in heads]
    m_new = [jnp.maximum(b_last[h] + m_prev[h], jnp.max(g[h], axis=1, keepdims=True)) for h in heads]
    wk = [jnp.exp(g[h] - m_new[h]) for h in heads]
    dec = [jnp.exp(b_last[h] + m_prev[h] - m_new[h]) for h in heads]
    for h in heads:
        ct_ref[h] = dec[h] * ct_state[h] + jnp.dot((vts[h].astype(F32) * wk[h]).astype(BF16), kss[h],
                                                   preferred_element_type=F32)
        n_ref[h] = dec[h] * n_state[h] + jnp.dot(jnp.broadcast_to(wk[h], (SUBLANES, L)).astype(BF16),
                                                 kss[h], preferred_element_type=F32)
        m_ref[h:h + 1, :] = jnp.broadcast_to(m_new[h], (1, LANES))

    for h in heads:
        sl = slice(h * dh, (h + 1) * dh)
        num = w_inter[h] * cq[h] + sv[h]
        den = w_inter[h] * nq[h] + jnp.sum(s[h], axis=0, keepdims=True)
        hh = num * (1.0 / jnp.maximum(jnp.abs(den), jnp.exp(-m_t[h])))
        mu = jnp.mean(hh, axis=0, keepdims=True)
        cen = hh - mu
        var = jnp.mean(cen * cen, axis=0, keepdims=True)
        hn = (cen * lax.rsqrt(var + LN_EPS)).T * nrm_ref[:, sl]
        o_ref[:, sl] = ((hn + skip_ref[:, sl] * xc[:, sl]) * _silu(z_ref[:, sl])).astype(o_ref.dtype)


def _mlstm(mxz, conv_w, conv_b, wq, wk, wv, w_if_t, b_if, norm_w, skip_w, *, nb=2, chunks=4):
    b, s, _ = mxz.shape
    span = chunks * MLSTM_CHUNK
    assert b % nb == 0 and s % span == 0, (b, s)
    w = MLSTM_WIDTH
    rows_per_span = span // SUBLANES
    const2 = lambda bi, c: (0, 0)
    const3 = lambda bi, c: (0, 0, 0)
    return pl.pallas_call(
        _mlstm_kernel,
        grid=(b // nb, s // span),
        in_specs=[
            pl.BlockSpec((nb, span, w), lambda bi, c: (bi, c, 0)),
            pl.BlockSpec((nb, SUBLANES, w), lambda bi, c: (bi, jnp.maximum(c * rows_per_span - 1, 0), 0)),
            pl.BlockSpec((nb, span, w), lambda bi, c: (bi, c, 1)),
            pl.BlockSpec(conv_w.shape, const2),
            pl.BlockSpec(conv_b.shape, const2),
            pl.BlockSpec(wq.shape, const3),
            pl.BlockSpec(wk.shape, const3),
            pl.BlockSpec(wv.shape, const3),
            pl.BlockSpec(w_if_t.shape, const2),
            pl.BlockSpec(b_if.shape, const2),
            pl.BlockSpec(norm_w.shape, const2),
            pl.BlockSpec(skip_w.shape, const2),
        ],
        out_specs=pl.BlockSpec((nb, span, w), lambda bi, c: (bi, c, 0)),
        out_shape=jax.ShapeDtypeStruct((b, s, w), BF16),
        scratch_shapes=[
            pltpu.VMEM((nb, SUBLANES + span, w), F32),
            pltpu.VMEM((nb, MLSTM_HEADS, MLSTM_HEAD_DIM, MLSTM_HEAD_DIM), F32),
            pltpu.VMEM((nb, MLSTM_HEADS, SUBLANES, MLSTM_HEAD_DIM), F32),
            pltpu.VMEM((nb, SUBLANES, LANES), F32),
        ],
        compiler_params=_params("parallel", "arbitrary"),
        name="mlstm",
    )(mxz, mxz, mxz, conv_w, conv_b, wq, wk, wv, w_if_t, b_if, norm_w, skip_w)


def _fox_mlstm_kernel(q_ref, k_ref, vt_ref, bias_ref, cur_ref, prev_ref, z_ref, cw_ref, cb_ref, wqt_ref, wk_ref,
                      wvt_ref, wif_ref, bif_ref, nrm_ref, skip_ref, of_ref, om_ref,
                      kaug_ref, vaug_ref, qm_ref, s_ref, acc_ref, xbuf_ref, ct_ref, n_ref, m_ref, *, tq, tk):
    g = pl.program_id(1)
    chunks = cur_ref.shape[1] // MLSTM_CHUNK

    @pl.when(g == 0)
    def _():
        ct_ref[...] = jnp.zeros_like(ct_ref)
        n_ref[...] = jnp.zeros_like(n_ref)
        m_ref[...] = jnp.zeros_like(m_ref)

    xbuf_ref[0:SUBLANES, :] = jnp.where(g > 0, prev_ref[0], 0.0)
    xbuf_ref[SUBLANES:, :] = cur_ref[0]

    refs = (cw_ref, cb_ref, wqt_ref, wk_ref, wvt_ref, wif_ref, bif_ref)
    staged = {}

    def gates(k):
        staged[k] = _mlstm_gates(xbuf_ref, k, *refs)

    def recur(k):
        rows = slice(k * MLSTM_CHUNK, (k + 1) * MLSTM_CHUNK)
        _mlstm_recur(staged.pop(k), z_ref.at[0, rows], nrm_ref, skip_ref, om_ref.at[0, rows], ct_ref, n_ref, m_ref)

    def stage(k):
        if k + 1 < chunks:
            gates(k + 1)
        recur(k)

    pending = [functools.partial(gates, 0)] + [functools.partial(stage, k) for k in range(chunks)]

    def between(n):
        if pending:
            pending.pop(0)()

    def finish(i, a_even, a_odd):
        of_ref[0, i * tq:(i + 1) * tq, :] = jnp.concatenate([a_even, a_odd], axis=0).T.astype(of_ref.dtype)

    half = LANES // 2
    _flash_pair(q_ref, k_ref, vt_ref, bias_ref, kaug_ref, vaug_ref, qm_ref, s_ref, acc_ref, finish, tq=tq, tk=tk,
                v_rows=(slice(0, half), slice(half, LANES)), between=between)
    while pending:
        pending.pop(0)()


def _fox_mlstm(qk, vt, bias, mxz, conv_w, conv_b, wq, wk, wv, w_if_t, b_if, norm_w, skip_w, *, q_col, v_row,
               tq=256, tk=1024):
    b, s, _ = qk.shape
    groups = ATTN_WIDTH // LANES
    span = s // groups
    assert s % tk == 0 and tk % tq == 0 and span % MLSTM_CHUNK == 0, (s, tq, tk)
    w = MLSTM_WIDTH
    rows_per_span = span // SUBLANES
    const2 = lambda bi, g: (0, 0)
    const3 = lambda bi, g: (0, 0, 0)
    rows = LANES // 2 + BF16_ROWS
    return pl.pallas_call(
        functools.partial(_fox_mlstm_kernel, tq=tq, tk=tk),
        grid=(b, groups),
        in_specs=[
            pl.BlockSpec((1, s, LANES), lambda bi, g: (bi, 0, q_col + g)),
            pl.BlockSpec((1, s, LANES), lambda bi, g: (bi, 0, q_col + groups + g)),
            pl.BlockSpec((LANES, s), lambda bi, g: (v_row + g, bi)),
            pl.BlockSpec((1, 2, s), lambda bi, g: (g, 0, bi)),
            pl.BlockSpec((1, span, w), lambda bi, g: (bi, g, 0)),
            pl.BlockSpec((1, SUBLANES, w), lambda bi, g: (bi, jnp.maximum(g * rows_per_span - 1, 0), 0)),
            pl.BlockSpec((1, span, w), lambda bi, g: (bi, g, 1)),
            pl.BlockSpec(conv_w.shape, const2),
            pl.BlockSpec(conv_b.shape, const2),
            pl.BlockSpec(wq.shape, const3),
            pl.BlockSpec(wk.shape, const3),
            pl.BlockSpec(wv.shape, const3),
            pl.BlockSpec(w_if_t.shape, const2),
            pl.BlockSpec(b_if.shape, const2),
            pl.BlockSpec(norm_w.shape, const2),
            pl.BlockSpec(skip_w.shape, const2),
        ],
        out_specs=[
            pl.BlockSpec((1, s, LANES), lambda bi, g: (bi, 0, g)),
            pl.BlockSpec((1, span, w), lambda bi, g: (bi, g, 0)),
        ],
        out_shape=[
            jax.ShapeDtypeStruct((b, s, ATTN_WIDTH), BF16),
            jax.ShapeDtypeStruct((b, s, w), BF16),
        ],
        scratch_shapes=[pltpu.VMEM((s, 2 * LANES), BF16), pltpu.VMEM((2, rows, s), BF16),
                        pltpu.VMEM((2, s, 2 * LANES), BF16), pltpu.VMEM((SCORE_SLOTS, 2, tk, tq), F32),
                        pltpu.VMEM((s // tq, 2, rows, tq), F32),
                        pltpu.VMEM((SUBLANES + span, w), F32),
                        pltpu.VMEM((MLSTM_HEADS, MLSTM_HEAD_DIM, MLSTM_HEAD_DIM), F32),
                        pltpu.VMEM((MLSTM_HEADS, SUBLANES, MLSTM_HEAD_DIM), F32),
                        pltpu.VMEM((SUBLANES, LANES), F32)],
        compiler_params=_params("parallel", "arbitrary"),
        name="fox_mlstm",
    )(qk, qk, vt, bias, mxz, mxz, mxz, conv_w, conv_b, wq, wk, wv, w_if_t, b_if, norm_w, skip_w)


def _merge_kernel(x_ref, nw_ref, wg_ref, gb_ref, od_ref, of_ref, om_ref, wbd_ref, wbf_ref, wbm_ref,
                  wo_ref, o_ref):
    x = x_ref[...]
    d = x.shape[1]
    xn = _rms_norm(x, nw_ref[...]).astype(BF16)
    merged = None
    for br, (b_ref, w_ref) in enumerate(((od_ref, wbd_ref), (of_ref, wbf_ref), (om_ref, wbm_ref))):
        cols = slice(br * d, (br + 1) * d)
        gate = jax.nn.sigmoid(jnp.dot(xn, wg_ref[:, cols], preferred_element_type=F32) + gb_ref[:, cols])
        term = gate * jnp.dot(b_ref[...], w_ref[...], preferred_element_type=F32)
        merged = term if merged is None else merged + term
    o_ref[...] = x + jnp.dot(merged.astype(BF16), wo_ref[...], preferred_element_type=F32)


def _merge(x, norm_w, w_g, gate_b, o_diff, o_fox, o_mlstm, wbd, wbf, wbm, w_out, *, tm=1024):
    m, d = x.shape
    assert m % tm == 0, m
    const = lambda i: (0, 0)
    row = lambda i: (i, 0)
    return pl.pallas_call(
        _merge_kernel,
        grid=(m // tm,),
        in_specs=[
            pl.BlockSpec((tm, d), row),
            pl.BlockSpec((1, d), const),
            pl.BlockSpec(w_g.shape, const),
            pl.BlockSpec(gate_b.shape, const),
            pl.BlockSpec((tm, ATTN_WIDTH), row),
            pl.BlockSpec((tm, ATTN_WIDTH), row),
            pl.BlockSpec((tm, MLSTM_WIDTH), row),
            pl.BlockSpec(wbd.shape, const),
            pl.BlockSpec(wbf.shape, const),
            pl.BlockSpec(wbm.shape, const),
            pl.BlockSpec(w_out.shape, const),
        ],
        out_specs=pl.BlockSpec((tm, d), row),
        out_shape=jax.ShapeDtypeStruct((m, d), F32),
        compiler_params=_params("parallel"),
        name="merge",
    )(x, norm_w, w_g, gate_b, o_diff, o_fox, o_mlstm, wbd, wbf, wbm, w_out)


def _alibi_bias(seq):
    slopes = np.array([2.0 ** (-8.0 * (h + 1) / DIFF_HEADS) for h in range(DIFF_HEADS)], dtype=np.float32)
    pos = np.arange(seq, dtype=np.float32)
    per_head = slopes[:, None] * pos[None, :]
    return jnp.asarray(np.repeat(per_head[:, None, :], 2, axis=1))


def kernel(x, ffn1_norm, ffn1_w_gate, ffn1_w_up, ffn1_w_down, mix_norm, w_in, gate_bias, diff_lq1, diff_lk1, diff_lq2, diff_lk2, diff_subln, fox_b_f, mlstm_conv_w, mlstm_conv_b, mlstm_wq, mlstm_wk, mlstm_wv, mlstm_w_if, mlstm_b_if, mlstm_norm, mlstm_skip, w_branch_diff, w_branch_fox, w_branch_mlstm, w_out, ffn2_norm, ffn2_w_gate, ffn2_w_up, ffn2_w_down, final_norm):
    batch, seq, d = x.shape
    depth = w_in.shape[0]
    m = batch * seq
    bf = lambda a: a.astype(BF16)
    row = lambda a: a.reshape(1, -1)
    aw = ATTN_WIDTH
    ff0 = 6 * aw
    mx0 = ff0 + FOX_HEADS
    g0 = mx0 + 2 * MLSTM_WIDTH
    blocks = aw // LANES
    alibi = _alibi_bias(seq)
    w_in_b = bf(w_in)

    x = x.reshape(m, d)
    for l in range(depth):
        x = _ffn(x, row(ffn1_norm[l]), ffn1_w_gate, ffn1_w_up, ffn1_w_down, l)

        w_in_l = w_in_b[l]
        w_vt = jnp.concatenate([w_in_l[:, 2 * aw:3 * aw], w_in_l[:, 5 * aw:6 * aw]], axis=1).T
        qk, vt, mxz, cf = _proj(x, row(mix_norm[l]), w_in_b, l, w_vt, w_in_l[:, mx0:g0],
                                w_in_l[:, ff0:mx0].T, fox_b_f[l].reshape(-1, 1), seq=seq)
        qk = qk.reshape(batch, seq, 4 * aw)
        lam_init = 0.8 - 0.6 * math.exp(-0.3 * l)
        o_diff = _attention(
            functools.partial(_diff_attn_kernel, lam_init=lam_init), qk, vt, alibi,
            [row(diff_lq1[l]), row(diff_lk1[l]), row(diff_lq2[l]), row(diff_lk2[l]), row(diff_subln[l])],
            q_col=0, v_row=0, v_rows=LANES, bias_per_batch=False, name="diff_attn")
        o_fox, o_mlstm = _fox_mlstm(
            qk, vt, cf.reshape(FOX_HEADS // 2, 2, m), mxz.reshape(batch, seq, 2 * MLSTM_WIDTH),
            mlstm_conv_w[l], row(mlstm_conv_b[l]), bf(jnp.swapaxes(mlstm_wq[l], 1, 2)), bf(mlstm_wk[l]),
            bf(jnp.swapaxes(mlstm_wv[l], 1, 2)), bf(mlstm_w_if[l].T), mlstm_b_if[l].reshape(-1, 1),
            row(mlstm_norm[l]), row(mlstm_skip[l]), q_col=2 * blocks, v_row=blocks)
        x = _merge(x, row(mix_norm[l]), w_in_l[:, g0:], gate_bias[l].reshape(1, -1),
                   o_diff.reshape(m, aw), o_fox.reshape(m, aw),
                   o_mlstm.reshape(m, MLSTM_WIDTH), bf(w_branch_diff[l]), bf(w_branch_fox[l]),
                   bf(w_branch_mlstm[l]), bf(w_out[l]))

        last = l == depth - 1
        x = _ffn(x, row(ffn2_norm[l]), ffn2_w_gate, ffn2_w_up, ffn2_w_down, l,
                 row(final_norm) if last else None)
    return x.reshape(batch, seq, d)
```

```python
import functools
import math

import numpy as np
import jax
import jax.numpy as jnp
from jax import lax
from jax.experimental import pallas as pl
from jax.experimental.pallas import tpu as pltpu

RMS_EPS = 1e-6
LN_EPS = 1e-5
DIFF_HEADS = 4
ATTN_HEAD_DIM = 64
FOX_HEADS = 8
MLSTM_HEADS = 4
MLSTM_HEAD_DIM = 128
MLSTM_WIDTH = MLSTM_HEADS * MLSTM_HEAD_DIM
MLSTM_CONV = 4
MLSTM_CHUNK = 128
ATTN_WIDTH = 512
LANES = 128
SUBLANES = 8
BF16_ROWS = 16
SCORE_SLOTS = 4
BIAS_TERMS = 3
VMEM_LIMIT_BYTES = 56 * 1024 * 1024
LOG2E = math.log2(math.e)
Q_SCALE = ATTN_HEAD_DIM ** -0.5 * LOG2E

F32 = jnp.float32
BF16 = jnp.bfloat16
NT_DIMS = (((1,), (1,)), ((), ()))


def _params(*semantics):
    return pltpu.CompilerParams(dimension_semantics=semantics, vmem_limit_bytes=VMEM_LIMIT_BYTES)


def _rms_norm(x, w):
    return x * lax.rsqrt(jnp.mean(x * x, axis=-1, keepdims=True) + RMS_EPS) * w


def _silu(x):
    half = 0.5 * x
    return half + half * jnp.tanh(half)


def _log_sigmoid(x):
    return jnp.minimum(x, 0.0) - jnp.log1p(jnp.exp(-jnp.abs(x)))


def _lane_cumsum(x):
    n = x.shape[-1]
    lane = lax.broadcasted_iota(jnp.int32, x.shape, 1)
    shift = 1
    while shift < n:
        x = x + jnp.where(lane >= shift, pltpu.roll(x, shift, axis=1), 0.0)
        shift *= 2
    return x


def _ffn_kernel(*refs, final):
    if final:
        x_ref, nw_ref, wg_ref, wu_ref, wd_ref, fn_ref, o_ref, xn_ref = refs
    else:
        x_ref, nw_ref, wg_ref, wu_ref, wd_ref, o_ref, xn_ref = refs
    f = pl.program_id(1)

    @pl.when(f == 0)
    def _():
        xn_ref[...] = _rms_norm(x_ref[...], nw_ref[...]).astype(BF16)
        o_ref[...] = jnp.zeros_like(o_ref)

    xn = xn_ref[...]
    g = jnp.dot(xn, wg_ref[...].astype(BF16), preferred_element_type=F32)
    u = jnp.dot(xn, wu_ref[...].astype(BF16), preferred_element_type=F32)
    h = (_silu(g) * u).astype(BF16)
    o_ref[...] += jnp.dot(h, wd_ref[...].astype(BF16), preferred_element_type=F32)

    @pl.when(f == pl.num_programs(1) - 1)
    def _():
        y = x_ref[...] + 0.5 * o_ref[...]
        if final:
            y = _rms_norm(y, fn_ref[...])
        o_ref[...] = y


def _ffn(x, norm_w, w_gate, w_up, w_down, layer, final_norm=None, *, tm=2048, tf=256):
    m, d = x.shape
    dff = w_gate.shape[2]
    assert m % tm == 0 and dff % tf == 0, (m, dff)
    final = final_norm is not None
    in_specs = [
        pl.BlockSpec((tm, d), lambda i, f: (i, 0)),
        pl.BlockSpec((1, d), lambda i, f: (0, 0)),
        pl.BlockSpec((None, d, tf), lambda i, f: (layer, 0, f)),
        pl.BlockSpec((None, d, tf), lambda i, f: (layer, 0, f)),
        pl.BlockSpec((None, tf, d), lambda i, f: (layer, f, 0)),
    ]
    args = [x, norm_w, w_gate, w_up, w_down]
    if final:
        in_specs.append(pl.BlockSpec((1, d), lambda i, f: (0, 0)))
        args.append(final_norm)
    return pl.pallas_call(
        functools.partial(_ffn_kernel, final=final),
        grid=(m // tm, dff // tf),
        in_specs=in_specs,
        out_specs=pl.BlockSpec((tm, d), lambda i, f: (i, 0)),
        out_shape=jax.ShapeDtypeStruct((m, d), F32),
        scratch_shapes=[pltpu.VMEM((tm, d), BF16)],
        compiler_params=_params("parallel", "arbitrary"),
        name="ffn_final" if final else "ffn",
    )(*args)


def _proj_kernel(x_ref, nw_ref, wdq_ref, wdk_ref, wfq_ref, wfk_ref, wvt_ref, wm_ref, wff_ref, bf_ref,
                 qk_ref, vt_ref, mxz_ref, cf_ref, carry_ref, *, tiles_per_seq):
    i = pl.program_id(0)

    @pl.when(i == 0)
    def _():
        carry_ref[...] = jnp.zeros_like(carry_ref)

    xn = _rms_norm(x_ref[...], nw_ref[...]).astype(BF16)

    ff = lax.dot_general(wff_ref[...], xn, NT_DIMS, preferred_element_type=F32)
    csum = _lane_cumsum(_log_sigmoid(ff + bf_ref[...]))
    csum = csum + jnp.where(i % tiles_per_seq == 0, 0.0, carry_ref[:, 0:1])
    carry_ref[...] = jnp.broadcast_to(csum[:, -1:], carry_ref.shape)
    cf_ref[...] = -csum

    for grp, w_ref in enumerate((wdq_ref, wdk_ref, wfq_ref, wfk_ref)):
        cols = slice(grp * ATTN_WIDTH, (grp + 1) * ATTN_WIDTH)
        r = jnp.dot(xn, w_ref[...], preferred_element_type=F32)
        if grp % 2 == 0:
            r = r * Q_SCALE
        qk_ref[:, cols] = r.astype(BF16)
    for grp in range(2):
        rows = slice(grp * ATTN_WIDTH, (grp + 1) * ATTN_WIDTH)
        vt_ref[rows, :] = lax.dot_general(wvt_ref[rows, :], xn, NT_DIMS,
                                          preferred_element_type=F32).astype(BF16)
    for grp in range(2):
        cols = slice(grp * MLSTM_WIDTH, (grp + 1) * MLSTM_WIDTH)
        mxz_ref[:, cols] = jnp.dot(xn, wm_ref[:, cols], preferred_element_type=F32)


def _proj(x, norm_w, w_in_b, layer, w_vt, w_m, w_ff_t, b_f, *, seq, tm=1024):
    m, d = x.shape
    nh = w_ff_t.shape[0]
    assert seq % tm == 0 and m % seq == 0, (m, seq)
    const = lambda i: (0, 0)
    qk_groups = (0, 1, 3, 4)
    qk_specs = [pl.BlockSpec((None, d, ATTN_WIDTH), functools.partial(lambda i, g: (layer, 0, g), g=g))
                for g in qk_groups]
    return pl.pallas_call(
        functools.partial(_proj_kernel, tiles_per_seq=seq // tm),
        grid=(m // tm,),
        in_specs=[
            pl.BlockSpec((tm, d), lambda i: (i, 0)),
            pl.BlockSpec((1, d), const),
            *qk_specs,
            pl.BlockSpec(w_vt.shape, const),
            pl.BlockSpec(w_m.shape, const),
            pl.BlockSpec(w_ff_t.shape, const),
            pl.BlockSpec((nh, 1), const),
        ],
        out_specs=[
            pl.BlockSpec((tm, len(qk_groups) * ATTN_WIDTH), lambda i: (i, 0)),
            pl.BlockSpec((w_vt.shape[0], tm), lambda i: (0, i)),
            pl.BlockSpec((tm, w_m.shape[1]), lambda i: (i, 0)),
            pl.BlockSpec((nh, tm), lambda i: (0, i)),
        ],
        out_shape=[
            jax.ShapeDtypeStruct((m, len(qk_groups) * ATTN_WIDTH), BF16),
            jax.ShapeDtypeStruct((w_vt.shape[0], m), BF16),
            jax.ShapeDtypeStruct((m, w_m.shape[1]), F32),
            jax.ShapeDtypeStruct((nh, m), F32),
        ],
        scratch_shapes=[pltpu.VMEM((nh, LANES), F32)],
        compiler_params=_params("arbitrary"),
        name="proj",
    )(x, norm_w, w_in_b, w_in_b, w_in_b, w_in_b, w_vt, w_m, w_ff_t, b_f)


def _flash_pair(q_ref, k_ref, vt_ref, bias_ref, kaug_ref, vaug_ref, qm_ref, s_ref, acc_ref, finish, *, tq, tk, v_rows,
                between=None):
    seq = k_ref.shape[1]
    dr = v_rows[0].stop - v_rows[0].start
    half = LANES // 2

    kaug_ref[:, 0:LANES] = k_ref[0]
    terms = []
    for c in range(2):
        rem = bias_ref[0, c:c + 1, :] * LOG2E
        for _ in range(BIAS_TERMS):
            part = rem.astype(BF16).astype(F32)
            terms.append(part)
            rem = rem - part
    term_rows = jnp.concatenate(terms + [jnp.zeros((LANES - 2 * BIAS_TERMS, seq), F32)], axis=0)
    for blk in range(seq // LANES):
        sl = slice(blk * LANES, (blk + 1) * LANES)
        kaug_ref[sl, LANES:] = term_rows[:, sl].T.astype(BF16)
    for c in range(2):
        vaug_ref[c, 0:dr, :] = vt_ref[v_rows[c], :]
        vaug_ref[c, dr:, :] = jnp.ones((BF16_ROWS, seq), BF16)

    qlane = lax.broadcasted_iota(jnp.int32, (seq, LANES), 1)
    q = q_ref[0]
    for c in range(2):
        qm_ref[c, :, 0:LANES] = jnp.where((qlane >= c * half) & (qlane < (c + 1) * half), q, jnp.zeros_like(q))
        picks_bias = (qlane >= BIAS_TERMS * c) & (qlane < BIAS_TERMS * (c + 1))
        qm_ref[c, :, LANES:] = jnp.where(picks_bias, 1.0, 0.0).astype(BF16)

    work = []
    for i in range(seq // tq):
        n_full = (i * tq) // tk
        lead = i * tq - n_full * tk
        work += [(i, j * tk, tk, None, j == 0, False) for j in range(n_full)]
        work.append((i, n_full * tk, lead + tq, lead, n_full == 0, True))

    def scores(n):
        i, k0, size, _, _, _ = work[n]
        for c in range(2):
            s_ref[n % s_ref.shape[0], c, 0:size, :] = lax.dot_general(
                kaug_ref[k0:k0 + size, :], qm_ref[c, i * tq:(i + 1) * tq, :], NT_DIMS,
                preferred_element_type=F32)

    scores(0)
    scores(1)
    m = [None, None]
    for n, (i, k0, size, lead, first, last) in enumerate(work):
        if n + 2 < len(work):
            scores(n + 2)
        for c in range(2):
            s = s_ref[n % s_ref.shape[0], c, 0:size, :]
            if lead is not None:
                key = lax.broadcasted_iota(jnp.int32, (tq, tq), 0)
                qry = lax.broadcasted_iota(jnp.int32, (tq, tq), 1)
                diag = jnp.where(key <= qry, s[lead:, :], -jnp.inf)
                s = diag if lead == 0 else jnp.concatenate([s[0:lead, :], diag], axis=0)
            m_blk = jnp.max(s, axis=0, keepdims=True)
            m_new = m_blk if first else jnp.maximum(m[c], m_blk)
            p = jnp.exp2(s - m_new).astype(BF16)
            pv = jnp.dot(vaug_ref[c, :, k0:k0 + size], p, preferred_element_type=F32)
            if first:
                acc_ref[i, c] = pv
            else:
                acc_ref[i, c] = jnp.exp2(m[c] - m_new) * acc_ref[i, c] + pv
            m[c] = m_new
        if last:
            finish(i, *[acc_ref[i, c, 0:dr, :] * (1.0 / acc_ref[i, c, dr:dr + 1, :]) for c in range(2)])
        if between is not None:
            between(n)


def _diff_attn_kernel(q_ref, k_ref, vt_ref, bias_ref, lq1_ref, lk1_ref, lq2_ref, lk2_ref, sub_ref,
                      o_ref, *scratch, tq, tk, lam_init):
    lam = (jnp.exp(jnp.sum(lq1_ref[...] * lk1_ref[...], axis=1, keepdims=True))
           - jnp.exp(jnp.sum(lq2_ref[...] * lk2_ref[...], axis=1, keepdims=True)) + lam_init)

    def finish(i, a1, a2):
        d = (a1 - lam * a2).T
        o_ref[0, i * tq:(i + 1) * tq, :] = (_rms_norm(d, sub_ref[...]) * (1.0 - lam_init)).astype(o_ref.dtype)

    full = slice(0, LANES)
    _flash_pair(q_ref, k_ref, vt_ref, bias_ref, *scratch, finish, tq=tq, tk=tk, v_rows=(full, full))


def _attention(kernel, qk, vt, bias, extra, *, q_col, v_row, v_rows, bias_per_batch, tq=256, tk=1024, name):
    b, s, _ = qk.shape
    assert s % tk == 0 and tk % tq == 0 and tq % LANES == 0, (s, tq, tk)
    groups = ATTN_WIDTH // LANES
    if bias_per_batch:
        bias_map = lambda bi, g: (g, 0, bi)
    else:
        bias_map = lambda bi, g: (g, 0, 0)
    in_specs = [
        pl.BlockSpec((1, s, LANES), lambda bi, g: (bi, 0, q_col + g)),
        pl.BlockSpec((1, s, LANES), lambda bi, g: (bi, 0, q_col + groups + g)),
        pl.BlockSpec((LANES, s), lambda bi, g: (v_row + g, bi)),
        pl.BlockSpec((1, 2, s), bias_map),
    ] + [pl.BlockSpec(e.shape, lambda bi, g: (0, 0)) for e in extra]
    rows = v_rows + BF16_ROWS
    return pl.pallas_call(
        functools.partial(kernel, tq=tq, tk=tk),
        grid=(b, groups),
        in_specs=in_specs,
        out_specs=pl.BlockSpec((1, s, LANES), lambda bi, g: (bi, 0, g)),
        out_shape=jax.ShapeDtypeStruct((b, s, ATTN_WIDTH), BF16),
        scratch_shapes=[pltpu.VMEM((s, 2 * LANES), BF16), pltpu.VMEM((2, rows, s), BF16),
                        pltpu.VMEM((2, s, 2 * LANES), BF16), pltpu.VMEM((SCORE_SLOTS, 2, tk, tq), F32),
                        pltpu.VMEM((s // tq, 2, rows, tq), F32)],
        compiler_params=_params("parallel", "parallel"),
        name=name,
    )(qk, qk, vt, bias, *extra)


def _mlstm_gates(xbuf_ref, k, cw_ref, cb_ref, wqt_ref, wk_ref, wvt_ref, wif_ref, bif_ref):
    L = MLSTM_CHUNK
    dh = MLSTM_HEAD_DIM
    nh = MLSTM_HEADS

    base = SUBLANES + k * L
    cur = xbuf_ref[base:base + L, :]
    conv = cb_ref[...]
    for j in range(MLSTM_CONV):
        off = base - (MLSTM_CONV - 1) + j
        conv = conv + cw_ref[j:j + 1, :] * xbuf_ref[off:off + L, :]
    xc = _silu(conv)

    qts, kss, vts = [], [], []
    ift = bif_ref[...]
    for h in range(nh):
        sl = slice(h * dh, (h + 1) * dh)
        xh = xc[:, sl].astype(BF16)
        vh = cur[:, sl].astype(BF16)
        qt = lax.dot_general(wqt_ref[h], xh, NT_DIMS, preferred_element_type=F32).astype(BF16)
        kk = jnp.dot(xh, wk_ref[h], preferred_element_type=F32)
        vt = lax.dot_general(wvt_ref[h], vh, NT_DIMS, preferred_element_type=F32).astype(BF16)
        ift = ift + jnp.dot(wif_ref[:, sl], qt, preferred_element_type=F32)
        ift = ift + lax.dot_general(wif_ref[:, MLSTM_WIDTH + h * dh:MLSTM_WIDTH + (h + 1) * dh],
                                    kk.astype(BF16), NT_DIMS, preferred_element_type=F32)
        ift = ift + jnp.dot(wif_ref[:, 2 * MLSTM_WIDTH + h * dh:2 * MLSTM_WIDTH + (h + 1) * dh], vt,
                            preferred_element_type=F32)
        qts.append(qt)
        kss.append((kk * (dh ** -0.5)).astype(BF16))
        vts.append(vt)

    key = lax.broadcasted_iota(jnp.int32, (L, L), 0)
    qry = lax.broadcasted_iota(jnp.int32, (L, L), 1)
    lf = _log_sigmoid(ift)
    tri = jnp.where(key <= qry, 1.0, 0.0).astype(BF16)
    b_all = jnp.zeros_like(lf)
    rem = lf
    for _ in range(3):
        part = rem.astype(BF16)
        b_all = b_all + jnp.dot(part, tri, preferred_element_type=F32)
        rem = rem - part.astype(F32)
    return dict(xc=xc, qts=qts, kss=kss, vts=vts, li_rows=ift[0:nh], b_rows=b_all[nh:2 * nh])


def _mlstm_recur(st, z_ref, nrm_ref, skip_ref, o_ref, ct_ref, n_ref, m_ref):
    L = MLSTM_CHUNK
    dh = MLSTM_HEAD_DIM
    key = lax.broadcasted_iota(jnp.int32, (L, L), 0)
    qry = lax.broadcasted_iota(jnp.int32, (L, L), 1)
    causal = key <= qry
    xc = st["xc"]

    heads = range(MLSTM_HEADS)
    qts, kss, vts = st["qts"], st["kss"], st["vts"]
    ct_state = [ct_ref[h] for h in heads]
    n_state = [n_ref[h] for h in heads]
    m_prev = [m_ref[h:h + 1, 0:1] for h in heads]
    qk = [jnp.dot(kss[h], qts[h], preferred_element_type=F32) for h in heads]
    cq = [jnp.dot(ct_state[h].astype(BF16), qts[h], preferred_element_type=F32) for h in heads]
    nq = [jnp.dot(n_state[h].astype(BF16), qts[h], preferred_element_type=F32)[0:1] for h in heads]

    li = [st["li_rows"][h:h + 1] for h in heads]
    b = [st["b_rows"][h:h + 1] for h in heads]
    a = [li[h] - b[h] for h in heads]
    a_keys = [jnp.broadcast_to(a[h], (L, L)).T for h in heads]
    d = [jnp.where(causal, b[h] + a_keys[h], -jnp.inf) for h in heads]
    inter = [b[h] + m_prev[h] for h in heads]
    m_t = [jnp.maximum(inter[h], jnp.max(d[h], axis=0, keepdims=True)) for h in heads]
    w_inter = [jnp.exp(inter[h] - m_t[h]) for h in heads]
    s = [qk[h] * jnp.exp(d[h] - m_t[h]) for h in heads]
    sv = [jnp.dot(vts[h], s[h].astype(BF16), preferred_element_type=F32) for h in heads]

    b_last = [b[h][:, L - 1:L] for h in heads]
    g = [b_last[h] + a[h] for h in heads]
    m_new = [jnp.maximum(b_last[h] + m_prev[h], jnp.max(g[h], axis=1, keepdims=True)) for h in heads]
    wk = [jnp.exp(g[h] - m_new[h]) for h in heads]
    dec = [jnp.exp(b_last[h] + m_prev[h] - m_new[h]) for h in heads]
    for h in heads:
        ct_ref[h] = dec[h] * ct_state[h] + jnp.dot((vts[h].astype(F32) * wk[h]).astype(BF16), kss[h],
                                                   preferred_element_type=F32)
        n_ref[h] = dec[h] * n_state[h] + jnp.dot(jnp.broadcast_to(wk[h], (SUBLANES, L)).astype(BF16),
                                                 kss[h], preferred_element_type=F32)
        m_ref[h:h + 1, :] = jnp.broadcast_to(m_new[h], (1, LANES))

    for h in heads:
        sl = slice(h * dh, (h + 1) * dh)
        num = w_inter[h] * cq[h] + sv[h]
        den = w_inter[h] * nq[h] + jnp.sum(s[h], axis=0, keepdims=True)
        hh = num * (1.0 / jnp.maximum(jnp.abs(den), jnp.exp(-m_t[h])))
        mu = jnp.mean(hh, axis=0, keepdims=True)
        cen = hh - mu
        var = jnp.mean(cen * cen, axis=0, keepdims=True)
        hn = (cen * lax.rsqrt(var + LN_EPS)).T * nrm_ref[:, sl]
        o_ref[:, sl] = ((hn + skip_ref[:, sl] * xc[:, sl]) * _silu(z_ref[:, sl])).astype(o_ref.dtype)


def _fox_mlstm_kernel(q_ref, k_ref, vt_ref, bias_ref, cur_ref, prev_ref, z_ref, cw_ref, cb_ref, wqt_ref, wk_ref,
                      wvt_ref, wif_ref, bif_ref, nrm_ref, skip_ref, of_ref, om_ref,
                      kaug_ref, vaug_ref, qm_ref, s_ref, acc_ref, xbuf_ref, ct_ref, n_ref, m_ref, *, tq, tk):
    g = pl.program_id(1)
    chunks = cur_ref.shape[1] // MLSTM_CHUNK

    @pl.when(g == 0)
    def _():
        ct_ref[...] = jnp.zeros_like(ct_ref)
        n_ref[...] = jnp.zeros_like(n_ref)
        m_ref[...] = jnp.zeros_like(m_ref)

    xbuf_ref[0:SUBLANES, :] = jnp.where(g > 0, prev_ref[0], 0.0)
    xbuf_ref[SUBLANES:, :] = cur_ref[0]

    refs = (cw_ref, cb_ref, wqt_ref, wk_ref, wvt_ref, wif_ref, bif_ref)
    staged = {}

    def gates(k):
        staged[k] = _mlstm_gates(xbuf_ref, k, *refs)

    def recur(k):
        rows = slice(k * MLSTM_CHUNK, (k + 1) * MLSTM_CHUNK)
        _mlstm_recur(staged.pop(k), z_ref.at[0, rows], nrm_ref, skip_ref, om_ref.at[0, rows], ct_ref, n_ref, m_ref)

    def stage(k):
        if k + 1 < chunks:
            gates(k + 1)
        recur(k)

    pending = [functools.partial(gates, 0)] + [functools.partial(stage, k) for k in range(chunks)]

    def between(n):
        if pending:
            pending.pop(0)()

    def finish(i, a_even, a_odd):
        of_ref[0, i * tq:(i + 1) * tq, :] = jnp.concatenate([a_even, a_odd], axis=0).T.astype(of_ref.dtype)

    half = LANES // 2
    _flash_pair(q_ref, k_ref, vt_ref, bias_ref, kaug_ref, vaug_ref, qm_ref, s_ref, acc_ref, finish, tq=tq, tk=tk,
                v_rows=(slice(0, half), slice(half, LANES)), between=between)
    while pending:
        pending.pop(0)()


def _fox_mlstm(qk, vt, bias, mxz, conv_w, conv_b, wq, wk, wv, w_if_t, b_if, norm_w, skip_w, *, q_col, v_row,
               tq=256, tk=1024):
    b, s, _ = qk.shape
    groups = ATTN_WIDTH // LANES
    span = s // groups
    assert s % tk == 0 and tk % tq == 0 and span % MLSTM_CHUNK == 0, (s, tq, tk)
    w = MLSTM_WIDTH
    rows_per_span = span // SUBLANES
    const2 = lambda bi, g: (0, 0)
    const3 = lambda bi, g: (0, 0, 0)
    rows = LANES // 2 + BF16_ROWS
    return pl.pallas_call(
        functools.partial(_fox_mlstm_kernel, tq=tq, tk=tk),
        grid=(b, groups),
        in_specs=[
            pl.BlockSpec((1, s, LANES), lambda bi, g: (bi, 0, q_col + g)),
            pl.BlockSpec((1, s, LANES), lambda bi, g: (bi, 0, q_col + groups + g)),
            pl.BlockSpec((LANES, s), lambda bi, g: (v_row + g, bi)),
            pl.BlockSpec((1, 2, s), lambda bi, g: (g, 0, bi)),
            pl.BlockSpec((1, span, w), lambda bi, g: (bi, g, 0)),
            pl.BlockSpec((1, SUBLANES, w), lambda bi, g: (bi, jnp.maximum(g * rows_per_span - 1, 0), 0)),
            pl.BlockSpec((1, span, w), lambda bi, g: (bi, g, 1)),
            pl.BlockSpec(conv_w.shape, const2),
            pl.BlockSpec(conv_b.shape, const2),
            pl.BlockSpec(wq.shape, const3),
            pl.BlockSpec(wk.shape, const3),
            pl.BlockSpec(wv.shape, const3),
            pl.BlockSpec(w_if_t.shape, const2),
            pl.BlockSpec(b_if.shape, const2),
            pl.BlockSpec(norm_w.shape, const2),
            pl.BlockSpec(skip_w.shape, const2),
        ],
        out_specs=[
            pl.BlockSpec((1, s, LANES), lambda bi, g: (bi, 0, g)),
            pl.BlockSpec((1, span, w), lambda bi, g: (bi, g, 0)),
        ],
        out_shape=[
            jax.ShapeDtypeStruct((b, s, ATTN_WIDTH), BF16),
            jax.ShapeDtypeStruct((b, s, w), BF16),
        ],
        scratch_shapes=[pltpu.VMEM((s, 2 * LANES), BF16), pltpu.VMEM((2, rows, s), BF16),
                        pltpu.VMEM((2, s, 2 * LANES), BF16), pltpu.VMEM((SCORE_SLOTS, 2, tk, tq), F32),
                        pltpu.VMEM((s // tq, 2, rows, tq), F32),
                        pltpu.VMEM((SUBLANES + span, w), F32),
                        pltpu.VMEM((MLSTM_HEADS, MLSTM_HEAD_DIM, MLSTM_HEAD_DIM), F32),
                        pltpu.VMEM((MLSTM_HEADS, SUBLANES, MLSTM_HEAD_DIM), F32),
                        pltpu.VMEM((SUBLANES, LANES), F32)],
        compiler_params=_params("parallel", "arbitrary"),
        name="fox_mlstm",
    )(qk, qk, vt, bias, mxz, mxz, mxz, conv_w, conv_b, wq, wk, wv, w_if_t, b_if, norm_w, skip_w)


def _merge_kernel(x_ref, nw_ref, wg_ref, gb_ref, od_ref, of_ref, om_ref, wbd_ref, wbf_ref, wbm_ref,
                  wo_ref, o_ref):
    x = x_ref[...]
    d = x.shape[1]
    xn = _rms_norm(x, nw_ref[...]).astype(BF16)
    merged = None
    for br, (b_ref, w_ref) in enumerate(((od_ref, wbd_ref), (of_ref, wbf_ref), (om_ref, wbm_ref))):
        cols = slice(br * d, (br + 1) * d)
        gate = jax.nn.sigmoid(jnp.dot(xn, wg_ref[:, cols], preferred_element_type=F32) + gb_ref[:, cols])
        term = gate * jnp.dot(b_ref[...], w_ref[...], preferred_element_type=F32)
        merged = term if merged is None else merged + term
    o_ref[...] = x + jnp.dot(merged.astype(BF16), wo_ref[...], preferred_element_type=F32)


def _merge(x, norm_w, w_g, gate_b, o_diff, o_fox, o_mlstm, wbd, wbf, wbm, w_out, *, tm=1024):
    m, d = x.shape
    assert m % tm == 0, m
    const = lambda i: (0, 0)
    row = lambda i: (i, 0)
    return pl.pallas_call(
        _merge_kernel,
        grid=(m // tm,),
        in_specs=[
            pl.BlockSpec((tm, d), row),
            pl.BlockSpec((1, d), const),
            pl.BlockSpec(w_g.shape, const),
            pl.BlockSpec(gate_b.shape, const),
            pl.BlockSpec((tm, ATTN_WIDTH), row),
            pl.BlockSpec((tm, ATTN_WIDTH), row),
            pl.BlockSpec((tm, MLSTM_WIDTH), row),
            pl.BlockSpec(wbd.shape, const),
            pl.BlockSpec(wbf.shape, const),
            pl.BlockSpec(wbm.shape, const),
            pl.BlockSpec(w_out.shape, const),
        ],
        out_specs=pl.BlockSpec((tm, d), row),
        out_shape=jax.ShapeDtypeStruct((m, d), F32),
        compiler_params=_params("parallel"),
        name="merge",
    )(x, norm_w, w_g, gate_b, o_diff, o_fox, o_mlstm, wbd, wbf, wbm, w_out)


def _alibi_bias(seq):
    slopes = np.array([2.0 ** (-8.0 * (h + 1) / DIFF_HEADS) for h in range(DIFF_HEADS)], dtype=np.float32)
    pos = np.arange(seq, dtype=np.float32)
    per_head = slopes[:, None] * pos[None, :]
    return jnp.asarray(np.repeat(per_head[:, None, :], 2, axis=1))


def kernel(x, ffn1_norm, ffn1_w_gate, ffn1_w_up, ffn1_w_down, mix_norm, w_in, gate_bias, diff_lq1, diff_lk1, diff_lq2, diff_lk2, diff_subln, fox_b_f, mlstm_conv_w, mlstm_conv_b, mlstm_wq, mlstm_wk, mlstm_wv, mlstm_w_if, mlstm_b_if, mlstm_norm, mlstm_skip, w_branch_diff, w_branch_fox, w_branch_mlstm, w_out, ffn2_norm, ffn2_w_gate, ffn2_w_up, ffn2_w_down, final_norm):
    batch, seq, d = x.shape
    depth = w_in.shape[0]
    m = batch * seq
    bf = lambda a: a.astype(BF16)
    row = lambda a: a.reshape(1, -1)
    aw = ATTN_WIDTH
    ff0 = 6 * aw
    mx0 = ff0 + FOX_HEADS
    g0 = mx0 + 2 * MLSTM_WIDTH
    blocks = aw // LANES
    alibi = _alibi_bias(seq)
    w_in_b = bf(w_in)

    x = x.reshape(m, d)
    for l in range(depth):
        x = _ffn(x, row(ffn1_norm[l]), ffn1_w_gate, ffn1_w_up, ffn1_w_down, l)

        w_in_l = w_in_b[l]
        w_vt = jnp.concatenate([w_in_l[:, 2 * aw:3 * aw], w_in_l[:, 5 * aw:6 * aw]], axis=1).T
        qk, vt, mxz, cf = _proj(x, row(mix_norm[l]), w_in_b, l, w_vt, w_in_l[:, mx0:g0],
                                w_in_l[:, ff0:mx0].T, fox_b_f[l].reshape(-1, 1), seq=seq)
        qk = qk.reshape(batch, seq, 4 * aw)
        lam_init = 0.8 - 0.6 * math.exp(-0.3 * l)
        o_diff = _attention(
            functools.partial(_diff_attn_kernel, lam_init=lam_init), qk, vt, alibi,
            [row(diff_lq1[l]), row(diff_lk1[l]), row(diff_lq2[l]), row(diff_lk2[l]), row(diff_subln[l])],
            q_col=0, v_row=0, v_rows=LANES, bias_per_batch=False, name="diff_attn")
        o_fox, o_mlstm = _fox_mlstm(
            qk, vt, cf.reshape(FOX_HEADS // 2, 2, m), mxz.reshape(batch, seq, 2 * MLSTM_WIDTH),
            mlstm_conv_w[l], row(mlstm_conv_b[l]), bf(jnp.swapaxes(mlstm_wq[l], 1, 2)), bf(mlstm_wk[l]),
            bf(jnp.swapaxes(mlstm_wv[l], 1, 2)), bf(mlstm_w_if[l].T), mlstm_b_if[l].reshape(-1, 1),
            row(mlstm_norm[l]), row(mlstm_skip[l]), q_col=2 * blocks, v_row=blocks)
        x = _merge(x, row(mix_norm[l]), w_in_l[:, g0:], gate_bias[l].reshape(1, -1),
                   o_diff.reshape(m, aw), o_fox.reshape(m, aw),
                   o_mlstm.reshape(m, MLSTM_WIDTH), bf(w_branch_diff[l]), bf(w_branch_fox[l]),
                   bf(w_branch_mlstm[l]), bf(w_out[l]))

        last = l == depth - 1
        x = _ffn(x, row(ffn2_norm[l]), ffn2_w_gate, ffn2_w_up, ffn2_w_down, l,
                 row(final_norm) if last else None)
    return x.reshape(batch, seq, d)
```
